```python
import jax, jax.numpy as jnp
from jax import lax
import numpy as np

D_MODEL = 4096
BATCH = 4
SEQ = 2048
DEPTH = 1
DEC_BATCH = 128
DEC_SEQ = 1
PAST_LEN = 16384
PAGE_SIZE = 128

MIX_WIDTH = D_MODEL
LRU_WIDTH = MIX_WIDTH // 2
LRU_BLOCKS = 16
LRU_BLOCK_DIM = LRU_WIDTH // LRU_BLOCKS
CONV_WIDTH = 4
RG_C = 8.0
GLA_HEADS = 4
GLA_DV_TOT = MIX_WIDTH - LRU_WIDTH
GLA_DK_TOT = GLA_DV_TOT // 2
GLA_DK = GLA_DK_TOT // GLA_HEADS
GLA_DV = GLA_DV_TOT // GLA_HEADS
GLA_GATE_RANK = 16
GLA_TAU = 16.0
GLA_CHUNK = 64
IN_COLS = 2 * LRU_WIDTH + 2 * GLA_DK_TOT + 2 * GLA_DV_TOT + GLA_GATE_RANK
PEER_HEADS = 8
PEER_N_KEYS = 128
PEER_N_EXPERTS = PEER_N_KEYS * PEER_N_KEYS
PEER_DQ = 256
PEER_TOPK = 16
PEER_BLOCK = 32
RMS_EPS = 1e-6

kernel_name = "hymba_rglru_gla_peer_adaln_step"


def _rmsnorm(x, g):
    xf = x.astype(jnp.float32)
    y = xf * lax.rsqrt(jnp.mean(xf * xf, axis=-1, keepdims=True) + RMS_EPS)
    return (y * g.astype(jnp.float32)).astype(x.dtype)


def _causal_conv(xb, buf, w, b):
    L = xb.shape[1]
    xp = jnp.concatenate([buf.astype(xb.dtype), xb], axis=1)
    out = b
    for j in range(CONV_WIDTH):
        out = out + xp[:, j:j + L] * w[j]
    return out, xp[:, L:]


def _lin_combine(e1, e2):
    a1, b1 = e1
    a2, b2 = e2
    return a1 * a2, a2 * b1 + b2


def _rglru(x, h0, w_a, b_a, w_x, b_x, lam):
    B, L, W = x.shape
    xb = x.reshape(B, L, LRU_BLOCKS, LRU_BLOCK_DIM)
    r = jax.nn.sigmoid((jnp.einsum('blnc,ncd->blnd', xb, w_a).reshape(B, L, W) + b_a).astype(jnp.float32))
    i = jax.nn.sigmoid((jnp.einsum('blnc,ncd->blnd', xb, w_x).reshape(B, L, W) + b_x).astype(jnp.float32))
    log_a = -RG_C * r * jax.nn.softplus(-lam.astype(jnp.float32))
    a = jnp.exp(log_a)
    u = jnp.sqrt(-jnp.expm1(2.0 * log_a)) * i * x.astype(jnp.float32)
    u = u.at[:, 0].add(a[:, 0] * h0.astype(jnp.float32))
    _, h = lax.associative_scan(_lin_combine, (a, u), axis=1)
    return h, h[:, -1]


def _gla(q, k, v, log_alpha, s0):
    B, L, H, DK = q.shape
    DV = v.shape[-1]
    C = min(GLA_CHUNK, L)
    n = -(-L // C)
    pad = n * C - L

    def prep(t):
        t = jnp.pad(t.astype(jnp.float32), ((0, 0), (0, pad), (0, 0), (0, 0)))
        return t.reshape(B, n, C, H, t.shape[-1]).transpose(1, 0, 3, 2, 4)

    qc, kc, vc, gc = prep(q), prep(k), prep(v), prep(log_alpha)
    causal = jnp.tril(jnp.ones((C, C), dtype=bool))

    def step(S, blk):
        qb, kb, vb, gb = blk
        b = jnp.cumsum(gb, axis=2)
        inter = jnp.einsum('bhtd,bhde->bhte', qb * jnp.exp(b), S)
        diff = b[:, :, :, None, :] - b[:, :, None, :, :]
        decay = jnp.where(causal[:, :, None], jnp.exp(jnp.minimum(diff, 0.0)), 0.0)
        att = jnp.einsum('bhtd,bhsd,bhtsd->bhts', qb, kb, decay)
        o = inter + jnp.einsum('bhts,bhse->bhte', att, vb)
        b_last = b[:, :, -1:, :]
        S = jnp.exp(b_last[:, :, 0, :])[..., None] * S + jnp.einsum('bhsd,bhse->bhde', kb * jnp.exp(b_last - b), vb)
        return S, o

    S, o = lax.scan(step, s0.astype(jnp.float32), (qc, kc, vc, gc))
    o = o.transpose(1, 0, 3, 2, 4).reshape(B, n * C, H, DV)[:, :L]
    return o, S


def _peer(x, w_q, sub_keys, u_tab, v_tab):
    B, L, D = x.shape
    T = B * L
    xf = x.reshape(T, D)
    q = (xf @ w_q).reshape(T, PEER_HEADS, 2, PEER_DQ // 2)
    s = jnp.einsum('thpd,hpnd->thpn', q, sub_keys)
    sv, si = lax.top_k(s, PEER_TOPK)
    cand = (sv[:, :, 0, :, None] + sv[:, :, 1, None, :]).reshape(T, PEER_HEADS, PEER_TOPK * PEER_TOPK)
    sc, pos = lax.top_k(cand, PEER_TOPK)
    i1 = jnp.take_along_axis(si[:, :, 0], pos // PEER_TOPK, axis=-1)
    i2 = jnp.take_along_axis(si[:, :, 1], pos % PEER_TOPK, axis=-1)
    experts = i1 * PEER_N_KEYS + i2
    g = jax.nn.softmax(sc.astype(jnp.float32), axis=-1).astype(x.dtype)
    nblk = -(-T // PEER_BLOCK)
    pad = nblk * PEER_BLOCK - T
    xp = jnp.pad(xf, ((0, pad), (0, 0))).reshape(nblk, PEER_BLOCK, D)
    ep = jnp.pad(experts, ((0, pad), (0, 0), (0, 0))).reshape(nblk, PEER_BLOCK, PEER_HEADS, PEER_TOPK)
    gp = jnp.pad(g, ((0, pad), (0, 0), (0, 0))).reshape(nblk, PEER_BLOCK, PEER_HEADS, PEER_TOPK)

    def blk(args):
        xb, eb, gb = args
        act = jax.nn.gelu(jnp.einsum('td,thkd->thk', xb, u_tab[eb]))
        return jnp.einsum('thk,thkd->td', gb * act, v_tab[eb])

    y = lax.map(blk, (xp, ep, gp)).reshape(nblk * PEER_BLOCK, D)[:T]
    return y.reshape(B, L, D)


def _layer(x, c, conv_buf, h0, s0, w_ada, b_ada, g_mix, w_in, conv_w, conv_b, lru_w_a, lru_b_a,
           lru_w_x, lru_b_x, lru_lambda, gla_w_alpha, gla_b_alpha, gla_g_norm, w_out, g_ffn,
           peer_w_q, peer_sub_keys, peer_u, peer_v):
    B, L, _ = x.shape
    ada = jax.nn.silu(c) @ w_ada + b_ada
    sh1, sc1, gt1, sh2, sc2, gt2 = [t[:, None, :] for t in jnp.split(ada, 6, axis=-1)]
    hn = _rmsnorm(x, g_mix) * (1.0 + sc1) + sh1
    proj = hn @ w_in
    sizes = [LRU_WIDTH, LRU_WIDTH, GLA_DK_TOT, GLA_DK_TOT, GLA_DV_TOT, GLA_DV_TOT]
    idx = []
    acc = 0
    for s_ in sizes:
        acc += s_
        idx.append(acc)
    xb, gb, q, k, v, r, a_lr = jnp.split(proj, idx, axis=-1)
    xc, new_conv = _causal_conv(xb, conv_buf, conv_w, conv_b)
    hseq, h_last = _rglru(xc, h0, lru_w_a, lru_b_a, lru_w_x, lru_b_x, lru_lambda)
    y_lru = hseq.astype(x.dtype) * jax.nn.gelu(gb)
    q = q.reshape(B, L, GLA_HEADS, GLA_DK) * (GLA_DK ** -0.5)
    k = k.reshape(B, L, GLA_HEADS, GLA_DK)
    v = v.reshape(B, L, GLA_HEADS, GLA_DV)
    log_alpha = jax.nn.log_sigmoid((a_lr @ gla_w_alpha + gla_b_alpha).astype(jnp.float32)) / GLA_TAU
    o, s_new = _gla(q, k, v, log_alpha.reshape(B, L, GLA_HEADS, GLA_DK), s0)
    o = _rmsnorm(o.astype(x.dtype), gla_g_norm.reshape(GLA_HEADS, GLA_DV))
    y_gla = (o * jax.nn.silu(r).reshape(B, L, GLA_HEADS, GLA_DV)).reshape(B, L, GLA_DV_TOT)
    mix = jnp.concatenate([y_lru, y_gla], axis=-1) @ w_out
    x = x + gt1 * mix
    hn2 = _rmsnorm(x, g_ffn) * (1.0 + sc2) + sh2
    x = x + gt2 * _peer(hn2, peer_w_q, peer_sub_keys, peer_u, peer_v)
    return x, new_conv, h_last.astype(x.dtype), s_new.astype(x.dtype)


def setup_inputs(seed: int = 0) -> dict:
    key = jax.random.key(seed)
    ks = jax.random.split(key, 32)
    f32 = jnp.float32
    D = D_MODEL

    def nrm(k, shape, scale):
        return jax.random.normal(k, shape, f32) * scale

    a0 = jax.random.uniform(ks[12], (DEPTH, LRU_WIDTH), f32, 0.9, 0.999)
    p = a0 ** (1.0 / RG_C)
    lam = jnp.log(p) - jnp.log1p(-p)
    return {
        'x_prompt': nrm(ks[0], (BATCH, SEQ, D), 1.0),
        'x_sample': nrm(ks[1], (DEC_BATCH, DEC_SEQ, D), 1.0),
        'state_conv': nrm(ks[2], (DEPTH, DEC_BATCH, CONV_WIDTH - 1, LRU_WIDTH), 1.0),
        'state_lru_h': nrm(ks[3], (DEPTH, DEC_BATCH, LRU_WIDTH), 0.5),
        'state_gla': nrm(ks[4], (DEPTH, DEC_BATCH, GLA_HEADS, GLA_DK, GLA_DV), 1.0),
        'c_prompt': nrm(ks[5], (BATCH, D), 1.0),
        'c_sample': nrm(ks[6], (DEC_BATCH, D), 1.0),
        'w_ada': nrm(ks[7], (DEPTH, D, 6 * D), 0.5 * D ** -0.5),
        'b_ada': nrm(ks[8], (DEPTH, 6 * D), 0.02),
        'g_mix': 1.0 + nrm(ks[9], (DEPTH, D), 0.02),
        'w_in': nrm(ks[10], (DEPTH, D, IN_COLS), D ** -0.5),
        'conv_w': nrm(ks[11], (DEPTH, CONV_WIDTH, LRU_WIDTH), CONV_WIDTH ** -0.5),
        'conv_b': nrm(ks[13], (DEPTH, LRU_WIDTH), 0.02),
        'lru_w_a': nrm(ks[14], (DEPTH, LRU_BLOCKS, LRU_BLOCK_DIM, LRU_BLOCK_DIM), LRU_BLOCK_DIM ** -0.5),
        'lru_b_a': nrm(ks[15], (DEPTH, LRU_WIDTH), 0.02),
        'lru_w_x': nrm(ks[16], (DEPTH, LRU_BLOCKS, LRU_BLOCK_DIM, LRU_BLOCK_DIM), LRU_BLOCK_DIM ** -0.5),
        'lru_b_x': nrm(ks[17], (DEPTH, LRU_WIDTH), 0.02),
        'lru_lambda': lam,
        'gla_w_alpha': nrm(ks[18], (DEPTH, GLA_GATE_RANK, GLA_DK_TOT), GLA_GATE_RANK ** -0.5),
        'gla_b_alpha': nrm(ks[19], (DEPTH, GLA_DK_TOT), 0.1),
        'gla_g_norm': 1.0 + nrm(ks[20], (DEPTH, GLA_DV_TOT), 0.02),
        'w_out': nrm(ks[21], (DEPTH, MIX_WIDTH, D), MIX_WIDTH ** -0.5),
        'g_ffn': 1.0 + nrm(ks[22], (DEPTH, D), 0.02),
        'peer_w_q': nrm(ks[23], (DEPTH, D, PEER_HEADS * PEER_DQ), D ** -0.5),
        'peer_sub_keys': nrm(ks[24], (DEPTH, PEER_HEADS, 2, PEER_N_KEYS, PEER_DQ // 2), (PEER_DQ // 2) ** -0.5),
        'peer_u': nrm(ks[25], (DEPTH, PEER_N_EXPERTS, D), D ** -0.5),
        'peer_v': nrm(ks[26], (DEPTH, PEER_N_EXPERTS, D), 0.25),
        'g_final': 1.0 + nrm(ks[27], (D,), 0.02),
    }


def reference(x_prompt, x_sample, state_conv, state_lru_h, state_gla, c_prompt, c_sample,
              w_ada, b_ada, g_mix, w_in, conv_w, conv_b, lru_w_a, lru_b_a, lru_w_x, lru_b_x,
              lru_lambda, gla_w_alpha, gla_b_alpha, gla_g_norm, w_out, g_ffn, peer_w_q,
              peer_sub_keys, peer_u, peer_v, g_final):
    x_p, x_s = x_prompt, x_sample
    Bp = x_p.shape[0]
    conv_p_l, h_p_l, s_p_l, conv_s_l, h_s_l, s_s_l = [], [], [], [], [], []
    for l in range(DEPTH):
        lw = (w_ada[l], b_ada[l], g_mix[l], w_in[l], conv_w[l], conv_b[l], lru_w_a[l], lru_b_a[l],
              lru_w_x[l], lru_b_x[l], lru_lambda[l], gla_w_alpha[l], gla_b_alpha[l], gla_g_norm[l],
              w_out[l], g_ffn[l], peer_w_q[l], peer_sub_keys[l], peer_u[l], peer_v[l])
        zc = jnp.zeros((Bp, CONV_WIDTH - 1, LRU_WIDTH), x_p.dtype)
        zh = jnp.zeros((Bp, LRU_WIDTH), x_p.dtype)
        zs = jnp.zeros((Bp, GLA_HEADS, GLA_DK, GLA_DV), x_p.dtype)
        x_p, cp, hp, sp = _layer(x_p, c_prompt, zc, zh, zs, *lw)
        x_s, cs, hs, ss = _layer(x_s, c_sample, state_conv[l], state_lru_h[l], state_gla[l], *lw)
        conv_p_l.append(cp)
        h_p_l.append(hp)
        s_p_l.append(sp)
        conv_s_l.append(cs)
        h_s_l.append(hs)
        s_s_l.append(ss)
    y_prompt = _rmsnorm(x_p, g_final)
    y_sample = _rmsnorm(x_s, g_final)
    return (y_prompt, y_sample, jnp.stack(conv_p_l), jnp.stack(h_p_l), jnp.stack(s_p_l),
            jnp.stack(conv_s_l), jnp.stack(h_s_l), jnp.stack(s_s_l))
```

```python
import functools

import jax
import jax.numpy as jnp
from jax import lax
from jax.experimental import pallas as pl
from jax.experimental.pallas import tpu as pltpu

F32 = jnp.float32
BF16 = jnp.bfloat16

RMS_EPS = 1e-6
RG_C = 8.0
GLA_TAU = 16.0
PEER_TOPK = 16
GLA_CHUNK = 128
GLA_SUB = 16
LANES = 128
SUBLANES = 8
VMEM_LIMIT = 56 * 1024 * 1024


def _params(*sem):
    return pltpu.CompilerParams(dimension_semantics=sem, vmem_limit_bytes=VMEM_LIMIT)


def _pick(n, pref, mult=SUBLANES):
    best = None
    for d in range(mult, min(n, pref) + 1, mult):
        if n % d == 0:
            best = d
    return n if best is None else best


def _softplus(y):
    return jnp.maximum(y, 0.0) + jnp.log1p(jnp.exp(-jnp.abs(y)))


def _log_sigmoid(z):
    return jnp.minimum(z, 0.0) - jnp.log1p(jnp.exp(-jnp.abs(z)))


def _row_to_col(row, n):
    r = lax.broadcasted_iota(jnp.int32, (n, n), 0)
    c = lax.broadcasted_iota(jnp.int32, (n, n), 1)
    return jnp.sum(jnp.where(r == c, jnp.broadcast_to(row, (n, n)), 0.0), axis=1, keepdims=True)


def _ada_kernel(c_ref, w_ref, b_ref, o_ref):
    a = jax.nn.silu(c_ref[...]).astype(BF16)
    o_ref[...] = jnp.dot(a, w_ref[...].astype(BF16), preferred_element_type=F32) + b_ref[...]


def _ada(c, w, b):
    R, D = c.shape
    N = w.shape[1]
    tn = _pick(N, 512, LANES)
    return pl.pallas_call(
        _ada_kernel,
        grid=(N // tn,),
        in_specs=[pl.BlockSpec((R, D), lambda j: (0, 0)),
                  pl.BlockSpec((D, tn), lambda j: (0, j)),
                  pl.BlockSpec((1, tn), lambda j: (0, j))],
        out_specs=pl.BlockSpec((R, tn), lambda j: (0, j)),
        out_shape=jax.ShapeDtypeStruct((R, N), F32),
        compiler_params=_params("arbitrary"),
    )(c, w, b)


def _norm_mod_kernel(x_ref, g_ref, sc_ref, sh_ref, o_ref):
    x = x_ref[0]
    y = x * lax.rsqrt(jnp.mean(x * x, axis=-1, keepdims=True) + RMS_EPS) * g_ref[...]
    o_ref[0] = (y * (1.0 + sc_ref[0]) + sh_ref[0]).astype(o_ref.dtype)


def _mod_spec(m, tl, D):
    if m.shape[1] == 1:
        return pl.BlockSpec((1, 1, D), lambda b, l: (b, 0, 0))
    return pl.BlockSpec((1, tl, D), lambda b, l: (b, l, 0))


def _norm_mod(x, g, sc, sh, out_dtype):
    B, L, D = x.shape
    tl = _pick(L, 256)
    return pl.pallas_call(
        _norm_mod_kernel,
        grid=(B, L // tl),
        in_specs=[pl.BlockSpec((1, tl, D), lambda b, l: (b, l, 0)),
                  pl.BlockSpec((1, D), lambda b, l: (0, 0)),
                  _mod_spec(sc, tl, D), _mod_spec(sh, tl, D)],
        out_specs=pl.BlockSpec((1, tl, D), lambda b, l: (b, l, 0)),
        out_shape=jax.ShapeDtypeStruct((B, L, D), out_dtype),
        compiler_params=_params("arbitrary", "arbitrary"),
    )(x, g, sc, sh)


def _mm_kernel(x_ref, w_ref, o_ref):
    o_ref[...] = jnp.dot(x_ref[...], w_ref[...], preferred_element_type=F32)


def _matmul(x, w, tm_pref=512, tn_pref=1024):
    M, K = x.shape
    N = w.shape[1]
    tm = _pick(M, tm_pref)
    tn = _pick(N, tn_pref, LANES)
    return pl.pallas_call(
        _mm_kernel,
        grid=(N // tn, M // tm),
        in_specs=[pl.BlockSpec((tm, K), lambda j, i: (i, 0)),
                  pl.BlockSpec((K, tn), lambda j, i: (0, j))],
        out_specs=pl.BlockSpec((tm, tn), lambda j, i: (i, j)),
        out_shape=jax.ShapeDtypeStruct((M, N), F32),
        compiler_params=_params("arbitrary", "arbitrary"),
    )(x, w)


def _gate_kernel(x_ref, w1_ref, w2_ref, b2_ref, o_ref):
    a = jnp.dot(x_ref[...], w1_ref[...], preferred_element_type=F32)
    z = jnp.dot(a.astype(BF16), w2_ref[...], preferred_element_type=F32) + b2_ref[...]
    o_ref[...] = _log_sigmoid(z) / GLA_TAU


def _gla_gate(hn, w1, w2, b2):
    M, K = hn.shape
    R = w1.shape[1]
    N = w2.shape[1]
    tm = _pick(M, 512)
    return pl.pallas_call(
        _gate_kernel,
        grid=(M // tm,),
        in_specs=[pl.BlockSpec((tm, K), lambda i: (i, 0)),
                  pl.BlockSpec((K, R), lambda i: (0, 0)),
                  pl.BlockSpec((R, N), lambda i: (0, 0)),
                  pl.BlockSpec((1, N), lambda i: (0, 0))],
        out_specs=pl.BlockSpec((tm, N), lambda i: (i, 0)),
        out_shape=jax.ShapeDtypeStruct((M, N), F32),
        compiler_params=_params("arbitrary"),
    )(hn, w1, w2, b2)


def _lru_gates(xc, wa_ref, ba, wx_ref, bx, lam):
    nb, bd = wa_ref.shape[0], wa_ref.shape[1]
    xcb = xc.astype(BF16)
    ra, ri = [], []
    for n in range(nb):
        xs = xcb[:, n * bd:(n + 1) * bd]
        ra.append(jnp.dot(xs, wa_ref[n], preferred_element_type=F32))
        ri.append(jnp.dot(xs, wx_ref[n], preferred_element_type=F32))
    r = jax.nn.sigmoid(jnp.concatenate(ra, axis=1) + ba)
    i = jax.nn.sigmoid(jnp.concatenate(ri, axis=1) + bx)
    log_a = -RG_C * r * _softplus(-lam)
    a = jnp.exp(log_a)
    u = jnp.sqrt(-jnp.tanh(log_a) * (a * a + 1.0)) * i * xc
    return a, u


def _lru_prompt_kernel(xb_ref, gb_ref, tail_ref, h0_ref, cw_ref, cb_ref, wa_ref, ba_ref, wx_ref,
                       bx_ref, lam_ref, y_ref, hl_ref, conv_ref, buf, hcar):
    l = pl.program_id(2)
    nl = pl.num_programs(2)
    tl, tw = xb_ref.shape[1], xb_ref.shape[2]
    nt = tail_ref.shape[1]
    base = SUBLANES

    @pl.when(l == 0)
    def _():
        buf[base - nt:base, :] = tail_ref[0]
        hcar[...] = h0_ref[0]

    xb = xb_ref[0]
    buf[base:base + tl, :] = xb
    cw = cw_ref[...]
    xc = cb_ref[...] + xb * cw[nt:nt + 1]
    for j in range(nt):
        xc = xc + buf[base - nt + j:base - nt + j + tl, :] * cw[j:j + 1]
    new_tail = buf[base + tl - nt:base + tl, :]

    a, u = _lru_gates(xc, wa_ref, ba_ref[...], wx_ref, bx_ref[...], lam_ref[...])

    row = lax.broadcasted_iota(jnp.int32, (tl, tw), 0) & (SUBLANES - 1)
    A, Bv = a, u
    s = 1
    while s < SUBLANES:
        As = pltpu.roll(A, s, axis=0)
        Bs = pltpu.roll(Bv, s, axis=0)
        m = row >= s
        Bv = jnp.where(m, A * Bs + Bv, Bv)
        A = jnp.where(m, A * As, A)
        s *= 2
    h = hcar[...]
    outs = []
    for g in range(tl // SUBLANES):
        hg = A[g * SUBLANES:(g + 1) * SUBLANES] * h + Bv[g * SUBLANES:(g + 1) * SUBLANES]
        outs.append(hg)
        h = hg[SUBLANES - 1:SUBLANES]
    hseq = jnp.concatenate(outs, axis=0)
    hcar[...] = h
    buf[base - nt:base, :] = new_tail
    y_ref[0] = (hseq * jax.nn.gelu(gb_ref[0])).astype(y_ref.dtype)

    @pl.when(l == nl - 1)
    def _():
        hl_ref[0] = h
        conv_ref[0] = new_tail


def _lru_prompt(proj, tail, h0, cw, cb, wa, ba, wx, bx, lam, W):
    B, L, _ = proj.shape
    nt = tail.shape[1]
    bd = wa.shape[1]
    tw = _pick(W, 512, bd)
    tl = _pick(L, 256)
    nwb = W // tw
    col = lambda b, j, l: (0, j)
    return pl.pallas_call(
        _lru_prompt_kernel,
        grid=(B, nwb, L // tl),
        in_specs=[pl.BlockSpec((1, tl, tw), lambda b, j, l: (b, l, j)),
                  pl.BlockSpec((1, tl, tw), lambda b, j, l: (b, l, j + nwb)),
                  pl.BlockSpec((1, nt, tw), lambda b, j, l: (b, 0, j)),
                  pl.BlockSpec((1, 1, tw), lambda b, j, l: (b, 0, j)),
                  pl.BlockSpec((nt + 1, tw), col),
                  pl.BlockSpec((1, tw), col),
                  pl.BlockSpec((tw // bd, bd, bd), lambda b, j, l: (j, 0, 0)),
                  pl.BlockSpec((1, tw), col),
                  pl.BlockSpec((tw // bd, bd, bd), lambda b, j, l: (j, 0, 0)),
                  pl.BlockSpec((1, tw), col),
                  pl.BlockSpec((1, tw), col)],
        out_specs=[pl.BlockSpec((1, tl, tw), lambda b, j, l: (b, l, j)),
                   pl.BlockSpec((1, 1, tw), lambda b, j, l: (b, 0, j)),
                   pl.BlockSpec((1, nt, tw), lambda b, j, l: (b, 0, j))],
        out_shape=[jax.ShapeDtypeStruct((B, L, W), BF16),
                   jax.ShapeDtypeStruct((B, 1, W), F32),
                   jax.ShapeDtypeStruct((B, nt, W), F32)],
        scratch_shapes=[pltpu.VMEM((tl + SUBLANES, tw), F32), pltpu.VMEM((1, tw), F32)],
        compiler_params=_params("arbitrary", "arbitrary", "arbitrary"),
    )(proj, proj, tail, h0, cw, cb, wa, ba, wx, bx, lam)


def _lru_decode_kernel(xb_ref, gb_ref, tail_ref, h0_ref, cw_ref, cb_ref, wa_ref, ba_ref, wx_ref,
                       bx_ref, lam_ref, y_ref, h_ref):
    nt = tail_ref.shape[0]
    xb = xb_ref[...]
    cw = cw_ref[...]
    xc = cb_ref[...] + xb * cw[nt:nt + 1]
    for j in range(nt):
        xc = xc + tail_ref[j] * cw[j:j + 1]
    a, u = _lru_gates(xc, wa_ref, ba_ref[...], wx_ref, bx_ref[...], lam_ref[...])
    h = a * h0_ref[...] + u
    h_ref[...] = h
    y_ref[...] = (h * jax.nn.gelu(gb_ref[...])).astype(y_ref.dtype)


def _lru_decode(proj, tail, h0, cw, cb, wa, ba, wx, bx, lam, W):
    B = proj.shape[0]
    nt = tail.shape[0]
    bd = wa.shape[1]
    tw = _pick(W, 512, bd)
    nwb = W // tw
    col = lambda j: (0, j)
    return pl.pallas_call(
        _lru_decode_kernel,
        grid=(nwb,),
        in_specs=[pl.BlockSpec((B, tw), col),
                  pl.BlockSpec((B, tw), lambda j: (0, j + nwb)),
                  pl.BlockSpec((nt, B, tw), lambda j: (0, 0, j)),
                  pl.BlockSpec((B, tw), col),
                  pl.BlockSpec((nt + 1, tw), col),
                  pl.BlockSpec((1, tw), col),
                  pl.BlockSpec((tw // bd, bd, bd), lambda j: (j, 0, 0)),
                  pl.BlockSpec((1, tw), col),
                  pl.BlockSpec((tw // bd, bd, bd), lambda j: (j, 0, 0)),
                  pl.BlockSpec((1, tw), col),
                  pl.BlockSpec((1, tw), col)],
        out_specs=[pl.BlockSpec((B, tw), col), pl.BlockSpec((B, tw), col)],
        out_shape=[jax.ShapeDtypeStruct((B, W), BF16), jax.ShapeDtypeStruct((B, W), F32)],
        compiler_params=_params("arbitrary"),
    )(proj, proj, tail, h0, cw, cb, wa, ba, wx, bx, lam)


def _head_norm_gate(o, gn, r):
    on = o * lax.rsqrt(jnp.mean(o * o, axis=-1, keepdims=True) + RMS_EPS) * gn
    return on * jax.nn.silu(r)


def _gla_prompt_kernel(q_ref, k_ref, v_ref, r_ref, la_ref, s0_ref, gn_ref, y_ref, so_ref, S, att):
    c = pl.program_id(2)
    nc = pl.num_programs(2)
    C, DK = q_ref.shape[1], q_ref.shape[2]
    sub = min(GLA_SUB, C)

    @pl.when(c == 0)
    def _():
        S[...] = s0_ref[0, 0]

    q = q_ref[0] * (DK ** -0.5)
    k = k_ref[0]
    vb = v_ref[0].astype(BF16)
    g = la_ref[0]
    rr = lax.broadcasted_iota(jnp.int32, (C, C), 0)
    cc = lax.broadcasted_iota(jnp.int32, (C, C), 1)
    b = jnp.dot((rr >= cc).astype(F32), g, precision=lax.Precision.HIGHEST,
                preferred_element_type=F32)
    b_last = b[C - 1:C]

    inter = jnp.dot((q * jnp.exp(b)).astype(BF16), S[...].astype(BF16), preferred_element_type=F32)

    lane = lax.broadcasted_iota(jnp.int32, (sub, sub), 1)
    srow = lax.broadcasted_iota(jnp.int32, (sub, sub), 0)
    for I in range(C // sub):
        lo = I * sub
        qI, kI, bI = q[lo:lo + sub], k[lo:lo + sub], b[lo:lo + sub]
        acc = jnp.zeros((sub, sub), F32)
        for s in range(sub):
            w = qI * kI[s:s + 1] * jnp.exp(jnp.minimum(bI - bI[s:s + 1], 0.0))
            acc = jnp.where(lane == s, jnp.sum(w, axis=1, keepdims=True), acc)
        att[lo:lo + sub, lo:lo + sub] = jnp.where(srow >= lane, acc, 0.0)
        if lo > 0:
            ref = b[lo - 1:lo]
            qt = (qI * jnp.exp(bI - ref)).astype(BF16)
            kt = (k[0:lo] * jnp.exp(ref - b[0:lo])).astype(BF16)
            att[lo:lo + sub, 0:lo] = lax.dot_general(qt, kt, (((1,), (1,)), ((), ())),
                                                     preferred_element_type=F32)
        if lo + sub < C:
            att[lo:lo + sub, lo + sub:C] = jnp.zeros((sub, C - lo - sub), F32)
    o = inter + jnp.dot(att[...].astype(BF16), vb, preferred_element_type=F32)

    kh = (k * jnp.exp(b_last - b)).astype(BF16)
    upd = lax.dot_general(kh, vb, (((0,), (0,)), ((), ())), preferred_element_type=F32)
    S[...] = _row_to_col(jnp.exp(b_last), DK) * S[...] + upd

    y_ref[0] = _head_norm_gate(o, gn_ref[...], r_ref[0]).astype(y_ref.dtype)

    @pl.when(c == nc - 1)
    def _():
        so_ref[0, 0] = S[...]


def _gla_prompt(proj, la, s0, gn, offs):
    B, L, _ = proj.shape
    _, H, DK, DV = s0.shape
    C = _pick(L, GLA_CHUNK)
    qo, ko, vo, ro = offs
    return pl.pallas_call(
        _gla_prompt_kernel,
        grid=(B, H, L // C),
        in_specs=[pl.BlockSpec((1, C, DK), lambda b, h, c: (b, c, qo // DK + h)),
                  pl.BlockSpec((1, C, DK), lambda b, h, c: (b, c, ko // DK + h)),
                  pl.BlockSpec((1, C, DV), lambda b, h, c: (b, c, vo // DV + h)),
                  pl.BlockSpec((1, C, DV), lambda b, h, c: (b, c, ro // DV + h)),
                  pl.BlockSpec((1, C, DK), lambda b, h, c: (b, c, h)),
                  pl.BlockSpec((1, 1, DK, DV), lambda b, h, c: (b, h, 0, 0)),
                  pl.BlockSpec((1, DV), lambda b, h, c: (0, h))],
        out_specs=[pl.BlockSpec((1, C, DV), lambda b, h, c: (b, c, h)),
                   pl.BlockSpec((1, 1, DK, DV), lambda b, h, c: (b, h, 0, 0))],
        out_shape=[jax.ShapeDtypeStruct((B, L, H * DV), BF16),
                   jax.ShapeDtypeStruct((B, H, DK, DV), F32)],
        scratch_shapes=[pltpu.VMEM((DK, DV), F32), pltpu.VMEM((C, C), F32)],
        compiler_params=_params("arbitrary", "arbitrary", "arbitrary"),
    )(proj, proj, proj, proj, la, s0, gn)


def _gla_decode_kernel(q_ref, k_ref, v_ref, r_ref, la_ref, s0_ref, gn_ref, y_ref, so_ref):
    DK = q_ref.shape[3]
    q = q_ref[0, 0] * (DK ** -0.5)
    S = (_row_to_col(jnp.exp(la_ref[0, 0]), DK) * s0_ref[0, 0]
         + _row_to_col(k_ref[0, 0], DK) * v_ref[0, 0])
    so_ref[0, 0] = S
    o = jnp.sum(_row_to_col(q, DK) * S, axis=0, keepdims=True)
    y_ref[0, 0] = _head_norm_gate(o, gn_ref[0], r_ref[0, 0]).astype(y_ref.dtype)


def _gla_decode(q, k, v, r, la, s0, gn):
    B, H, DK, DV = s0.shape
    vec = lambda d: pl.BlockSpec((1, 1, 1, d), lambda b, h: (b, h, 0, 0))
    mat = pl.BlockSpec((1, 1, DK, DV), lambda b, h: (b, h, 0, 0))
    return pl.pallas_call(
        _gla_decode_kernel,
        grid=(B, H),
        in_specs=[vec(DK), vec(DK), vec(DV), vec(DV), vec(DK), mat,
                  pl.BlockSpec((1, 1, DV), lambda b, h: (h, 0, 0))],
        out_specs=[vec(DV), mat],
        out_shape=[jax.ShapeDtypeStruct((B, H, 1, DV), BF16),
                   jax.ShapeDtypeStruct((B, H, DK, DV), F32)],
        compiler_params=_params("arbitrary", "arbitrary"),
    )(q, k, v, r, la, s0, gn)


def _outproj_kernel(ya_ref, yb_ref, wa_ref, wb_ref, x_ref, gt_ref, o_ref):
    mix = (jnp.dot(ya_ref[0], wa_ref[...], preferred_element_type=F32)
           + jnp.dot(yb_ref[0], wb_ref[...], preferred_element_type=F32))
    o_ref[0] = x_ref[0] + gt_ref[0] * mix


def _outproj(ya, yb, w, x, gt):
    B, L, D = x.shape
    Ka, Kb = ya.shape[2], yb.shape[2]
    assert Ka == Kb
    tm = _pick(L, 512)
    tn = _pick(D, 1024, LANES)
    gspec = (pl.BlockSpec((1, 1, tn), lambda j, b, l: (b, 0, j)) if gt.shape[1] == 1
             else pl.BlockSpec((1, tm, tn), lambda j, b, l: (b, l, j)))
    return pl.pallas_call(
        _outproj_kernel,
        grid=(D // tn, B, L // tm),
        in_specs=[pl.BlockSpec((1, tm, Ka), lambda j, b, l: (b, l, 0)),
                  pl.BlockSpec((1, tm, Kb), lambda j, b, l: (b, l, 0)),
                  pl.BlockSpec((Ka, tn), lambda j, b, l: (0, j)),
                  pl.BlockSpec((Kb, tn), lambda j, b, l: (1, j)),
                  pl.BlockSpec((1, tm, tn), lambda j, b, l: (b, l, j)),
                  gspec],
        out_specs=pl.BlockSpec((1, tm, tn), lambda j, b, l: (b, l, j)),
        out_shape=jax.ShapeDtypeStruct((B, L, D), F32),
        compiler_params=_params("arbitrary", "arbitrary", "arbitrary"),
    )(ya, yb, w, w, x, gt)


def _topk_rows(s, ids, kk, fill, val_ref, idx_ref):
    for j in range(kk):
        m = jnp.max(s, axis=0, keepdims=True)
        idx = jnp.min(jnp.where(s == m, ids, fill), axis=0, keepdims=True)
        val_ref[j:j + 1, :] = m
        idx_ref[j:j + 1, :] = idx
        s = jnp.where(ids == idx, -jnp.inf, s)


def _peer_topk_kernel(q_ref, keys_ref, e1_ref, e2_ref, g_ref, sv0, si0, sv1, si1, cand, sc, pos):
    tT = q_ref.shape[0]
    NK, dq = keys_ref.shape[2], keys_ref.shape[3]
    K = PEER_TOPK
    ids = lax.broadcasted_iota(jnp.int32, (NK, tT), 0).astype(F32)
    for p, (sv, si) in enumerate(((sv0, si0), (sv1, si1))):
        qp = q_ref[:, p * dq:(p + 1) * dq].astype(BF16)
        s = lax.dot_general(keys_ref[0, p].astype(BF16), qp, (((1,), (1,)), ((), ())),
                            preferred_element_type=F32)
        _topk_rows(s, ids, K, float(NK), sv, si)
    v0, v1 = sv0[...], sv1[...]
    for a in range(K):
        cand[a * K:(a + 1) * K, :] = v0[a:a + 1] + v1
    pids = lax.broadcasted_iota(jnp.int32, (K * K, tT), 0).astype(F32)
    _topk_rows(cand[...], pids, K, float(K * K), sc, pos)
    pa = jnp.floor(pos[...] * (1.0 / K))
    pb = pos[...] - pa * K
    i0, i1 = si0[...], si1[...]
    e1 = jnp.zeros((K, tT), F32)
    e2 = jnp.zeros((K, tT), F32)
    for a in range(K):
        e1 = jnp.where(pa == a, i0[a:a + 1], e1)
        e2 = jnp.where(pb == a, i1[a:a + 1], e2)
    scv = sc[...]
    ex = jnp.exp(scv - jnp.max(scv, axis=0, keepdims=True))
    e1_ref[0] = e1
    e2_ref[0] = e2
    g_ref[0] = ex / jnp.sum(ex, axis=0, keepdims=True)


def _peer_topk(q, keys):
    T = q.shape[0]
    H, _, NK, dq = keys.shape
    K = PEER_TOPK
    tT = _pick(T, 512, LANES)
    out = jax.ShapeDtypeStruct((H, K, T), F32)
    ospec = pl.BlockSpec((1, K, tT), lambda i, h: (h, 0, i))
    return pl.pallas_call(
        _peer_topk_kernel,
        grid=(T // tT, H),
        in_specs=[pl.BlockSpec((tT, 2 * dq), lambda i, h: (i, h)),
                  pl.BlockSpec((1, 2, NK, dq), lambda i, h: (h, 0, 0, 0))],
        out_specs=[ospec, ospec, ospec],
        out_shape=[out, out, out],
        scratch_shapes=[pltpu.VMEM((K, tT), F32)] * 4
        + [pltpu.VMEM((K * K, tT), F32), pltpu.VMEM((K, tT), F32), pltpu.VMEM((K, tT), F32)],
        compiler_params=_params("arbitrary", "arbitrary"),
    )(q, keys)


def _peer_gate_kernel(e1_ref, e2_ref, g_ref, o_ref, e1t, e2t, gt):
    tT = e1_ref.shape[1]
    NS = e1_ref.shape[0]
    NK = o_ref.shape[1]
    e1t[...] = e1_ref[...].T
    e2t[...] = e2_ref[...].T
    gt[...] = g_ref[...].T
    kid = lax.broadcasted_iota(jnp.int32, (NK, NS), 0).astype(F32)

    def body(t, carry):
        a = jnp.where(kid == e1t[pl.ds(t, 1), :], gt[pl.ds(t, 1), :], 0.0).astype(BF16)
        b = jnp.where(kid == e2t[pl.ds(t, 1), :], 1.0, 0.0).astype(BF16)
        o_ref[t] = lax.dot_general(a, b, (((1,), (1,)), ((), ())),
                                   preferred_element_type=F32).astype(o_ref.dtype)
        return carry

    lax.fori_loop(0, tT, body, 0)


def _peer_gates(e1, e2, g, NK):
    NS, T = e1.shape
    tT = _pick(T, 128, LANES)
    ispec = pl.BlockSpec((NS, tT), lambda i: (0, i))
    return pl.pallas_call(
        _peer_gate_kernel,
        grid=(T // tT,),
        in_specs=[ispec, ispec, ispec],
        out_specs=pl.BlockSpec((tT, NK, NK), lambda i: (i, 0, 0)),
        out_shape=jax.ShapeDtypeStruct((T, NK, NK), BF16),
        scratch_shapes=[pltpu.VMEM((tT, NS), F32)] * 3,
        compiler_params=_params("arbitrary"),
    )(e1, e2, g)


def _peer_dense_kernel(x_ref, u_ref, v_ref, g_ref, o_ref):
    e = pl.program_id(1)

    @pl.when(e == 0)
    def _():
        o_ref[...] = jnp.zeros(o_ref.shape, o_ref.dtype)

    act = lax.dot_general(x_ref[...], u_ref[...], (((1,), (1,)), ((), ())),
                          preferred_element_type=F32)
    p = (g_ref[...].astype(F32) * jax.nn.gelu(act)).astype(BF16)
    o_ref[...] += jnp.dot(p, v_ref[...], preferred_element_type=F32)


def _peer_dense(x, u, v, G):
    T, D = x.shape
    E = u.shape[0]
    tT = _pick(T, 640, 16)
    tE = _pick(E, 256, LANES)
    return pl.pallas_call(
        _peer_dense_kernel,
        grid=(T // tT, E // tE),
        in_specs=[pl.BlockSpec((tT, D), lambda i, e: (i, 0)),
                  pl.BlockSpec((tE, D), lambda i, e: (e, 0)),
                  pl.BlockSpec((tE, D), lambda i, e: (e, 0)),
                  pl.BlockSpec((tT, tE), lambda i, e: (i, e))],
        out_specs=pl.BlockSpec((tT, D), lambda i, e: (i, 0)),
        out_shape=jax.ShapeDtypeStruct((T, D), F32),
        compiler_params=_params("arbitrary", "arbitrary"),
    )(x, u, v, G)


def _resid_kernel(x_ref, y_ref, gt_ref, o_ref):
    o_ref[0] = x_ref[0] + gt_ref[0] * y_ref[0]


def _resid_norm_kernel(x_ref, y_ref, gt_ref, g_ref, o_ref):
    x = x_ref[0] + gt_ref[0] * y_ref[0]
    o_ref[0] = x * lax.rsqrt(jnp.mean(x * x, axis=-1, keepdims=True) + RMS_EPS) * g_ref[...]


def _resid(x, y, gt, g=None):
    B, L, D = x.shape
    tl = _pick(L, 256)
    blk = pl.BlockSpec((1, tl, D), lambda b, l: (b, l, 0))
    in_specs = [blk, blk, _mod_spec(gt, tl, D)]
    args = [x, y, gt]
    if g is not None:
        in_specs.append(pl.BlockSpec((1, D), lambda b, l: (0, 0)))
        args.append(g)
    return pl.pallas_call(
        _resid_kernel if g is None else _resid_norm_kernel,
        grid=(B, L // tl),
        in_specs=in_specs,
        out_specs=blk,
        out_shape=jax.ShapeDtypeStruct((B, L, D), F32),
        compiler_params=_params("arbitrary", "arbitrary"),
    )(*args)


def kernel(x_prompt, x_sample, state_conv, state_lru_h, state_gla, c_prompt, c_sample, w_ada, b_ada, g_mix, w_in, conv_w, conv_b, lru_w_a, lru_b_a, lru_w_x, lru_b_x, lru_lambda, gla_w_alpha, gla_b_alpha, gla_g_norm, w_out, g_ffn, peer_w_q, peer_sub_keys, peer_u, peer_v, g_final):
    depth = w_ada.shape[0]
    Bp, L, D = x_prompt.shape
    Bs = x_sample.shape[0]
    assert x_sample.shape[1] == 1
    W = conv_w.shape[2]
    nt = conv_w.shape[1] - 1
    _, _, H, DK, DV = state_gla.shape
    rank = gla_w_alpha.shape[1]
    PH, _, NK, _ = peer_sub_keys.shape[1:]
    n_main = 2 * W + 2 * H * DK + 2 * H * DV
    assert w_in.shape[2] == n_main + rank and rank <= LANES
    offs = (2 * W, 2 * W + H * DK, 2 * W + 2 * H * DK, 2 * W + 2 * H * DK + H * DV)
    Tp = Bp * L

    R = Bp + Bs
    Rp = -(-R // SUBLANES) * SUBLANES
    c_all = jnp.pad(jnp.concatenate([c_prompt, c_sample], axis=0), ((0, Rp - R), (0, 0)))

    x_p = x_prompt
    x_s = x_sample.reshape(1, Bs, D)
    outs = [[] for _ in range(6)]
    for l in range(depth):
        ada = _ada(c_all, w_ada[l], b_ada[l][None])
        mods_p = [ada[:Bp, i * D:(i + 1) * D][:, None, :] for i in range(6)]
        mods_s = [ada[Bp:R, i * D:(i + 1) * D][None] for i in range(6)]

        w_in_main = w_in[l][:, :n_main].astype(BF16)
        w_in_lr = jnp.pad(w_in[l][:, n_main:], ((0, 0), (0, LANES - rank))).astype(BF16)
        w_al = jnp.pad(gla_w_alpha[l], ((0, LANES - rank), (0, 0))).astype(BF16)
        b_al = gla_b_alpha[l][None]
        w_out_b = w_out[l].astype(BF16)
        wa = lru_w_a[l].astype(BF16)
        wx = lru_w_x[l].astype(BF16)
        lru_vecs = (conv_w[l], conv_b[l][None], wa, lru_b_a[l][None], wx, lru_b_x[l][None],
                    lru_lambda[l][None])
        gn = gla_g_norm[l]

        sh1, sc1, gt1, sh2, sc2, gt2 = mods_p
        hn = _norm_mod(x_p, g_mix[l][None], sc1, sh1, BF16).reshape(Tp, D)
        proj = _matmul(hn, w_in_main).reshape(Bp, L, n_main)
        la = _gla_gate(hn, w_in_lr, w_al, b_al).reshape(Bp, L, H * DK)
        y_lru, h_p, conv_p = _lru_prompt(
            proj, jnp.zeros((Bp, nt, W), F32), jnp.zeros((Bp, 1, W), F32), *lru_vecs, W)
        y_gla, s_p = _gla_prompt(proj, la, jnp.zeros((Bp, H, DK, DV), F32), gn[None], offs)
        x1_p = _outproj(y_lru, y_gla, w_out_b, x_p, gt1)
        hn2_p = _norm_mod(x1_p, g_ffn[l][None], sc2, sh2, BF16).reshape(Tp, D)
        gt2_p = gt2

        sh1, sc1, gt1, sh2, sc2, gt2 = mods_s
        hn = _norm_mod(x_s, g_mix[l][None], sc1, sh1, BF16).reshape(Bs, D)
        proj = _matmul(hn, w_in_main)
        la = _gla_gate(hn, w_in_lr, w_al, b_al)
        tail = jnp.transpose(state_conv[l], (1, 0, 2))
        y_lru, h_s = _lru_decode(proj, tail, state_lru_h[l], *lru_vecs, W)
        conv_s = jnp.concatenate([state_conv[l][:, 1:], proj[:, None, :W]], axis=1)
        qo, ko, vo, ro = offs
        heads = lambda a, d: a.reshape(Bs, H, 1, d)
        y_gla, s_s = _gla_decode(
            heads(proj[:, qo:ko], DK), heads(proj[:, ko:vo], DK), heads(proj[:, vo:ro], DV),
            heads(proj[:, ro:n_main], DV), heads(la, DK), state_gla[l], gn.reshape(H, 1, DV))
        x1_s = _outproj(y_lru[None], y_gla.reshape(1, Bs, H * DV), w_out_b, x_s, gt1)
        hn2_s = _norm_mod(x1_s, g_ffn[l][None], sc2, sh2, BF16).reshape(Bs, D)
        gt2_s = gt2

        hn2 = jnp.concatenate([hn2_p, hn2_s], axis=0)
        T = Tp + Bs
        q = _matmul(hn2, peer_w_q[l].astype(BF16), tm_pref=640)
        e1, e2, gate = _peer_topk(q, peer_sub_keys[l])
        G = _peer_gates(e1.reshape(PH * PEER_TOPK, T), e2.reshape(PH * PEER_TOPK, T),
                        gate.reshape(PH * PEER_TOPK, T), NK).reshape(T, NK * NK)
        y = _peer_dense(hn2, peer_u[l].astype(BF16), peer_v[l].astype(BF16), G)
        y_p = y[:Tp].reshape(Bp, L, D)
        y_s = y[Tp:].reshape(1, Bs, D)
        last = l == depth - 1
        x_p = _resid(x1_p, y_p, gt2_p, g_final[None] if last else None)
        x_s = _resid(x1_s, y_s, gt2_s, g_final[None] if last else None)

        for lst, val in zip(outs, (conv_p, h_p.reshape(Bp, W), s_p, conv_s, h_s, s_s)):
            lst.append(val)

    return (x_p, x_s.reshape(Bs, 1, D)) + tuple(jnp.stack(o) for o in outs)
```

```python
import functools

import jax
import jax.numpy as jnp
from jax import lax
from jax.experimental import pallas as pl
from jax.experimental.pallas import tpu as pltpu

F32 = jnp.float32
BF16 = jnp.bfloat16

RMS_EPS = 1e-6
RG_C = 8.0
GLA_TAU = 16.0
PEER_TOPK = 16
GLA_CHUNK = 128
GLA_SUB = 16
LANES = 128
SUBLANES = 8
VMEM_LIMIT = 56 * 1024 * 1024


def _params(*sem):
    return pltpu.CompilerParams(dimension_semantics=sem, vmem_limit_bytes=VMEM_LIMIT)


def _pick(n, pref, mult=SUBLANES):
    best = None
    for d in range(mult, min(n, pref) + 1, mult):
        if n % d == 0:
            best = d
    return n if best is None else best


def _softplus(y):
    return jnp.maximum(y, 0.0) + jnp.log1p(jnp.exp(-jnp.abs(y)))


def _log_sigmoid(z):
    return jnp.minimum(z, 0.0) - jnp.log1p(jnp.exp(-jnp.abs(z)))


def _row_to_col(row, n):
    r = lax.broadcasted_iota(jnp.int32, (n, n), 0)
    c = lax.broadcasted_iota(jnp.int32, (n, n), 1)
    return jnp.sum(jnp.where(r == c, jnp.broadcast_to(row, (n, n)), 0.0), axis=1, keepdims=True)


def _ada_kernel(c_ref, w_ref, b_ref, o_ref):
    a = jax.nn.silu(c_ref[...]).astype(BF16)
    o_ref[...] = jnp.dot(a, w_ref[...].astype(BF16), preferred_element_type=F32) + b_ref[...]


def _ada(c, w, b):
    R, D = c.shape
    N = w.shape[1]
    tn = _pick(N, 512, LANES)
    return pl.pallas_call(
        _ada_kernel,
        name="ada",
        grid=(N // tn,),
        in_specs=[pl.BlockSpec((R, D), lambda j: (0, 0)),
                  pl.BlockSpec((D, tn), lambda j: (0, j)),
                  pl.BlockSpec((1, tn), lambda j: (0, j))],
        out_specs=pl.BlockSpec((R, tn), lambda j: (0, j)),
        out_shape=jax.ShapeDtypeStruct((R, N), F32),
        compiler_params=_params("arbitrary"),
    )(c, w, b)


def _norm_mod_kernel(x_ref, g_ref, sc_ref, sh_ref, o_ref):
    x = x_ref[0]
    y = x * lax.rsqrt(jnp.mean(x * x, axis=-1, keepdims=True) + RMS_EPS) * g_ref[...]
    o_ref[0] = (y * (1.0 + sc_ref[0]) + sh_ref[0]).astype(o_ref.dtype)


def _mod_spec(m, tl, D):
    if m.shape[1] == 1:
        return pl.BlockSpec((1, 1, D), lambda b, l: (b, 0, 0))
    return pl.BlockSpec((1, tl, D), lambda b, l: (b, l, 0))


def _norm_mod(x, g, sc, sh, out_dtype):
    B, L, D = x.shape
    tl = _pick(L, 256)
    return pl.pallas_call(
        _norm_mod_kernel,
        name="norm_mod",
        grid=(B, L // tl),
        in_specs=[pl.BlockSpec((1, tl, D), lambda b, l: (b, l, 0)),
                  pl.BlockSpec((1, D), lambda b, l: (0, 0)),
                  _mod_spec(sc, tl, D), _mod_spec(sh, tl, D)],
        out_specs=pl.BlockSpec((1, tl, D), lambda b, l: (b, l, 0)),
        out_shape=jax.ShapeDtypeStruct((B, L, D), out_dtype),
        compiler_params=_params("arbitrary", "arbitrary"),
    )(x, g, sc, sh)


def _norm_mod_cat_kernel(xp_ref, xs_ref, g_ref, scp_ref, shp_ref, scs_ref, shs_ref, o_ref, *, n_p):
    i = pl.program_id(0)

    def f(x, sc, sh):
        y = x * lax.rsqrt(jnp.mean(x * x, axis=-1, keepdims=True) + RMS_EPS) * g_ref[...]
        return (y * (1.0 + sc) + sh).astype(o_ref.dtype)

    @pl.when(i < n_p)
    def _():
        o_ref[...] = f(xp_ref[0], scp_ref[0], shp_ref[0])

    @pl.when(i >= n_p)
    def _():
        o_ref[...] = f(xs_ref[0], scs_ref[0], shs_ref[0])


def _norm_mod_cat(xp, xs, g, scp, shp, scs, shs, out_dtype):
    Bp, L, D = xp.shape
    Bs = xs.shape[1]
    tl = _pick(Bs, 256)
    assert L % tl == 0
    lb = L // tl
    n_p = Bp * lb
    n_s = Bs // tl
    pb = lambda i: jnp.minimum(i, n_p - 1)
    sb = lambda i: jnp.maximum(i - n_p, 0)
    pmod = pl.BlockSpec((1, 1, D), lambda i: (pb(i) // lb, 0, 0))
    smod = pl.BlockSpec((1, tl, D), lambda i: (0, sb(i), 0))
    return pl.pallas_call(
        functools.partial(_norm_mod_cat_kernel, n_p=n_p),
        name="norm_mod_cat",
        grid=(n_p + n_s,),
        in_specs=[pl.BlockSpec((1, tl, D), lambda i: (pb(i) // lb, pb(i) % lb, 0)),
                  pl.BlockSpec((1, tl, D), lambda i: (0, sb(i), 0)),
                  pl.BlockSpec((1, D), lambda i: (0, 0)),
                  pmod, pmod, smod, smod],
        out_specs=pl.BlockSpec((tl, D), lambda i: (i, 0)),
        out_shape=jax.ShapeDtypeStruct((Bp * L + Bs, D), out_dtype),
        compiler_params=_params("arbitrary"),
    )(xp, xs, g, scp, shp, scs, shs)


def _mm_kernel(x_ref, w_ref, o_ref):
    o_ref[...] = jnp.dot(x_ref[...], w_ref[...], preferred_element_type=F32)


def _matmul(x, w, tm_pref=512, tn_pref=1024):
    M, K = x.shape
    N = w.shape[1]
    tm = _pick(M, tm_pref)
    tn = _pick(N, tn_pref, LANES)
    return pl.pallas_call(
        _mm_kernel,
        name="matmul",
        grid=(N // tn, M // tm),
        in_specs=[pl.BlockSpec((tm, K), lambda j, i: (i, 0)),
                  pl.BlockSpec((K, tn), lambda j, i: (0, j))],
        out_specs=pl.BlockSpec((tm, tn), lambda j, i: (i, j)),
        out_shape=jax.ShapeDtypeStruct((M, N), F32),
        compiler_params=_params("arbitrary", "arbitrary"),
    )(x, w)


def _gate_kernel(x_ref, w1_ref, w2_ref, b2_ref, o_ref):
    a = jnp.dot(x_ref[...], w1_ref[...], preferred_element_type=F32)
    z = jnp.dot(a.astype(BF16), w2_ref[...], preferred_element_type=F32) + b2_ref[...]
    o_ref[...] = _log_sigmoid(z) / GLA_TAU


def _gla_gate(hn, w1, w2, b2):
    M, K = hn.shape
    R = w1.shape[1]
    N = w2.shape[1]
    tm = _pick(M, 512)
    return pl.pallas_call(
        _gate_kernel,
        name="gla_gate",
        grid=(M // tm,),
        in_specs=[pl.BlockSpec((tm, K), lambda i: (i, 0)),
                  pl.BlockSpec((K, R), lambda i: (0, 0)),
                  pl.BlockSpec((R, N), lambda i: (0, 0)),
                  pl.BlockSpec((1, N), lambda i: (0, 0))],
        out_specs=pl.BlockSpec((tm, N), lambda i: (i, 0)),
        out_shape=jax.ShapeDtypeStruct((M, N), F32),
        compiler_params=_params("arbitrary"),
    )(hn, w1, w2, b2)


def _lru_gates(xc, wa_ref, ba, wx_ref, bx, lam):
    nb, bd = wa_ref.shape[0], wa_ref.shape[1]
    xcb = xc.astype(BF16)
    ra, ri = [], []
    for n in range(nb):
        xs = xcb[:, n * bd:(n + 1) * bd]
        ra.append(jnp.dot(xs, wa_ref[n], preferred_element_type=F32))
        ri.append(jnp.dot(xs, wx_ref[n], preferred_element_type=F32))
    r = jax.nn.sigmoid(jnp.concatenate(ra, axis=1) + ba)
    i = jax.nn.sigmoid(jnp.concatenate(ri, axis=1) + bx)
    log_a = -RG_C * r * _softplus(-lam)
    a = jnp.exp(log_a)
    u = jnp.sqrt(-jnp.tanh(log_a) * (a * a + 1.0)) * i * xc
    return a, u


def _lru_prompt_kernel(xb_ref, gb_ref, tail_ref, h0_ref, cw_ref, cb_ref, wa_ref, ba_ref, wx_ref,
                       bx_ref, lam_ref, y_ref, hl_ref, conv_ref, buf, hcar):
    l = pl.program_id(2)
    nl = pl.num_programs(2)
    tl, tw = xb_ref.shape[1], xb_ref.shape[2]
    nt = tail_ref.shape[1]
    base = SUBLANES

    @pl.when(l == 0)
    def _():
        buf[base - nt:base, :] = tail_ref[0]
        hcar[...] = h0_ref[0]

    xb = xb_ref[0]
    buf[base:base + tl, :] = xb
    cw = cw_ref[...]
    xc = cb_ref[...] + xb * cw[nt:nt + 1]
    for j in range(nt):
        xc = xc + buf[base - nt + j:base - nt + j + tl, :] * cw[j:j + 1]
    new_tail = buf[base + tl - nt:base + tl, :]

    a, u = _lru_gates(xc, wa_ref, ba_ref[...], wx_ref, bx_ref[...], lam_ref[...])

    row = lax.broadcasted_iota(jnp.int32, (tl, tw), 0) & (SUBLANES - 1)
    A, Bv = a, u
    s = 1
    while s < SUBLANES:
        As = pltpu.roll(A, s, axis=0)
        Bs = pltpu.roll(Bv, s, axis=0)
        m = row >= s
        Bv = jnp.where(m, A * Bs + Bv, Bv)
        A = jnp.where(m, A * As, A)
        s *= 2
    h = hcar[...]
    outs = []
    for g in range(tl // SUBLANES):
        hg = A[g * SUBLANES:(g + 1) * SUBLANES] * h + Bv[g * SUBLANES:(g + 1) * SUBLANES]
        outs.append(hg)
        h = hg[SUBLANES - 1:SUBLANES]
    hseq = jnp.concatenate(outs, axis=0)
    hcar[...] = h
    buf[base - nt:base, :] = new_tail
    y_ref[0] = (hseq * jax.nn.gelu(gb_ref[0])).astype(y_ref.dtype)

    @pl.when(l == nl - 1)
    def _():
        hl_ref[0] = h
        conv_ref[0] = new_tail


def _lru_prompt(proj, tail, h0, cw, cb, wa, ba, wx, bx, lam, W):
    B, L, _ = proj.shape
    nt = tail.shape[1]
    bd = wa.shape[1]
    tw = _pick(W, 512, bd)
    tl = _pick(L, 256)
    nwb = W // tw
    col = lambda b, j, l: (0, j)
    return pl.pallas_call(
        _lru_prompt_kernel,
        name="lru_prompt",
        grid=(B, nwb, L // tl),
        in_specs=[pl.BlockSpec((1, tl, tw), lambda b, j, l: (b, l, j)),
                  pl.BlockSpec((1, tl, tw), lambda b, j, l: (b, l, j + nwb)),
                  pl.BlockSpec((1, nt, tw), lambda b, j, l: (b, 0, j)),
                  pl.BlockSpec((1, 1, tw), lambda b, j, l: (b, 0, j)),
                  pl.BlockSpec((nt + 1, tw), col),
                  pl.BlockSpec((1, tw), col),
                  pl.BlockSpec((tw // bd, bd, bd), lambda b, j, l: (j, 0, 0)),
                  pl.BlockSpec((1, tw), col),
                  pl.BlockSpec((tw // bd, bd, bd), lambda b, j, l: (j, 0, 0)),
                  pl.BlockSpec((1, tw), col),
                  pl.BlockSpec((1, tw), col)],
        out_specs=[pl.BlockSpec((1, tl, tw), lambda b, j, l: (b, l, j)),
                   pl.BlockSpec((1, 1, tw), lambda b, j, l: (b, 0, j)),
                   pl.BlockSpec((1, nt, tw), lambda b, j, l: (b, 0, j))],
        out_shape=[jax.ShapeDtypeStruct((B, L, W), BF16),
                   jax.ShapeDtypeStruct((B, 1, W), F32),
                   jax.ShapeDtypeStruct((B, nt, W), F32)],
        scratch_shapes=[pltpu.VMEM((tl + SUBLANES, tw), F32), pltpu.VMEM((1, tw), F32)],
        compiler_params=_params("arbitrary", "arbitrary", "arbitrary"),
    )(proj, proj, tail, h0, cw, cb, wa, ba, wx, bx, lam)


def _lru_decode_kernel(xb_ref, gb_ref, tail_ref, h0_ref, cw_ref, cb_ref, wa_ref, ba_ref, wx_ref,
                       bx_ref, lam_ref, y_ref, h_ref):
    nt = tail_ref.shape[0]
    xb = xb_ref[...]
    cw = cw_ref[...]
    xc = cb_ref[...] + xb * cw[nt:nt + 1]
    for j in range(nt):
        xc = xc + tail_ref[j] * cw[j:j + 1]
    a, u = _lru_gates(xc, wa_ref, ba_ref[...], wx_ref, bx_ref[...], lam_ref[...])
    h = a * h0_ref[...] + u
    h_ref[...] = h
    y_ref[...] = (h * jax.nn.gelu(gb_ref[...])).astype(y_ref.dtype)


def _lru_decode(proj, tail, h0, cw, cb, wa, ba, wx, bx, lam, W):
    B = proj.shape[0]
    nt = tail.shape[0]
    bd = wa.shape[1]
    tw = _pick(W, 512, bd)
    nwb = W // tw
    col = lambda j: (0, j)
    return pl.pallas_call(
        _lru_decode_kernel,
        name="lru_decode",
        grid=(nwb,),
        in_specs=[pl.BlockSpec((B, tw), col),
                  pl.BlockSpec((B, tw), lambda j: (0, j + nwb)),
                  pl.BlockSpec((nt, B, tw), lambda j: (0, 0, j)),
                  pl.BlockSpec((B, tw), col),
                  pl.BlockSpec((nt + 1, tw), col),
                  pl.BlockSpec((1, tw), col),
                  pl.BlockSpec((tw // bd, bd, bd), lambda j: (j, 0, 0)),
                  pl.BlockSpec((1, tw), col),
                  pl.BlockSpec((tw // bd, bd, bd), lambda j: (j, 0, 0)),
                  pl.BlockSpec((1, tw), col),
                  pl.BlockSpec((1, tw), col)],
        out_specs=[pl.BlockSpec((B, tw), col), pl.BlockSpec((B, tw), col)],
        out_shape=[jax.ShapeDtypeStruct((B, W), BF16), jax.ShapeDtypeStruct((B, W), F32)],
        compiler_params=_params("arbitrary"),
    )(proj, proj, tail, h0, cw, cb, wa, ba, wx, bx, lam)


def _head_norm_gate(o, gn, r):
    on = o * lax.rsqrt(jnp.mean(o * o, axis=-1, keepdims=True) + RMS_EPS) * gn
    return on * jax.nn.silu(r)


def _gla_prompt_kernel(q_ref, k_ref, v_ref, r_ref, la_ref, s0_ref, gn_ref, y_ref, so_ref, S, att):
    c = pl.program_id(2)
    nc = pl.num_programs(2)
    C, DK = q_ref.shape[1], q_ref.shape[2]
    sub = min(GLA_SUB, C)

    @pl.when(c == 0)
    def _():
        S[...] = s0_ref[0, 0]

    q = q_ref[0] * (DK ** -0.5)
    k = k_ref[0]
    vb = v_ref[0].astype(BF16)
    g = la_ref[0]
    rr = lax.broadcasted_iota(jnp.int32, (C, C), 0)
    cc = lax.broadcasted_iota(jnp.int32, (C, C), 1)
    b = jnp.dot((rr >= cc).astype(F32), g, precision=lax.Precision.HIGHEST,
                preferred_element_type=F32)
    b_last = b[C - 1:C]

    inter = jnp.dot((q * jnp.exp(b)).astype(BF16), S[...].astype(BF16), preferred_element_type=F32)

    lane = lax.broadcasted_iota(jnp.int32, (sub, sub), 1)
    srow = lax.broadcasted_iota(jnp.int32, (sub, sub), 0)
    for I in range(C // sub):
        lo = I * sub
        qI, kI, bI = q[lo:lo + sub], k[lo:lo + sub], b[lo:lo + sub]
        acc = jnp.zeros((sub, sub), F32)
        for s in range(sub):
            w = qI * kI[s:s + 1] * jnp.exp(jnp.minimum(bI - bI[s:s + 1], 0.0))
            acc = jnp.where(lane == s, jnp.sum(w, axis=1, keepdims=True), acc)
        att[lo:lo + sub, lo:lo + sub] = jnp.where(srow >= lane, acc, 0.0)
        if lo > 0:
            ref = b[lo - 1:lo]
            qt = (qI * jnp.exp(bI - ref)).astype(BF16)
            kt = (k[0:lo] * jnp.exp(ref - b[0:lo])).astype(BF16)
            att[lo:lo + sub, 0:lo] = lax.dot_general(qt, kt, (((1,), (1,)), ((), ())),
                                                     preferred_element_type=F32)
        if lo + sub < C:
            att[lo:lo + sub, lo + sub:C] = jnp.zeros((sub, C - lo - sub), F32)
    o = inter + jnp.dot(att[...].astype(BF16), vb, preferred_element_type=F32)

    kh = (k * jnp.exp(b_last - b)).astype(BF16)
    upd = lax.dot_general(kh, vb, (((0,), (0,)), ((), ())), preferred_element_type=F32)
    S[...] = _row_to_col(jnp.exp(b_last), DK) * S[...] + upd

    y_ref[0] = _head_norm_gate(o, gn_ref[...], r_ref[0]).astype(y_ref.dtype)

    @pl.when(c == nc - 1)
    def _():
        so_ref[0, 0] = S[...]


def _gla_prompt(proj, la, s0, gn, offs):
    B, L, _ = proj.shape
    _, H, DK, DV = s0.shape
    C = _pick(L, GLA_CHUNK)
    qo, ko, vo, ro = offs
    return pl.pallas_call(
        _gla_prompt_kernel,
        name="gla_prompt",
        grid=(B, H, L // C),
        in_specs=[pl.BlockSpec((1, C, DK), lambda b, h, c: (b, c, qo // DK + h)),
                  pl.BlockSpec((1, C, DK), lambda b, h, c: (b, c, ko // DK + h)),
                  pl.BlockSpec((1, C, DV), lambda b, h, c: (b, c, vo // DV + h)),
                  pl.BlockSpec((1, C, DV), lambda b, h, c: (b, c, ro // DV + h)),
                  pl.BlockSpec((1, C, DK), lambda b, h, c: (b, c, h)),
                  pl.BlockSpec((1, 1, DK, DV), lambda b, h, c: (b, h, 0, 0)),
                  pl.BlockSpec((1, DV), lambda b, h, c: (0, h))],
        out_specs=[pl.BlockSpec((1, C, DV), lambda b, h, c: (b, c, h)),
                   pl.BlockSpec((1, 1, DK, DV), lambda b, h, c: (b, h, 0, 0))],
        out_shape=[jax.ShapeDtypeStruct((B, L, H * DV), BF16),
                   jax.ShapeDtypeStruct((B, H, DK, DV), F32)],
        scratch_shapes=[pltpu.VMEM((DK, DV), F32), pltpu.VMEM((C, C), F32)],
        compiler_params=_params("arbitrary", "arbitrary", "arbitrary"),
    )(proj, proj, proj, proj, la, s0, gn)


def _gla_decode_kernel(q_ref, k_ref, v_ref, r_ref, la_ref, s0_ref, gn_ref, y_ref, so_ref):
    bb, H, _, DK = q_ref.shape
    for b in range(bb):
        for h in range(H):
            q = q_ref[b, h] * (DK ** -0.5)
            S = (_row_to_col(jnp.exp(la_ref[b, h]), DK) * s0_ref[b, h]
                 + _row_to_col(k_ref[b, h], DK) * v_ref[b, h])
            so_ref[b, h] = S
            o = jnp.sum(_row_to_col(q, DK) * S, axis=0, keepdims=True)
            y_ref[b, h] = _head_norm_gate(o, gn_ref[h], r_ref[b, h]).astype(y_ref.dtype)


def _gla_decode(q, k, v, r, la, s0, gn):
    B, H, DK, DV = s0.shape
    bb = 2 if B % 2 == 0 else 1
    vec = lambda d: pl.BlockSpec((bb, H, 1, d), lambda b: (b, 0, 0, 0))
    mat = pl.BlockSpec((bb, H, DK, DV), lambda b: (b, 0, 0, 0))
    return pl.pallas_call(
        _gla_decode_kernel,
        name="gla_decode",
        grid=(B // bb,),
        in_specs=[vec(DK), vec(DK), vec(DV), vec(DV), vec(DK), mat,
                  pl.BlockSpec((H, 1, DV), lambda b: (0, 0, 0))],
        out_specs=[vec(DV), mat],
        out_shape=[jax.ShapeDtypeStruct((B, H, 1, DV), BF16),
                   jax.ShapeDtypeStruct((B, H, DK, DV), F32)],
        compiler_params=_params("arbitrary"),
    )(q, k, v, r, la, s0, gn)


def _outproj_kernel(ya_ref, yb_ref, wa_ref, wb_ref, x_ref, gt_ref, o_ref):
    mix = (jnp.dot(ya_ref[0], wa_ref[...], preferred_element_type=F32)
           + jnp.dot(yb_ref[0], wb_ref[...], preferred_element_type=F32))
    o_ref[0] = x_ref[0] + gt_ref[0] * mix


def _outproj(ya, yb, w, x, gt):
    B, L, D = x.shape
    Ka, Kb = ya.shape[2], yb.shape[2]
    assert Ka == Kb
    tm = _pick(L, 512)
    tn = _pick(D, 1024, LANES)
    gspec = (pl.BlockSpec((1, 1, tn), lambda j, b, l: (b, 0, j)) if gt.shape[1] == 1
             else pl.BlockSpec((1, tm, tn), lambda j, b, l: (b, l, j)))
    return pl.pallas_call(
        _outproj_kernel,
        name="outproj",
        grid=(D // tn, B, L // tm),
        in_specs=[pl.BlockSpec((1, tm, Ka), lambda j, b, l: (b, l, 0)),
                  pl.BlockSpec((1, tm, Kb), lambda j, b, l: (b, l, 0)),
                  pl.BlockSpec((Ka, tn), lambda j, b, l: (0, j)),
                  pl.BlockSpec((Kb, tn), lambda j, b, l: (1, j)),
                  pl.BlockSpec((1, tm, tn), lambda j, b, l: (b, l, j)),
                  gspec],
        out_specs=pl.BlockSpec((1, tm, tn), lambda j, b, l: (b, l, j)),
        out_shape=jax.ShapeDtypeStruct((B, L, D), F32),
        compiler_params=_params("arbitrary", "arbitrary", "arbitrary"),
    )(ya, yb, w, w, x, gt)


def _topk_rows(s, ids, kk, fill, val_ref, idx_ref):
    for j in range(kk):
        m = jnp.max(s, axis=0, keepdims=True)
        idx = jnp.min(jnp.where(s == m, ids, fill), axis=0, keepdims=True)
        val_ref[j:j + 1, :] = m
        idx_ref[j:j + 1, :] = idx
        s = jnp.where(ids == idx, -jnp.inf, s)


def _pair_layout(K):
    segs, ids, r0 = [], [], 0
    b = 0
    while b < K and K // (b + 1) > 1:
        na = K // (b + 1)
        segs.append((r0, (0, na), (b, b + 1)))
        rows = -(-na // SUBLANES) * SUBLANES
        ids += [a * K + b for a in range(na)] + [K * K] * (rows - na)
        r0 += rows
        b += 1
    if b < K:
        nb = K - b
        segs.append((r0, (0, 1), (b, K)))
        rows = -(-nb // SUBLANES) * SUBLANES
        ids += list(range(b, K)) + [K * K] * (rows - nb)
        r0 += rows
    return segs, ids, r0


def _peer_topk_kernel(q_ref, keys_ref, pid_ref, e1_ref, e2_ref, g_ref, sv0, si0, sv1, si1, cand,
                      sc, pos, *, segs):
    tT = q_ref.shape[0]
    H, _, NK, dq = keys_ref.shape
    K = PEER_TOPK
    ids = lax.broadcasted_iota(jnp.int32, (NK, tT), 0).astype(F32)
    pids = pid_ref[...]

    def head(h, carry):
        for p, (sv, si) in enumerate(((sv0, si0), (sv1, si1))):
            off = pl.multiple_of((2 * h + p) * dq, dq)
            qp = q_ref[:, pl.ds(off, dq)].astype(BF16)
            s = lax.dot_general(keys_ref[h, p].astype(BF16), qp, (((1,), (1,)), ((), ())),
                                preferred_element_type=F32)
            _topk_rows(s, ids, K, float(NK), sv, si)
        v0, v1 = sv0[...], sv1[...]
        cand[...] = jnp.full(cand.shape, -jnp.inf, F32)
        for r0, (a0, a1), (b0, b1) in segs:
            n = max(a1 - a0, b1 - b0)
            cand[r0:r0 + n, :] = v0[a0:a1] + v1[b0:b1]
        _topk_rows(cand[...], pids, K, float(K * K), sc, pos)
        pa = jnp.floor(pos[...] * (1.0 / K))
        pb = pos[...] - pa * K
        i0, i1 = si0[...], si1[...]
        e1 = jnp.zeros((K, tT), F32)
        e2 = jnp.zeros((K, tT), F32)
        for a in range(K):
            e1 = jnp.where(pa == a, i0[a:a + 1], e1)
            e2 = jnp.where(pb == a, i1[a:a + 1], e2)
        scv = sc[...]
        ex = jnp.exp(scv - jnp.max(scv, axis=0, keepdims=True))
        e1_ref[h] = e1
        e2_ref[h] = e2
        g_ref[h] = ex / jnp.sum(ex, axis=0, keepdims=True)
        return carry

    lax.fori_loop(0, H, head, 0)


def _peer_topk(q, keys):
    T = q.shape[0]
    H, _, NK, dq = keys.shape
    K = PEER_TOPK
    tT = _pick(T, LANES, LANES)
    segs, ids, nr = _pair_layout(K)
    pids = jnp.broadcast_to(jnp.asarray(ids, F32)[:, None], (nr, tT))
    out = jax.ShapeDtypeStruct((H, K, T), F32)
    ospec = pl.BlockSpec((H, K, tT), lambda i: (0, 0, i))
    return pl.pallas_call(
        functools.partial(_peer_topk_kernel, segs=segs),
        name="peer_topk",
        grid=(T // tT,),
        in_specs=[pl.BlockSpec((tT, 2 * H * dq), lambda i: (i, 0)),
                  pl.BlockSpec((H, 2, NK, dq), lambda i: (0, 0, 0, 0)),
                  pl.BlockSpec((nr, tT), lambda i: (0, 0))],
        out_specs=[ospec, ospec, ospec],
        out_shape=[out, out, out],
        scratch_shapes=[pltpu.VMEM((K, tT), F32)] * 4
        + [pltpu.VMEM((nr, tT), F32), pltpu.VMEM((K, tT), F32), pltpu.VMEM((K, tT), F32)],
        compiler_params=_params("arbitrary"),
    )(q, keys, pids)


def _peer_gate_kernel(e1_ref, e2_ref, g_ref, o_ref, e1t, e2t, gt, gbuf, *, unroll):
    NS, tT = e1_ref.shape
    NK = gbuf.shape[1]
    e1t[...] = e1_ref[...].T
    e2t[...] = e2_ref[...].T
    gt[...] = g_ref[...].T
    kid = lax.broadcasted_iota(jnp.int32, (NK, NS), 0).astype(F32)

    def body(i, carry):
        for j in range(unroll):
            t = i * unroll + j
            a = jnp.where(kid == e1t[pl.ds(t, 1), :], gt[pl.ds(t, 1), :], 0.0).astype(BF16)
            b = jnp.where(kid == e2t[pl.ds(t, 1), :], 1.0, 0.0).astype(BF16)
            gbuf[pl.ds(pl.multiple_of(t * NK, NK), NK), :] = lax.dot_general(
                a, b, (((1,), (1,)), ((), ())), preferred_element_type=F32)
        return carry

    lax.fori_loop(0, tT // unroll, body, 0)
    for i1 in range(NK):
        o_ref[:, i1 * NK:(i1 + 1) * NK] = gbuf[pl.ds(i1, tT, stride=NK), :].astype(o_ref.dtype)


def _peer_gates(e1, e2, g, NK):
    NS, T = e1.shape
    tT = _pick(T, 128, LANES)
    unroll = 8 if tT % 8 == 0 else 1
    ispec = pl.BlockSpec((NS, tT), lambda i: (0, i))
    return pl.pallas_call(
        functools.partial(_peer_gate_kernel, unroll=unroll),
        name="peer_gate",
        grid=(T // tT,),
        in_specs=[ispec, ispec, ispec],
        out_specs=pl.BlockSpec((tT, NK * NK), lambda i: (i, 0)),
        out_shape=jax.ShapeDtypeStruct((T, NK * NK), BF16),
        scratch_shapes=[pltpu.VMEM((tT, NS), F32)] * 3 + [pltpu.VMEM((tT * NK, NK), F32)],
        compiler_params=_params("arbitrary"),
    )(e1, e2, g)


def _peer_dense_kernel(x_ref, u_ref, v_ref, g_ref, o_ref, *, nsplit):
    e = pl.program_id(1)

    @pl.when(e == 0)
    def _():
        o_ref[...] = jnp.zeros(o_ref.shape, o_ref.dtype)

    rs = x_ref.shape[0] // nsplit
    for r in range(nsplit):
        rows = slice(r * rs, (r + 1) * rs)
        act = lax.dot_general(x_ref[rows, :], u_ref[...], (((1,), (1,)), ((), ())),
                              preferred_element_type=F32)
        p = (g_ref[rows, :].astype(F32) * jax.nn.gelu(act)).astype(BF16)
        o_ref[rows, :] += jnp.dot(p, v_ref[...], preferred_element_type=F32)


def _peer_dense(x, u, v, G):
    T, D = x.shape
    E = u.shape[0]
    tT = _pick(T, 640, 16)
    tE = _pick(E, 512, LANES)
    nsplit = 2 if tT % 32 == 0 else 1
    return pl.pallas_call(
        functools.partial(_peer_dense_kernel, nsplit=nsplit),
        name="peer_dense",
        grid=(T // tT, E // tE),
        in_specs=[pl.BlockSpec((tT, D), lambda i, e: (i, 0)),
                  pl.BlockSpec((tE, D), lambda i, e: (e, 0)),
                  pl.BlockSpec((tE, D), lambda i, e: (e, 0)),
                  pl.BlockSpec((tT, tE), lambda i, e: (i, e))],
        out_specs=pl.BlockSpec((tT, D), lambda i, e: (i, 0)),
        out_shape=jax.ShapeDtypeStruct((T, D), F32),
        compiler_params=_params("arbitrary", "arbitrary"),
    )(x, u, v, G)


def _resid_kernel(x_ref, y_ref, gt_ref, o_ref):
    o_ref[0] = x_ref[0] + gt_ref[0] * y_ref[...]


def _resid_norm_kernel(x_ref, y_ref, gt_ref, g_ref, o_ref):
    x = x_ref[0] + gt_ref[0] * y_ref[...]
    o_ref[0] = x * lax.rsqrt(jnp.mean(x * x, axis=-1, keepdims=True) + RMS_EPS) * g_ref[...]


def _resid(x, y, row0, gt, g=None):
    B, L, D = x.shape
    tl = _pick(L, 256)
    assert row0 % tl == 0
    lb = L // tl
    blk = pl.BlockSpec((1, tl, D), lambda b, l: (b, l, 0))
    in_specs = [blk, pl.BlockSpec((tl, D), lambda b, l: (row0 // tl + b * lb + l, 0)),
                _mod_spec(gt, tl, D)]
    args = [x, y, gt]
    if g is not None:
        in_specs.append(pl.BlockSpec((1, D), lambda b, l: (0, 0)))
        args.append(g)
    return pl.pallas_call(
        _resid_kernel if g is None else _resid_norm_kernel,
        name="resid",
        grid=(B, L // tl),
        in_specs=in_specs,
        out_specs=blk,
        out_shape=jax.ShapeDtypeStruct((B, L, D), F32),
        compiler_params=_params("arbitrary", "arbitrary"),
    )(*args)


def kernel(x_prompt, x_sample, state_conv, state_lru_h, state_gla, c_prompt, c_sample, w_ada, b_ada, g_mix, w_in, conv_w, conv_b, lru_w_a, lru_b_a, lru_w_x, lru_b_x, lru_lambda, gla_w_alpha, gla_b_alpha, gla_g_norm, w_out, g_ffn, peer_w_q, peer_sub_keys, peer_u, peer_v, g_final):
    depth = w_ada.shape[0]
    Bp, L, D = x_prompt.shape
    Bs = x_sample.shape[0]
    assert x_sample.shape[1] == 1
    W = conv_w.shape[2]
    nt = conv_w.shape[1] - 1
    _, _, H, DK, DV = state_gla.shape
    rank = gla_w_alpha.shape[1]
    PH, _, NK, _ = peer_sub_keys.shape[1:]
    n_main = 2 * W + 2 * H * DK + 2 * H * DV
    assert w_in.shape[2] == n_main + rank and rank <= LANES
    offs = (2 * W, 2 * W + H * DK, 2 * W + 2 * H * DK, 2 * W + 2 * H * DK + H * DV)
    Tp = Bp * L

    R = Bp + Bs
    Rp = -(-R // SUBLANES) * SUBLANES
    c_all = jnp.pad(jnp.concatenate([c_prompt, c_sample], axis=0), ((0, Rp - R), (0, 0)))

    x_p = x_prompt
    x_s = x_sample.reshape(1, Bs, D)
    outs = [[] for _ in range(6)]
    for l in range(depth):
        ada = _ada(c_all, w_ada[l], b_ada[l][None])
        mods_p = [ada[:Bp, i * D:(i + 1) * D][:, None, :] for i in range(6)]
        mods_s = [ada[Bp:R, i * D:(i + 1) * D][None] for i in range(6)]

        w_in_main = w_in[l][:, :n_main].astype(BF16)
        w_in_lr = jnp.pad(w_in[l][:, n_main:], ((0, 0), (0, LANES - rank))).astype(BF16)
        w_al = jnp.pad(gla_w_alpha[l], ((0, LANES - rank), (0, 0))).astype(BF16)
        b_al = gla_b_alpha[l][None]
        w_out_b = w_out[l].astype(BF16)
        wa = lru_w_a[l].astype(BF16)
        wx = lru_w_x[l].astype(BF16)
        lru_vecs = (conv_w[l], conv_b[l][None], wa, lru_b_a[l][None], wx, lru_b_x[l][None],
                    lru_lambda[l][None])
        gn = gla_g_norm[l]

        sh1, sc1, gt1, sh2, sc2, gt2 = mods_p
        hn = _norm_mod(x_p, g_mix[l][None], sc1, sh1, BF16).reshape(Tp, D)
        proj = _matmul(hn, w_in_main).reshape(Bp, L, n_main)
        la = _gla_gate(hn, w_in_lr, w_al, b_al).reshape(Bp, L, H * DK)
        y_lru, h_p, conv_p = _lru_prompt(
            proj, jnp.zeros((Bp, nt, W), F32), jnp.zeros((Bp, 1, W), F32), *lru_vecs, W)
        y_gla, s_p = _gla_prompt(proj, la, jnp.zeros((Bp, H, DK, DV), F32), gn[None], offs)
        x1_p = _outproj(y_lru, y_gla, w_out_b, x_p, gt1)
        sc2_p, sh2_p, gt2_p = sc2, sh2, gt2

        sh1, sc1, gt1, sh2, sc2, gt2 = mods_s
        hn = _norm_mod(x_s, g_mix[l][None], sc1, sh1, BF16).reshape(Bs, D)
        proj = _matmul(hn, w_in_main)
        la = _gla_gate(hn, w_in_lr, w_al, b_al)
        tail = jnp.transpose(state_conv[l], (1, 0, 2))
        y_lru, h_s = _lru_decode(proj, tail, state_lru_h[l], *lru_vecs, W)
        conv_s = jnp.concatenate([state_conv[l][:, 1:], proj[:, None, :W]], axis=1)
        qo, ko, vo, ro = offs
        heads = lambda a, d: a.reshape(Bs, H, 1, d)
        y_gla, s_s = _gla_decode(
            heads(proj[:, qo:ko], DK), heads(proj[:, ko:vo], DK), heads(proj[:, vo:ro], DV),
            heads(proj[:, ro:n_main], DV), heads(la, DK), state_gla[l], gn.reshape(H, 1, DV))
        x1_s = _outproj(y_lru[None], y_gla.reshape(1, Bs, H * DV), w_out_b, x_s, gt1)
        gt2_s = gt2

        hn2 = _norm_mod_cat(x1_p, x1_s, g_ffn[l][None], sc2_p, sh2_p, sc2, sh2, BF16)
        T = Tp + Bs
        q = _matmul(hn2, peer_w_q[l].astype(BF16), tm_pref=640)
        e1, e2, gate = _peer_topk(q, peer_sub_keys[l])
        G = _peer_gates(e1.reshape(PH * PEER_TOPK, T), e2.reshape(PH * PEER_TOPK, T),
                        gate.reshape(PH * PEER_TOPK, T), NK)
        y = _peer_dense(hn2, peer_u[l].astype(BF16), peer_v[l].astype(BF16), G)
        last = l == depth - 1
        x_p = _resid(x1_p, y, 0, gt2_p, g_final[None] if last else None)
        x_s = _resid(x1_s, y, Tp, gt2_s, g_final[None] if last else None)

        for lst, val in zip(outs, (conv_p, h_p.reshape(Bp, W), s_p, conv_s, h_s, s_s)):
            lst.append(val)

    return (x_p, x_s.reshape(Bs, 1, D)) + tuple(jnp.stack(o) for o in outs)
```

```python
import functools

import jax
import jax.numpy as jnp
from jax import lax
from jax.experimental import pallas as pl
from jax.experimental.pallas import tpu as pltpu

F32 = jnp.float32
BF16 = jnp.bfloat16

RMS_EPS = 1e-6
RG_C = 8.0
GLA_TAU = 16.0
PEER_TOPK = 16
GLA_CHUNK = 128
GLA_SUB = 16
LANES = 128
SUBLANES = 8
VMEM_LIMIT = 56 * 1024 * 1024


def _params(*sem):
    return pltpu.CompilerParams(dimension_semantics=sem, vmem_limit_bytes=VMEM_LIMIT)


def _pick(n, pref, mult=SUBLANES):
    best = None
    for d in range(mult, min(n, pref) + 1, mult):
        if n % d == 0:
            best = d
    return n if best is None else best


def _softplus(y):
    return jnp.maximum(y, 0.0) + jnp.log1p(jnp.exp(-jnp.abs(y)))


def _log_sigmoid(z):
    return jnp.minimum(z, 0.0) - jnp.log1p(jnp.exp(-jnp.abs(z)))


def _row_to_col(row, n):
    r = lax.broadcasted_iota(jnp.int32, (n, n), 0)
    c = lax.broadcasted_iota(jnp.int32, (n, n), 1)
    return jnp.sum(jnp.where(r == c, jnp.broadcast_to(row, (n, n)), 0.0), axis=1, keepdims=True)


def _ada_kernel(c_ref, w_ref, b_ref, o_ref):
    a = jax.nn.silu(c_ref[...]).astype(BF16)
    o_ref[...] = jnp.dot(a, w_ref[...].astype(BF16), preferred_element_type=F32) + b_ref[...]


def _ada(c, w, b):
    R, D = c.shape
    N = w.shape[1]
    tn = _pick(N, 512, LANES)
    return pl.pallas_call(
        _ada_kernel,
        name="ada",
        grid=(N // tn,),
        in_specs=[pl.BlockSpec((R, D), lambda j: (0, 0)),
                  pl.BlockSpec((D, tn), lambda j: (0, j)),
                  pl.BlockSpec((1, tn), lambda j: (0, j))],
        out_specs=pl.BlockSpec((R, tn), lambda j: (0, j)),
        out_shape=jax.ShapeDtypeStruct((R, N), F32),
        compiler_params=_params("arbitrary"),
    )(c, w, b)


def _norm_mod_kernel(x_ref, g_ref, sc_ref, sh_ref, o_ref):
    x = x_ref[0]
    y = x * lax.rsqrt(jnp.mean(x * x, axis=-1, keepdims=True) + RMS_EPS) * g_ref[...]
    o_ref[0] = (y * (1.0 + sc_ref[0]) + sh_ref[0]).astype(o_ref.dtype)


def _mod_spec(m, tl, D):
    if m.shape[1] == 1:
        return pl.BlockSpec((1, 1, D), lambda b, l: (b, 0, 0))
    return pl.BlockSpec((1, tl, D), lambda b, l: (b, l, 0))


def _norm_mod(x, g, sc, sh, out_dtype):
    B, L, D = x.shape
    tl = _pick(L, 256)
    return pl.pallas_call(
        _norm_mod_kernel,
        name="norm_mod",
        grid=(B, L // tl),
        in_specs=[pl.BlockSpec((1, tl, D), lambda b, l: (b, l, 0)),
                  pl.BlockSpec((1, D), lambda b, l: (0, 0)),
                  _mod_spec(sc, tl, D), _mod_spec(sh, tl, D)],
        out_specs=pl.BlockSpec((1, tl, D), lambda b, l: (b, l, 0)),
        out_shape=jax.ShapeDtypeStruct((B, L, D), out_dtype),
        compiler_params=_params("arbitrary", "arbitrary"),
    )(x, g, sc, sh)


def _norm_mod_cat_kernel(xp_ref, xs_ref, g_ref, scp_ref, shp_ref, scs_ref, shs_ref, o_ref, *, n_p):
    i = pl.program_id(0)

    def f(x, sc, sh):
        y = x * lax.rsqrt(jnp.mean(x * x, axis=-1, keepdims=True) + RMS_EPS) * g_ref[...]
        return (y * (1.0 + sc) + sh).astype(o_ref.dtype)

    @pl.when(i < n_p)
    def _():
        o_ref[...] = f(xp_ref[0], scp_ref[0], shp_ref[0])

    @pl.when(i >= n_p)
    def _():
        o_ref[...] = f(xs_ref[0], scs_ref[0], shs_ref[0])


def _norm_mod_cat(xp, xs, g, scp, shp, scs, shs, out_dtype):
    Bp, L, D = xp.shape
    Bs = xs.shape[1]
    tl = _pick(Bs, 256)
    assert L % tl == 0
    lb = L // tl
    n_p = Bp * lb
    n_s = Bs // tl
    pb = lambda i: jnp.minimum(i, n_p - 1)
    sb = lambda i: jnp.maximum(i - n_p, 0)
    pmod = pl.BlockSpec((1, 1, D), lambda i: (pb(i) // lb, 0, 0))
    smod = pl.BlockSpec((1, tl, D), lambda i: (0, sb(i), 0))
    return pl.pallas_call(
        functools.partial(_norm_mod_cat_kernel, n_p=n_p),
        name="norm_mod_cat",
        grid=(n_p + n_s,),
        in_specs=[pl.BlockSpec((1, tl, D), lambda i: (pb(i) // lb, pb(i) % lb, 0)),
                  pl.BlockSpec((1, tl, D), lambda i: (0, sb(i), 0)),
                  pl.BlockSpec((1, D), lambda i: (0, 0)),
                  pmod, pmod, smod, smod],
        out_specs=pl.BlockSpec((tl, D), lambda i: (i, 0)),
        out_shape=jax.ShapeDtypeStruct((Bp * L + Bs, D), out_dtype),
        compiler_params=_params("arbitrary"),
    )(xp, xs, g, scp, shp, scs, shs)


def _mm_kernel(x_ref, w_ref, o_ref, wb):
    @pl.when(pl.program_id(1) == 0)
    def _():
        wb[...] = w_ref[...].astype(BF16)

    o_ref[...] = jnp.dot(x_ref[...], wb[...], preferred_element_type=F32)


def _matmul(x, w, N=None, tm_pref=256, tn_pref=1024):
    M, K = x.shape
    N = w.shape[1] if N is None else N
    tm = _pick(M, tm_pref)
    tn = _pick(N, tn_pref, LANES)
    return pl.pallas_call(
        _mm_kernel,
        name="matmul",
        grid=(N // tn, M // tm),
        in_specs=[pl.BlockSpec((tm, K), lambda j, i: (i, 0)),
                  pl.BlockSpec((K, tn), lambda j, i: (0, j))],
        out_specs=pl.BlockSpec((tm, tn), lambda j, i: (i, j)),
        out_shape=jax.ShapeDtypeStruct((M, N), F32),
        scratch_shapes=[pltpu.VMEM((K, tn), BF16)],
        compiler_params=_params("arbitrary", "arbitrary"),
    )(x, w)


def _gate_kernel(x_ref, w1_ref, w2_ref, b2_ref, o_ref):
    a = jnp.dot(x_ref[...], w1_ref[...], preferred_element_type=F32)
    z = jnp.dot(a.astype(BF16), w2_ref[...], preferred_element_type=F32) + b2_ref[...]
    o_ref[...] = _log_sigmoid(z) / GLA_TAU


def _gla_gate(hn, w1, w2, b2):
    M, K = hn.shape
    R = w1.shape[1]
    N = w2.shape[1]
    tm = _pick(M, 512)
    return pl.pallas_call(
        _gate_kernel,
        name="gla_gate",
        grid=(M // tm,),
        in_specs=[pl.BlockSpec((tm, K), lambda i: (i, 0)),
                  pl.BlockSpec((K, R), lambda i: (0, 0)),
                  pl.BlockSpec((R, N), lambda i: (0, 0)),
                  pl.BlockSpec((1, N), lambda i: (0, 0))],
        out_specs=pl.BlockSpec((tm, N), lambda i: (i, 0)),
        out_shape=jax.ShapeDtypeStruct((M, N), F32),
        compiler_params=_params("arbitrary"),
    )(hn, w1, w2, b2)


def _lru_gates(xc, wa_ref, ba, wx_ref, bx, lam):
    nb, bd = wa_ref.shape[0], wa_ref.shape[1]
    xcb = xc.astype(BF16)
    ra, ri = [], []
    for n in range(nb):
        xs = xcb[:, n * bd:(n + 1) * bd]
        ra.append(jnp.dot(xs, wa_ref[n], preferred_element_type=F32))
        ri.append(jnp.dot(xs, wx_ref[n], preferred_element_type=F32))
    r = jax.nn.sigmoid(jnp.concatenate(ra, axis=1) + ba)
    i = jax.nn.sigmoid(jnp.concatenate(ri, axis=1) + bx)
    log_a = -RG_C * r * _softplus(-lam)
    a = jnp.exp(log_a)
    u = jnp.sqrt(-jnp.tanh(log_a) * (a * a + 1.0)) * i * xc
    return a, u


def _lru_prompt_kernel(xb_ref, gb_ref, tail_ref, h0_ref, cw_ref, cb_ref, wa_ref, ba_ref, wx_ref,
                       bx_ref, lam_ref, y_ref, hl_ref, conv_ref, buf, hcar):
    l = pl.program_id(2)
    nl = pl.num_programs(2)
    tl, tw = xb_ref.shape[1], xb_ref.shape[2]
    nt = tail_ref.shape[1]
    base = SUBLANES

    @pl.when(l == 0)
    def _():
        buf[base - nt:base, :] = tail_ref[0]
        hcar[...] = h0_ref[0]

    xb = xb_ref[0]
    buf[base:base + tl, :] = xb
    cw = cw_ref[...]
    xc = cb_ref[...] + xb * cw[nt:nt + 1]
    for j in range(nt):
        xc = xc + buf[base - nt + j:base - nt + j + tl, :] * cw[j:j + 1]
    new_tail = buf[base + tl - nt:base + tl, :]

    a, u = _lru_gates(xc, wa_ref, ba_ref[...], wx_ref, bx_ref[...], lam_ref[...])

    row = lax.broadcasted_iota(jnp.int32, (tl, tw), 0) & (SUBLANES - 1)
    A, Bv = a, u
    s = 1
    while s < SUBLANES:
        As = pltpu.roll(A, s, axis=0)
        Bs = pltpu.roll(Bv, s, axis=0)
        m = row >= s
        Bv = jnp.where(m, A * Bs + Bv, Bv)
        A = jnp.where(m, A * As, A)
        s *= 2
    h = hcar[...]
    outs = []
    for g in range(tl // SUBLANES):
        hg = A[g * SUBLANES:(g + 1) * SUBLANES] * h + Bv[g * SUBLANES:(g + 1) * SUBLANES]
        outs.append(hg)
        h = hg[SUBLANES - 1:SUBLANES]
    hseq = jnp.concatenate(outs, axis=0)
    hcar[...] = h
    buf[base - nt:base, :] = new_tail
    y_ref[0] = (hseq * jax.nn.gelu(gb_ref[0])).astype(y_ref.dtype)

    @pl.when(l == nl - 1)
    def _():
        hl_ref[0] = h
        conv_ref[0] = new_tail


def _lru_prompt(proj, tail, h0, cw, cb, wa, ba, wx, bx, lam, W):
    B, L, _ = proj.shape
    nt = tail.shape[1]
    bd = wa.shape[1]
    tw = _pick(W, 512, bd)
    tl = _pick(L, 256)
    nwb = W // tw
    col = lambda b, j, l: (0, j)
    return pl.pallas_call(
        _lru_prompt_kernel,
        name="lru_prompt",
        grid=(B, nwb, L // tl),
        in_specs=[pl.BlockSpec((1, tl, tw), lambda b, j, l: (b, l, j)),
                  pl.BlockSpec((1, tl, tw), lambda b, j, l: (b, l, j + nwb)),
                  pl.BlockSpec((1, nt, tw), lambda b, j, l: (b, 0, j)),
                  pl.BlockSpec((1, 1, tw), lambda b, j, l: (b, 0, j)),
                  pl.BlockSpec((nt + 1, tw), col),
                  pl.BlockSpec((1, tw), col),
                  pl.BlockSpec((tw // bd, bd, bd), lambda b, j, l: (j, 0, 0)),
                  pl.BlockSpec((1, tw), col),
                  pl.BlockSpec((tw // bd, bd, bd), lambda b, j, l: (j, 0, 0)),
                  pl.BlockSpec((1, tw), col),
                  pl.BlockSpec((1, tw), col)],
        out_specs=[pl.BlockSpec((1, tl, tw), lambda b, j, l: (b, l, j)),
                   pl.BlockSpec((1, 1, tw), lambda b, j, l: (b, 0, j)),
                   pl.BlockSpec((1, nt, tw), lambda b, j, l: (b, 0, j))],
        out_shape=[jax.ShapeDtypeStruct((B, L, W), BF16),
                   jax.ShapeDtypeStruct((B, 1, W), F32),
                   jax.ShapeDtypeStruct((B, nt, W), F32)],
        scratch_shapes=[pltpu.VMEM((tl + SUBLANES, tw), F32), pltpu.VMEM((1, tw), F32)],
        compiler_params=_params("arbitrary", "arbitrary", "arbitrary"),
    )(proj, proj, tail, h0, cw, cb, wa, ba, wx, bx, lam)


def _lru_decode_kernel(xb_ref, gb_ref, tail_ref, h0_ref, cw_ref, cb_ref, wa_ref, ba_ref, wx_ref,
                       bx_ref, lam_ref, y_ref, h_ref):
    nt = tail_ref.shape[0]
    xb = xb_ref[...]
    cw = cw_ref[...]
    xc = cb_ref[...] + xb * cw[nt:nt + 1]
    for j in range(nt):
        xc = xc + tail_ref[j] * cw[j:j + 1]
    a, u = _lru_gates(xc, wa_ref, ba_ref[...], wx_ref, bx_ref[...], lam_ref[...])
    h = a * h0_ref[...] + u
    h_ref[...] = h
    y_ref[...] = (h * jax.nn.gelu(gb_ref[...])).astype(y_ref.dtype)


def _lru_decode(proj, tail, h0, cw, cb, wa, ba, wx, bx, lam, W):
    B = proj.shape[0]
    nt = tail.shape[0]
    bd = wa.shape[1]
    tw = _pick(W, 512, bd)
    nwb = W // tw
    col = lambda j: (0, j)
    return pl.pallas_call(
        _lru_decode_kernel,
        name="lru_decode",
        grid=(nwb,),
        in_specs=[pl.BlockSpec((B, tw), col),
                  pl.BlockSpec((B, tw), lambda j: (0, j + nwb)),
                  pl.BlockSpec((nt, B, tw), lambda j: (0, 0, j)),
                  pl.BlockSpec((B, tw), col),
                  pl.BlockSpec((nt + 1, tw), col),
                  pl.BlockSpec((1, tw), col),
                  pl.BlockSpec((tw // bd, bd, bd), lambda j: (j, 0, 0)),
                  pl.BlockSpec((1, tw), col),
                  pl.BlockSpec((tw // bd, bd, bd), lambda j: (j, 0, 0)),
                  pl.BlockSpec((1, tw), col),
                  pl.BlockSpec((1, tw), col)],
        out_specs=[pl.BlockSpec((B, tw), col), pl.BlockSpec((B, tw), col)],
        out_shape=[jax.ShapeDtypeStruct((B, W), BF16), jax.ShapeDtypeStruct((B, W), F32)],
        compiler_params=_params("arbitrary"),
    )(proj, proj, tail, h0, cw, cb, wa, ba, wx, bx, lam)


def _head_norm_gate(o, gn, r):
    on = o * lax.rsqrt(jnp.mean(o * o, axis=-1, keepdims=True) + RMS_EPS) * gn
    return on * jax.nn.silu(r)


def _gla_prompt_kernel(q_ref, k_ref, v_ref, r_ref, la_ref, s0_ref, gn_ref, y_ref, so_ref, S, att):
    c = pl.program_id(2)
    nc = pl.num_programs(2)
    C, DK = q_ref.shape[1], q_ref.shape[2]
    sub = min(GLA_SUB, C)

    @pl.when(c == 0)
    def _():
        S[...] = s0_ref[0, 0]

    q = q_ref[0] * (DK ** -0.5)
    k = k_ref[0]
    vb = v_ref[0].astype(BF16)
    g = la_ref[0]
    rr = lax.broadcasted_iota(jnp.int32, (C, C), 0)
    cc = lax.broadcasted_iota(jnp.int32, (C, C), 1)
    b = jnp.dot((rr >= cc).astype(F32), g, precision=lax.Precision.HIGHEST,
                preferred_element_type=F32)
    b_last = b[C - 1:C]

    inter = jnp.dot((q * jnp.exp(b)).astype(BF16), S[...].astype(BF16), preferred_element_type=F32)

    lane = lax.broadcasted_iota(jnp.int32, (sub, sub), 1)
    srow = lax.broadcasted_iota(jnp.int32, (sub, sub), 0)
    for I in range(C // sub):
        lo = I * sub
        qI, kI, bI = q[lo:lo + sub], k[lo:lo + sub], b[lo:lo + sub]
        acc = jnp.zeros((sub, sub), F32)
        for s in range(sub):
            w = qI * kI[s:s + 1] * jnp.exp(jnp.minimum(bI - bI[s:s + 1], 0.0))
            acc = jnp.where(lane == s, jnp.sum(w, axis=1, keepdims=True), acc)
        att[lo:lo + sub, lo:lo + sub] = jnp.where(srow >= lane, acc, 0.0)
        if lo > 0:
            ref = b[lo - 1:lo]
            qt = (qI * jnp.exp(bI - ref)).astype(BF16)
            kt = (k[0:lo] * jnp.exp(ref - b[0:lo])).astype(BF16)
            att[lo:lo + sub, 0:lo] = lax.dot_general(qt, kt, (((1,), (1,)), ((), ())),
                                                     preferred_element_type=F32)
        if lo + sub < C:
            att[lo:lo + sub, lo + sub:C] = jnp.zeros((sub, C - lo - sub), F32)
    o = inter + jnp.dot(att[...].astype(BF16), vb, preferred_element_type=F32)

    kh = (k * jnp.exp(b_last - b)).astype(BF16)
    upd = lax.dot_general(kh, vb, (((0,), (0,)), ((), ())), preferred_element_type=F32)
    S[...] = _row_to_col(jnp.exp(b_last), DK) * S[...] + upd

    y_ref[0] = _head_norm_gate(o, gn_ref[...], r_ref[0]).astype(y_ref.dtype)

    @pl.when(c == nc - 1)
    def _():
        so_ref[0, 0] = S[...]


def _gla_prompt(proj, la, s0, gn, offs):
    B, L, _ = proj.shape
    _, H, DK, DV = s0.shape
    C = _pick(L, GLA_CHUNK)
    qo, ko, vo, ro = offs
    return pl.pallas_call(
        _gla_prompt_kernel,
        name="gla_prompt",
        grid=(B, H, L // C),
        in_specs=[pl.BlockSpec((1, C, DK), lambda b, h, c: (b, c, qo // DK + h)),
                  pl.BlockSpec((1, C, DK), lambda b, h, c: (b, c, ko // DK + h)),
                  pl.BlockSpec((1, C, DV), lambda b, h, c: (b, c, vo // DV + h)),
                  pl.BlockSpec((1, C, DV), lambda b, h, c: (b, c, ro // DV + h)),
                  pl.BlockSpec((1, C, DK), lambda b, h, c: (b, c, h)),
                  pl.BlockSpec((1, 1, DK, DV), lambda b, h, c: (b, h, 0, 0)),
                  pl.BlockSpec((1, DV), lambda b, h, c: (0, h))],
        out_specs=[pl.BlockSpec((1, C, DV), lambda b, h, c: (b, c, h)),
                   pl.BlockSpec((1, 1, DK, DV), lambda b, h, c: (b, h, 0, 0))],
        out_shape=[jax.ShapeDtypeStruct((B, L, H * DV), BF16),
                   jax.ShapeDtypeStruct((B, H, DK, DV), F32)],
        scratch_shapes=[pltpu.VMEM((DK, DV), F32), pltpu.VMEM((C, C), F32)],
        compiler_params=_params("arbitrary", "arbitrary", "arbitrary"),
    )(proj, proj, proj, proj, la, s0, gn)


def _gla_decode_kernel(q_ref, k_ref, v_ref, r_ref, la_ref, s0_ref, gn_ref, y_ref, so_ref):
    bb, H, _, DK = q_ref.shape
    for b in range(bb):
        for h in range(H):
            q = q_ref[b, h] * (DK ** -0.5)
            S = (_row_to_col(jnp.exp(la_ref[b, h]), DK) * s0_ref[b, h]
                 + _row_to_col(k_ref[b, h], DK) * v_ref[b, h])
            so_ref[b, h] = S
            o = jnp.sum(_row_to_col(q, DK) * S, axis=0, keepdims=True)
            y_ref[b, h] = _head_norm_gate(o, gn_ref[h], r_ref[b, h]).astype(y_ref.dtype)


def _gla_decode(q, k, v, r, la, s0, gn):
    B, H, DK, DV = s0.shape
    bb = 2 if B % 2 == 0 else 1
    vec = lambda d: pl.BlockSpec((bb, H, 1, d), lambda b: (b, 0, 0, 0))
    mat = pl.BlockSpec((bb, H, DK, DV), lambda b: (b, 0, 0, 0))
    return pl.pallas_call(
        _gla_decode_kernel,
        name="gla_decode",
        grid=(B // bb,),
        in_specs=[vec(DK), vec(DK), vec(DV), vec(DV), vec(DK), mat,
                  pl.BlockSpec((H, 1, DV), lambda b: (0, 0, 0))],
        out_specs=[vec(DV), mat],
        out_shape=[jax.ShapeDtypeStruct((B, H, 1, DV), BF16),
                   jax.ShapeDtypeStruct((B, H, DK, DV), F32)],
        compiler_params=_params("arbitrary"),
    )(q, k, v, r, la, s0, gn)


def _outproj_kernel(ya_ref, yb_ref, wa_ref, wb_ref, x_ref, gt_ref, o_ref, wab, wbb):
    @pl.when((pl.program_id(1) == 0) & (pl.program_id(2) == 0))
    def _():
        wab[...] = wa_ref[...].astype(BF16)
        wbb[...] = wb_ref[...].astype(BF16)

    mix = (jnp.dot(ya_ref[0], wab[...], preferred_element_type=F32)
           + jnp.dot(yb_ref[0], wbb[...], preferred_element_type=F32))
    o_ref[0] = x_ref[0] + gt_ref[0] * mix


def _outproj(ya, yb, w, x, gt):
    B, L, D = x.shape
    Ka, Kb = ya.shape[2], yb.shape[2]
    assert Ka == Kb
    tm = _pick(L, 256)
    tn = _pick(D, 1024, LANES)
    gspec = (pl.BlockSpec((1, 1, tn), lambda j, b, l: (b, 0, j)) if gt.shape[1] == 1
             else pl.BlockSpec((1, tm, tn), lambda j, b, l: (b, l, j)))
    return pl.pallas_call(
        _outproj_kernel,
        name="outproj",
        grid=(D // tn, B, L // tm),
        in_specs=[pl.BlockSpec((1, tm, Ka), lambda j, b, l: (b, l, 0)),
                  pl.BlockSpec((1, tm, Kb), lambda j, b, l: (b, l, 0)),
                  pl.BlockSpec((Ka, tn), lambda j, b, l: (0, j)),
                  pl.BlockSpec((Kb, tn), lambda j, b, l: (1, j)),
                  pl.BlockSpec((1, tm, tn), lambda j, b, l: (b, l, j)),
                  gspec],
        out_specs=pl.BlockSpec((1, tm, tn), lambda j, b, l: (b, l, j)),
        out_shape=jax.ShapeDtypeStruct((B, L, D), F32),
        scratch_shapes=[pltpu.VMEM((Ka, tn), BF16), pltpu.VMEM((Kb, tn), BF16)],
        compiler_params=_params("arbitrary", "arbitrary", "arbitrary"),
    )(ya, yb, w, w, x, gt)


def _topk_rows(problems, ids, kk, fill):
    ss = [p[0] for p in problems]
    for j in range(kk):
        for n, (_, val_ref, idx_ref) in enumerate(problems):
            s = ss[n]
            m = jnp.max(s, axis=0, keepdims=True)
            idx = jnp.min(jnp.where(s == m, ids, fill), axis=0, keepdims=True)
            val_ref[j:j + 1, :] = m
            idx_ref[j:j + 1, :] = idx
            ss[n] = jnp.where(ids == idx, -jnp.inf, s)


def _pair_layout(K):
    segs, ids, r0 = [], [], 0
    b = 0
    while b < K and K // (b + 1) > 1:
        na = K // (b + 1)
        segs.append((r0, (0, na), (b, b + 1)))
        rows = -(-na // SUBLANES) * SUBLANES
        ids += [a * K + b for a in range(na)] + [K * K] * (rows - na)
        r0 += rows
        b += 1
    if b < K:
        nb = K - b
        segs.append((r0, (0, 1), (b, K)))
        rows = -(-nb // SUBLANES) * SUBLANES
        ids += list(range(b, K)) + [K * K] * (rows - nb)
        r0 += rows
    return segs, ids, r0


def _peer_topk_kernel(q_ref, keys_ref, pid_ref, e1_ref, e2_ref, g_ref, sv, si, cand, sc, pos,
                      *, segs):
    tT = q_ref.shape[0]
    H, _, NK, dq = keys_ref.shape
    K = PEER_TOPK
    hp = sv.shape[0]
    ids = lax.broadcasted_iota(jnp.int32, (NK, tT), 0).astype(F32)
    pids = pid_ref[...]

    def heads(i, carry):
        halves = []
        for n in range(hp):
            for p in range(2):
                h = i * hp + n
                off = pl.multiple_of((2 * h + p) * dq, dq)
                qp = q_ref[:, pl.ds(off, dq)].astype(BF16)
                s = lax.dot_general(keys_ref[h, p].astype(BF16), qp, (((1,), (1,)), ((), ())),
                                    preferred_element_type=F32)
                halves.append((s, sv.at[n, p], si.at[n, p]))
        _topk_rows(halves, ids, K, float(NK))
        pairs = []
        for n in range(hp):
            v0, v1 = sv[n, 0], sv[n, 1]
            cand[n] = jnp.full(cand.shape[1:], -jnp.inf, F32)
            for r0, (a0, a1), (b0, b1) in segs:
                rows = max(a1 - a0, b1 - b0)
                cand[n, r0:r0 + rows, :] = v0[a0:a1] + v1[b0:b1]
            pairs.append((cand[n], sc.at[n], pos.at[n]))
        _topk_rows(pairs, pids, K, float(K * K))
        for n in range(hp):
            h = i * hp + n
            pa = jnp.floor(pos[n] * (1.0 / K))
            pb = pos[n] - pa * K
            i0, i1 = si[n, 0], si[n, 1]
            e1 = jnp.zeros((K, tT), F32)
            e2 = jnp.zeros((K, tT), F32)
            for a in range(K):
                e1 = jnp.where(pa == a, i0[a:a + 1], e1)
                e2 = jnp.where(pb == a, i1[a:a + 1], e2)
            scv = sc[n]
            ex = jnp.exp(scv - jnp.max(scv, axis=0, keepdims=True))
            e1_ref[h] = e1
            e2_ref[h] = e2
            g_ref[h] = ex / jnp.sum(ex, axis=0, keepdims=True)
        return carry

    lax.fori_loop(0, H // hp, heads, 0)


def _peer_topk(q, keys):
    T = q.shape[0]
    H, _, NK, dq = keys.shape
    K = PEER_TOPK
    tT = _pick(T, LANES, LANES)
    hp = 4 if H % 4 == 0 else 1
    segs, ids, nr = _pair_layout(K)
    pids = jnp.broadcast_to(jnp.asarray(ids, F32)[:, None], (nr, tT))
    out = jax.ShapeDtypeStruct((H, K, T), F32)
    ospec = pl.BlockSpec((H, K, tT), lambda i: (0, 0, i))
    return pl.pallas_call(
        functools.partial(_peer_topk_kernel, segs=segs),
        name="peer_topk",
        grid=(T // tT,),
        in_specs=[pl.BlockSpec((tT, 2 * H * dq), lambda i: (i, 0)),
                  pl.BlockSpec((H, 2, NK, dq), lambda i: (0, 0, 0, 0)),
                  pl.BlockSpec((nr, tT), lambda i: (0, 0))],
        out_specs=[ospec, ospec, ospec],
        out_shape=[out, out, out],
        scratch_shapes=[pltpu.VMEM((hp, 2, K, tT), F32), pltpu.VMEM((hp, 2, K, tT), F32),
                        pltpu.VMEM((hp, nr, tT), F32), pltpu.VMEM((hp, K, tT), F32),
                        pltpu.VMEM((hp, K, tT), F32)],
        compiler_params=_params("arbitrary"),
    )(q, keys, pids)


def _peer_gate_kernel(e1_ref, e2_ref, g_ref, o_ref, e1t, e2t, gt, gbuf, *, unroll, pitch):
    NS, tT = e1_ref.shape
    NK = gbuf.shape[1]
    e1t[...] = e1_ref[...].T
    e2t[...] = e2_ref[...].T
    gt[...] = g_ref[...].T
    kid = lax.broadcasted_iota(jnp.int32, (NK, NS), 0).astype(F32)

    def body(i, carry):
        for j in range(unroll):
            t = i * unroll + j
            a = jnp.where(kid == e1t[pl.ds(t, 1), :], gt[pl.ds(t, 1), :], 0.0).astype(BF16)
            b = jnp.where(kid == e2t[pl.ds(t, 1), :], 1.0, 0.0).astype(BF16)
            gbuf[pl.ds(pl.multiple_of(t * pitch, SUBLANES), NK), :] = lax.dot_general(
                a, b, (((1,), (1,)), ((), ())), preferred_element_type=F32)
        return carry

    lax.fori_loop(0, tT // unroll, body, 0)
    for i1 in range(NK):
        o_ref[:, i1 * NK:(i1 + 1) * NK] = gbuf[pl.ds(i1, tT, stride=pitch), :].astype(o_ref.dtype)


def _peer_gates(e1, e2, g, NK):
    NS, T = e1.shape
    tT = _pick(T, 128, LANES)
    unroll = 16 if tT % 16 == 0 else 1
    pitch = NK + SUBLANES
    ispec = pl.BlockSpec((NS, tT), lambda i: (0, i))
    return pl.pallas_call(
        functools.partial(_peer_gate_kernel, unroll=unroll, pitch=pitch),
        name="peer_gate",
        grid=(T // tT,),
        in_specs=[ispec, ispec, ispec],
        out_specs=pl.BlockSpec((tT, NK * NK), lambda i: (i, 0)),
        out_shape=jax.ShapeDtypeStruct((T, NK * NK), BF16),
        scratch_shapes=[pltpu.VMEM((tT, NS), F32)] * 3 + [pltpu.VMEM((tT * pitch, NK), F32)],
        compiler_params=_params("arbitrary"),
    )(e1, e2, g)


def _peer_dense_kernel(x_ref, u_ref, v_ref, g_ref, o_ref, *, nsplit):
    e = pl.program_id(1)

    @pl.when(e == 0)
    def _():
        o_ref[...] = jnp.zeros(o_ref.shape, o_ref.dtype)

    rs = x_ref.shape[0] // nsplit
    for r in range(nsplit):
        rows = slice(r * rs, (r + 1) * rs)
        act = lax.dot_general(x_ref[rows, :], u_ref[...], (((1,), (1,)), ((), ())),
                              preferred_element_type=F32)
        p = (g_ref[rows, :].astype(F32) * jax.nn.gelu(act)).astype(BF16)
        o_ref[rows, :] += jnp.dot(p, v_ref[...], preferred_element_type=F32)


def _peer_dense(x, u, v, G):
    T, D = x.shape
    E = u.shape[0]
    tT = _pick(T, 640, 16)
    tE = _pick(E, 512, LANES)
    nsplit = 2 if tT % 32 == 0 else 1
    return pl.pallas_call(
        functools.partial(_peer_dense_kernel, nsplit=nsplit),
        name="peer_dense",
        grid=(T // tT, E // tE),
        in_specs=[pl.BlockSpec((tT, D), lambda i, e: (i, 0)),
                  pl.BlockSpec((tE, D), lambda i, e: (e, 0)),
                  pl.BlockSpec((tE, D), lambda i, e: (e, 0)),
                  pl.BlockSpec((tT, tE), lambda i, e: (i, e))],
        out_specs=pl.BlockSpec((tT, D), lambda i, e: (i, 0)),
        out_shape=jax.ShapeDtypeStruct((T, D), F32),
        compiler_params=_params("arbitrary", "arbitrary"),
    )(x, u, v, G)


def _resid_kernel(x_ref, y_ref, gt_ref, o_ref):
    o_ref[0] = x_ref[0] + gt_ref[0] * y_ref[...]


def _resid_norm_kernel(x_ref, y_ref, gt_ref, g_ref, o_ref):
    x = x_ref[0] + gt_ref[0] * y_ref[...]
    o_ref[0] = x * lax.rsqrt(jnp.mean(x * x, axis=-1, keepdims=True) + RMS_EPS) * g_ref[...]


def _resid(x, y, row0, gt, g=None):
    B, L, D = x.shape
    tl = _pick(L, 256)
    assert row0 % tl == 0
    lb = L // tl
    blk = pl.BlockSpec((1, tl, D), lambda b, l: (b, l, 0))
    in_specs = [blk, pl.BlockSpec((tl, D), lambda b, l: (row0 // tl + b * lb + l, 0)),
                _mod_spec(gt, tl, D)]
    args = [x, y, gt]
    if g is not None:
        in_specs.append(pl.BlockSpec((1, D), lambda b, l: (0, 0)))
        args.append(g)
    return pl.pallas_call(
        _resid_kernel if g is None else _resid_norm_kernel,
        name="resid",
        grid=(B, L // tl),
        in_specs=in_specs,
        out_specs=blk,
        out_shape=jax.ShapeDtypeStruct((B, L, D), F32),
        compiler_params=_params("arbitrary", "arbitrary"),
    )(*args)


def kernel(x_prompt, x_sample, state_conv, state_lru_h, state_gla, c_prompt, c_sample, w_ada, b_ada, g_mix, w_in, conv_w, conv_b, lru_w_a, lru_b_a, lru_w_x, lru_b_x, lru_lambda, gla_w_alpha, gla_b_alpha, gla_g_norm, w_out, g_ffn, peer_w_q, peer_sub_keys, peer_u, peer_v, g_final):
    depth = w_ada.shape[0]
    Bp, L, D = x_prompt.shape
    Bs = x_sample.shape[0]
    assert x_sample.shape[1] == 1
    W = conv_w.shape[2]
    nt = conv_w.shape[1] - 1
    _, _, H, DK, DV = state_gla.shape
    rank = gla_w_alpha.shape[1]
    PH, _, NK, _ = peer_sub_keys.shape[1:]
    n_main = 2 * W + 2 * H * DK + 2 * H * DV
    assert w_in.shape[2] == n_main + rank and rank <= LANES
    offs = (2 * W, 2 * W + H * DK, 2 * W + 2 * H * DK, 2 * W + 2 * H * DK + H * DV)
    Tp = Bp * L

    R = Bp + Bs
    Rp = -(-R // SUBLANES) * SUBLANES
    c_all = jnp.pad(jnp.concatenate([c_prompt, c_sample], axis=0), ((0, Rp - R), (0, 0)))

    x_p = x_prompt
    x_s = x_sample.reshape(1, Bs, D)
    outs = [[] for _ in range(6)]
    for l in range(depth):
        ada = _ada(c_all, w_ada[l], b_ada[l][None])
        mods_p = [ada[:Bp, i * D:(i + 1) * D][:, None, :] for i in range(6)]
        mods_s = [ada[Bp:R, i * D:(i + 1) * D][None] for i in range(6)]

        w_in_lr = jnp.pad(w_in[l][:, n_main:], ((0, 0), (0, LANES - rank))).astype(BF16)
        w_al = jnp.pad(gla_w_alpha[l], ((0, LANES - rank), (0, 0))).astype(BF16)
        b_al = gla_b_alpha[l][None]
        wa = lru_w_a[l].astype(BF16)
        wx = lru_w_x[l].astype(BF16)
        lru_vecs = (conv_w[l], conv_b[l][None], wa, lru_b_a[l][None], wx, lru_b_x[l][None],
                    lru_lambda[l][None])
        gn = gla_g_norm[l]

        sh1, sc1, gt1, sh2, sc2, gt2 = mods_p
        hn = _norm_mod(x_p, g_mix[l][None], sc1, sh1, BF16).reshape(Tp, D)
        proj = _matmul(hn, w_in[l], n_main).reshape(Bp, L, n_main)
        la = _gla_gate(hn, w_in_lr, w_al, b_al).reshape(Bp, L, H * DK)
        y_lru, h_p, conv_p = _lru_prompt(
            proj, jnp.zeros((Bp, nt, W), F32), jnp.zeros((Bp, 1, W), F32), *lru_vecs, W)
        y_gla, s_p = _gla_prompt(proj, la, jnp.zeros((Bp, H, DK, DV), F32), gn[None], offs)
        x1_p = _outproj(y_lru, y_gla, w_out[l], x_p, gt1)
        sc2_p, sh2_p, gt2_p = sc2, sh2, gt2

        sh1, sc1, gt1, sh2, sc2, gt2 = mods_s
        hn = _norm_mod(x_s, g_mix[l][None], sc1, sh1, BF16).reshape(Bs, D)
        proj = _matmul(hn, w_in[l], n_main)
        la = _gla_gate(hn, w_in_lr, w_al, b_al)
        tail = jnp.transpose(state_conv[l], (1, 0, 2))
        y_lru, h_s = _lru_decode(proj, tail, state_lru_h[l], *lru_vecs, W)
        conv_s = jnp.concatenate([state_conv[l][:, 1:], proj[:, None, :W]], axis=1)
        qo, ko, vo, ro = offs
        heads = lambda a, d: a.reshape(Bs, H, 1, d)
        y_gla, s_s = _gla_decode(
            heads(proj[:, qo:ko], DK), heads(proj[:, ko:vo], DK), heads(proj[:, vo:ro], DV),
            heads(proj[:, ro:n_main], DV), heads(la, DK), state_gla[l], gn.reshape(H, 1, DV))
        x1_s = _outproj(y_lru[None], y_gla.reshape(1, Bs, H * DV), w_out[l], x_s, gt1)
        gt2_s = gt2

        hn2 = _norm_mod_cat(x1_p, x1_s, g_ffn[l][None], sc2_p, sh2_p, sc2, sh2, BF16)
        T = Tp + Bs
        q = _matmul(hn2, peer_w_q[l])
        e1, e2, gate = _peer_topk(q, peer_sub_keys[l])
        G = _peer_gates(e1.reshape(PH * PEER_TOPK, T), e2.reshape(PH * PEER_TOPK, T),
                        gate.reshape(PH * PEER_TOPK, T), NK)
        y = _peer_dense(hn2, peer_u[l].astype(BF16), peer_v[l].astype(BF16), G)
        last = l == depth - 1
        x_p = _resid(x1_p, y, 0, gt2_p, g_final[None] if last else None)
        x_s = _resid(x1_s, y, Tp, gt2_s, g_final[None] if last else None)

        for lst, val in zip(outs, (conv_p, h_p.reshape(Bp, W), s_p, conv_s, h_s, s_s)):
            lst.append(val)

    return (x_p, x_s.reshape(Bs, 1, D)) + tuple(jnp.stack(o) for o in outs)
```

```python
import functools

import jax
import jax.numpy as jnp
from jax import lax
from jax.experimental import pallas as pl
from jax.experimental.pallas import tpu as pltpu

F32 = jnp.float32
BF16 = jnp.bfloat16

RMS_EPS = 1e-6
RG_C = 8.0
GLA_TAU = 16.0
PEER_TOPK = 16
GLA_CHUNK = 128
GLA_SUB = 16
LANES = 128
SUBLANES = 8
VMEM_LIMIT = 56 * 1024 * 1024
IN_PROJ_VMEM_LIMIT = 60 * 1024 * 1024


def _params(*sem):
    return pltpu.CompilerParams(dimension_semantics=sem, vmem_limit_bytes=VMEM_LIMIT)


def _pick(n, pref, mult=SUBLANES):
    best = None
    for d in range(mult, min(n, pref) + 1, mult):
        if n % d == 0:
            best = d
    return n if best is None else best


def _softplus(y):
    return jnp.maximum(y, 0.0) + jnp.log1p(jnp.exp(-jnp.abs(y)))


def _log_sigmoid(z):
    return jnp.minimum(z, 0.0) - jnp.log1p(jnp.exp(-jnp.abs(z)))


def _row_to_col(row, n):
    r = lax.broadcasted_iota(jnp.int32, (n, n), 0)
    c = lax.broadcasted_iota(jnp.int32, (n, n), 1)
    return jnp.sum(jnp.where(r == c, jnp.broadcast_to(row, (n, n)), 0.0), axis=1, keepdims=True)


def _ada_kernel(c_ref, w_ref, b_ref, o_ref):
    a = jax.nn.silu(c_ref[...]).astype(BF16)
    o_ref[...] = jnp.dot(a, w_ref[...].astype(BF16), preferred_element_type=F32) + b_ref[...]


def _ada(c, w, b):
    R, D = c.shape
    N = w.shape[1]
    tn = _pick(N, 512, LANES)
    return pl.pallas_call(
        _ada_kernel,
        name="ada",
        grid=(N // tn,),
        in_specs=[pl.BlockSpec((R, D), lambda j: (0, 0)),
                  pl.BlockSpec((D, tn), lambda j: (0, j)),
                  pl.BlockSpec((1, tn), lambda j: (0, j))],
        out_specs=pl.BlockSpec((R, tn), lambda j: (0, j)),
        out_shape=jax.ShapeDtypeStruct((R, N), F32),
        compiler_params=_params("arbitrary"),
    )(c, w, b)


def _norm_mod_kernel(x_ref, g_ref, sc_ref, sh_ref, o_ref):
    x = x_ref[0]
    y = x * lax.rsqrt(jnp.mean(x * x, axis=-1, keepdims=True) + RMS_EPS) * g_ref[...]
    o_ref[0] = (y * (1.0 + sc_ref[0]) + sh_ref[0]).astype(o_ref.dtype)


def _mod_spec(m, tl, D):
    if m.shape[1] == 1:
        return pl.BlockSpec((1, 1, D), lambda b, l: (b, 0, 0))
    return pl.BlockSpec((1, tl, D), lambda b, l: (b, l, 0))


def _norm_mod(x, g, sc, sh, out_dtype):
    B, L, D = x.shape
    tl = _pick(L, 256)
    return pl.pallas_call(
        _norm_mod_kernel,
        name="norm_mod",
        grid=(B, L // tl),
        in_specs=[pl.BlockSpec((1, tl, D), lambda b, l: (b, l, 0)),
                  pl.BlockSpec((1, D), lambda b, l: (0, 0)),
                  _mod_spec(sc, tl, D), _mod_spec(sh, tl, D)],
        out_specs=pl.BlockSpec((1, tl, D), lambda b, l: (b, l, 0)),
        out_shape=jax.ShapeDtypeStruct((B, L, D), out_dtype),
        compiler_params=_params("arbitrary", "arbitrary"),
    )(x, g, sc, sh)


def _norm_mod_cat_kernel(xp_ref, xs_ref, g_ref, scp_ref, shp_ref, scs_ref, shs_ref, o_ref, *, n_p):
    i = pl.program_id(0)

    def f(x, sc, sh):
        y = x * lax.rsqrt(jnp.mean(x * x, axis=-1, keepdims=True) + RMS_EPS) * g_ref[...]
        return (y * (1.0 + sc) + sh).astype(o_ref.dtype)

    @pl.when(i < n_p)
    def _():
        o_ref[...] = f(xp_ref[0], scp_ref[0], shp_ref[0])

    @pl.when(i >= n_p)
    def _():
        o_ref[...] = f(xs_ref[0], scs_ref[0], shs_ref[0])


def _norm_mod_cat(xp, xs, g, scp, shp, scs, shs, out_dtype):
    Bp, L, D = xp.shape
    Bs = xs.shape[1]
    tl = _pick(Bs, 256)
    assert L % tl == 0
    lb = L // tl
    n_p = Bp * lb
    n_s = Bs // tl
    pb = lambda i: jnp.minimum(i, n_p - 1)
    sb = lambda i: jnp.maximum(i - n_p, 0)
    pmod = pl.BlockSpec((1, 1, D), lambda i: (pb(i) // lb, 0, 0))
    smod = pl.BlockSpec((1, tl, D), lambda i: (0, sb(i), 0))
    return pl.pallas_call(
        functools.partial(_norm_mod_cat_kernel, n_p=n_p),
        name="norm_mod_cat",
        grid=(n_p + n_s,),
        in_specs=[pl.BlockSpec((1, tl, D), lambda i: (pb(i) // lb, pb(i) % lb, 0)),
                  pl.BlockSpec((1, tl, D), lambda i: (0, sb(i), 0)),
                  pl.BlockSpec((1, D), lambda i: (0, 0)),
                  pmod, pmod, smod, smod],
        out_specs=pl.BlockSpec((tl, D), lambda i: (i, 0)),
        out_shape=jax.ShapeDtypeStruct((Bp * L + Bs, D), out_dtype),
        compiler_params=_params("arbitrary"),
    )(xp, xs, g, scp, shp, scs, shs)


def _mm_kernel(x_ref, w_ref, o_ref, wb, *, w_rows_are_outputs):
    @pl.when(pl.program_id(1) == 0)
    def _():
        wb[...] = w_ref[...].astype(BF16)

    dims = (((1,), (1,)), ((), ())) if w_rows_are_outputs else (((1,), (0,)), ((), ()))
    o_ref[...] = lax.dot_general(x_ref[...], wb[...], dims, preferred_element_type=F32)


def _matmul(x, w, N=None, *, w_rows_are_outputs=False, tm_pref=512, tn_pref=1024,
            vmem_limit=VMEM_LIMIT):
    M, K = x.shape
    n_all = w.shape[0] if w_rows_are_outputs else w.shape[1]
    N = n_all if N is None else N
    tm = _pick(M, tm_pref, 16)
    tn = _pick(N, tn_pref, LANES)
    wblk = (tn, K) if w_rows_are_outputs else (K, tn)
    wmap = (lambda j, i: (j, 0)) if w_rows_are_outputs else (lambda j, i: (0, j))
    return pl.pallas_call(
        functools.partial(_mm_kernel, w_rows_are_outputs=w_rows_are_outputs),
        name="matmul",
        grid=(N // tn, M // tm),
        in_specs=[pl.BlockSpec((tm, K), lambda j, i: (i, 0)), pl.BlockSpec(wblk, wmap)],
        out_specs=pl.BlockSpec((tm, tn), lambda j, i: (i, j)),
        out_shape=jax.ShapeDtypeStruct((M, N), F32),
        scratch_shapes=[pltpu.VMEM(wblk, BF16)],
        compiler_params=pltpu.CompilerParams(dimension_semantics=("arbitrary", "arbitrary"),
                                             vmem_limit_bytes=vmem_limit),
    )(x, w)


def _gate_kernel(x_ref, w1_ref, w2_ref, b2_ref, o_ref):
    a = lax.dot_general(x_ref[...], w1_ref[...].astype(BF16), (((1,), (1,)), ((), ())),
                        preferred_element_type=F32)
    z = jnp.dot(a.astype(BF16), w2_ref[...].astype(BF16), preferred_element_type=F32) + b2_ref[...]
    o_ref[...] = _log_sigmoid(z) / GLA_TAU


def _gla_gate(hn, w_t, row0, w2, b2):
    M, K = hn.shape
    R, N = w2.shape
    assert row0 % R == 0 and R % SUBLANES == 0
    tm = _pick(M, 512)
    return pl.pallas_call(
        _gate_kernel,
        name="gla_gate",
        grid=(M // tm,),
        in_specs=[pl.BlockSpec((tm, K), lambda i: (i, 0)),
                  pl.BlockSpec((R, K), lambda i: (row0 // R, 0)),
                  pl.BlockSpec((R, N), lambda i: (0, 0)),
                  pl.BlockSpec((1, N), lambda i: (0, 0))],
        out_specs=pl.BlockSpec((tm, N), lambda i: (i, 0)),
        out_shape=jax.ShapeDtypeStruct((M, N), F32),
        compiler_params=_params("arbitrary"),
    )(hn, w_t, w2, b2)


def _lru_gates(xc, wa_ref, ba, wx_ref, bx, lam):
    nb, bd = wa_ref.shape[0], wa_ref.shape[1]
    xcb = xc.astype(BF16)
    ra, ri = [], []
    for n in range(nb):
        xs = xcb[:, n * bd:(n + 1) * bd]
        ra.append(jnp.dot(xs, wa_ref[n], preferred_element_type=F32))
        ri.append(jnp.dot(xs, wx_ref[n], preferred_element_type=F32))
    r = jax.nn.sigmoid(jnp.concatenate(ra, axis=1) + ba)
    i = jax.nn.sigmoid(jnp.concatenate(ri, axis=1) + bx)
    log_a = -RG_C * r * _softplus(-lam)
    a = jnp.exp(log_a)
    u = jnp.sqrt(-jnp.tanh(log_a) * (a * a + 1.0)) * i * xc
    return a, u


def _lru_prompt_kernel(xb_ref, gb_ref, tail_ref, h0_ref, cw_ref, cb_ref, wa_ref, ba_ref, wx_ref,
                       bx_ref, lam_ref, y_ref, hl_ref, conv_ref, buf, hcar):
    l = pl.program_id(2)
    nl = pl.num_programs(2)
    tl, tw = xb_ref.shape[1], xb_ref.shape[2]
    nt = tail_ref.shape[1]
    base = SUBLANES

    @pl.when(l == 0)
    def _():
        buf[base - nt:base, :] = tail_ref[0]
        hcar[...] = h0_ref[0]

    xb = xb_ref[0]
    buf[base:base + tl, :] = xb
    cw = cw_ref[...]
    xc = cb_ref[...] + xb * cw[nt:nt + 1]
    for j in range(nt):
        xc = xc + buf[base - nt + j:base - nt + j + tl, :] * cw[j:j + 1]
    new_tail = buf[base + tl - nt:base + tl, :]

    a, u = _lru_gates(xc, wa_ref, ba_ref[...], wx_ref, bx_ref[...], lam_ref[...])

    row = lax.broadcasted_iota(jnp.int32, (tl, tw), 0) & (SUBLANES - 1)
    A, Bv = a, u
    s = 1
    while s < SUBLANES:
        As = pltpu.roll(A, s, axis=0)
        Bs = pltpu.roll(Bv, s, axis=0)
        m = row >= s
        Bv = jnp.where(m, A * Bs + Bv, Bv)
        A = jnp.where(m, A * As, A)
        s *= 2
    h = hcar[...]
    outs = []
    for g in range(tl // SUBLANES):
        hg = A[g * SUBLANES:(g + 1) * SUBLANES] * h + Bv[g * SUBLANES:(g + 1) * SUBLANES]
        outs.append(hg)
        h = hg[SUBLANES - 1:SUBLANES]
    hseq = jnp.concatenate(outs, axis=0)
    hcar[...] = h
    buf[base - nt:base, :] = new_tail
    y_ref[0] = (hseq * jax.nn.gelu(gb_ref[0])).astype(y_ref.dtype)

    @pl.when(l == nl - 1)
    def _():
        hl_ref[0] = h
        conv_ref[0] = new_tail


def _lru_prompt(proj, tail, h0, cw, cb, wa, ba, wx, bx, lam, W):
    B, L, _ = proj.shape
    nt = tail.shape[1]
    bd = wa.shape[1]
    tw = _pick(W, 512, bd)
    tl = _pick(L, 256)
    nwb = W // tw
    col = lambda b, j, l: (0, j)
    return pl.pallas_call(
        _lru_prompt_kernel,
        name="lru_prompt",
        grid=(B, nwb, L // tl),
        in_specs=[pl.BlockSpec((1, tl, tw), lambda b, j, l: (b, l, j)),
                  pl.BlockSpec((1, tl, tw), lambda b, j, l: (b, l, j + nwb)),
                  pl.BlockSpec((1, nt, tw), lambda b, j, l: (b, 0, j)),
                  pl.BlockSpec((1, 1, tw), lambda b, j, l: (b, 0, j)),
                  pl.BlockSpec((nt + 1, tw), col),
                  pl.BlockSpec((1, tw), col),
                  pl.BlockSpec((tw // bd, bd, bd), lambda b, j, l: (j, 0, 0)),
                  pl.BlockSpec((1, tw), col),
                  pl.BlockSpec((tw // bd, bd, bd), lambda b, j, l: (j, 0, 0)),
                  pl.BlockSpec((1, tw), col),
                  pl.BlockSpec((1, tw), col)],
        out_specs=[pl.BlockSpec((1, tl, tw), lambda b, j, l: (b, l, j)),
                   pl.BlockSpec((1, 1, tw), lambda b, j, l: (b, 0, j)),
                   pl.BlockSpec((1, nt, tw), lambda b, j, l: (b, 0, j))],
        out_shape=[jax.ShapeDtypeStruct((B, L, W), BF16),
                   jax.ShapeDtypeStruct((B, 1, W), F32),
                   jax.ShapeDtypeStruct((B, nt, W), F32)],
        scratch_shapes=[pltpu.VMEM((tl + SUBLANES, tw), F32), pltpu.VMEM((1, tw), F32)],
        compiler_params=_params("arbitrary", "arbitrary", "arbitrary"),
    )(proj, proj, tail, h0, cw, cb, wa, ba, wx, bx, lam)


def _lru_decode_kernel(xb_ref, gb_ref, tail_ref, h0_ref, cw_ref, cb_ref, wa_ref, ba_ref, wx_ref,
                       bx_ref, lam_ref, y_ref, h_ref):
    nt = tail_ref.shape[0]
    xb = xb_ref[...]
    cw = cw_ref[...]
    xc = cb_ref[...] + xb * cw[nt:nt + 1]
    for j in range(nt):
        xc = xc + tail_ref[j] * cw[j:j + 1]
    a, u = _lru_gates(xc, wa_ref, ba_ref[...], wx_ref, bx_ref[...], lam_ref[...])
    h = a * h0_ref[...] + u
    h_ref[...] = h
    y_ref[...] = (h * jax.nn.gelu(gb_ref[...])).astype(y_ref.dtype)


def _lru_decode(proj, tail, h0, cw, cb, wa, ba, wx, bx, lam, W):
    B = proj.shape[0]
    nt = tail.shape[0]
    bd = wa.shape[1]
    tw = _pick(W, 512, bd)
    nwb = W // tw
    col = lambda j: (0, j)
    return pl.pallas_call(
        _lru_decode_kernel,
        name="lru_decode",
        grid=(nwb,),
        in_specs=[pl.BlockSpec((B, tw), col),
                  pl.BlockSpec((B, tw), lambda j: (0, j + nwb)),
                  pl.BlockSpec((nt, B, tw), lambda j: (0, 0, j)),
                  pl.BlockSpec((B, tw), col),
                  pl.BlockSpec((nt + 1, tw), col),
                  pl.BlockSpec((1, tw), col),
                  pl.BlockSpec((tw // bd, bd, bd), lambda j: (j, 0, 0)),
                  pl.BlockSpec((1, tw), col),
                  pl.BlockSpec((tw // bd, bd, bd), lambda j: (j, 0, 0)),
                  pl.BlockSpec((1, tw), col),
                  pl.BlockSpec((1, tw), col)],
        out_specs=[pl.BlockSpec((B, tw), col), pl.BlockSpec((B, tw), col)],
        out_shape=[jax.ShapeDtypeStruct((B, W), BF16), jax.ShapeDtypeStruct((B, W), F32)],
        compiler_params=_params("arbitrary"),
    )(proj, proj, tail, h0, cw, cb, wa, ba, wx, bx, lam)


def _head_norm_gate(o, gn, r):
    on = o * lax.rsqrt(jnp.mean(o * o, axis=-1, keepdims=True) + RMS_EPS) * gn
    return on * jax.nn.silu(r)


def _gla_chunk(q, k, v, r, g, gn, S, att):
    C, DK = q.shape
    sub = min(GLA_SUB, C)
    q = q * (DK ** -0.5)
    vb = v.astype(BF16)
    rr = lax.broadcasted_iota(jnp.int32, (C, C), 0)
    cc = lax.broadcasted_iota(jnp.int32, (C, C), 1)
    b = jnp.dot((rr >= cc).astype(F32), g, precision=lax.Precision.HIGHEST,
                preferred_element_type=F32)
    b_last = b[C - 1:C]

    inter = jnp.dot((q * jnp.exp(b)).astype(BF16), S[...].astype(BF16), preferred_element_type=F32)

    lane = lax.broadcasted_iota(jnp.int32, (sub, sub), 1)
    srow = lax.broadcasted_iota(jnp.int32, (sub, sub), 0)
    for I in range(C // sub):
        lo = I * sub
        qI, kI, bI = q[lo:lo + sub], k[lo:lo + sub], b[lo:lo + sub]
        acc = jnp.zeros((sub, sub), F32)
        for s in range(sub):
            w = qI * kI[s:s + 1] * jnp.exp(jnp.minimum(bI - bI[s:s + 1], 0.0))
            acc = jnp.where(lane == s, jnp.sum(w, axis=1, keepdims=True), acc)
        att[lo:lo + sub, lo:lo + sub] = jnp.where(srow >= lane, acc, 0.0)
        if lo > 0:
            ref = b[lo - 1:lo]
            qt = (qI * jnp.exp(bI - ref)).astype(BF16)
            kt = (k[0:lo] * jnp.exp(ref - b[0:lo])).astype(BF16)
            att[lo:lo + sub, 0:lo] = lax.dot_general(qt, kt, (((1,), (1,)), ((), ())),
                                                     preferred_element_type=F32)
        if lo + sub < C:
            att[lo:lo + sub, lo + sub:C] = jnp.zeros((sub, C - lo - sub), F32)
    o = inter + jnp.dot(att[...].astype(BF16), vb, preferred_element_type=F32)

    kh = (k * jnp.exp(b_last - b)).astype(BF16)
    upd = lax.dot_general(kh, vb, (((0,), (0,)), ((), ())), preferred_element_type=F32)
    S[...] = _row_to_col(jnp.exp(b_last), DK) * S[...] + upd
    return _head_norm_gate(o, gn, r)


def _gla_prompt_kernel(q_ref, k_ref, v_ref, r_ref, la_ref, s0_ref, gn_ref, y_ref, so_ref, S, att):
    c = pl.program_id(2)
    nc = pl.num_programs(2)
    hb, DK, DV = S.shape

    @pl.when(c == 0)
    def _():
        S[...] = s0_ref[0]

    for h in range(hb):
        kk = slice(h * DK, (h + 1) * DK)
        vv = slice(h * DV, (h + 1) * DV)
        y = _gla_chunk(q_ref[0, :, kk], k_ref[0, :, kk], v_ref[0, :, vv], r_ref[0, :, vv],
                       la_ref[0, :, kk], gn_ref[:, vv], S.at[h], att.at[h])
        y_ref[0, :, vv] = y.astype(y_ref.dtype)

    @pl.when(c == nc - 1)
    def _():
        so_ref[0] = S[...]


def _gla_prompt(proj, la, s0, gn, offs):
    B, L, _ = proj.shape
    _, H, DK, DV = s0.shape
    C = _pick(L, GLA_CHUNK)
    hb = 2 if H % 2 == 0 else 1
    wk, wv = hb * DK, hb * DV
    qo, ko, vo, ro = offs
    assert all(o % wk == 0 for o in (qo, ko)) and all(o % wv == 0 for o in (vo, ro))
    return pl.pallas_call(
        _gla_prompt_kernel,
        name="gla_prompt",
        grid=(B, H // hb, L // C),
        in_specs=[pl.BlockSpec((1, C, wk), lambda b, h, c: (b, c, qo // wk + h)),
                  pl.BlockSpec((1, C, wk), lambda b, h, c: (b, c, ko // wk + h)),
                  pl.BlockSpec((1, C, wv), lambda b, h, c: (b, c, vo // wv + h)),
                  pl.BlockSpec((1, C, wv), lambda b, h, c: (b, c, ro // wv + h)),
                  pl.BlockSpec((1, C, wk), lambda b, h, c: (b, c, h)),
                  pl.BlockSpec((1, hb, DK, DV), lambda b, h, c: (b, h, 0, 0)),
                  pl.BlockSpec((1, wv), lambda b, h, c: (0, h))],
        out_specs=[pl.BlockSpec((1, C, wv), lambda b, h, c: (b, c, h)),
                   pl.BlockSpec((1, hb, DK, DV), lambda b, h, c: (b, h, 0, 0))],
        out_shape=[jax.ShapeDtypeStruct((B, L, H * DV), BF16),
                   jax.ShapeDtypeStruct((B, H, DK, DV), F32)],
        scratch_shapes=[pltpu.VMEM((hb, DK, DV), F32), pltpu.VMEM((hb, C, C), F32)],
        compiler_params=_params("arbitrary", "arbitrary", "arbitrary"),
    )(proj, proj, proj, proj, la, s0, gn)


def _gla_decode_kernel(q_ref, k_ref, v_ref, r_ref, la_ref, s0_ref, gn_ref, y_ref, so_ref):
    bb, H, _, DK = q_ref.shape
    for b in range(bb):
        for h in range(H):
            q = q_ref[b, h] * (DK ** -0.5)
            S = (_row_to_col(jnp.exp(la_ref[b, h]), DK) * s0_ref[b, h]
                 + _row_to_col(k_ref[b, h], DK) * v_ref[b, h])
            so_ref[b, h] = S
            o = jnp.sum(_row_to_col(q, DK) * S, axis=0, keepdims=True)
            y_ref[b, h] = _head_norm_gate(o, gn_ref[h], r_ref[b, h]).astype(y_ref.dtype)


def _gla_decode(q, k, v, r, la, s0, gn):
    B, H, DK, DV = s0.shape
    bb = 2 if B % 2 == 0 else 1
    vec = lambda d: pl.BlockSpec((bb, H, 1, d), lambda b: (b, 0, 0, 0))
    mat = pl.BlockSpec((bb, H, DK, DV), lambda b: (b, 0, 0, 0))
    return pl.pallas_call(
        _gla_decode_kernel,
        name="gla_decode",
        grid=(B // bb,),
        in_specs=[vec(DK), vec(DK), vec(DV), vec(DV), vec(DK), mat,
                  pl.BlockSpec((H, 1, DV), lambda b: (0, 0, 0))],
        out_specs=[vec(DV), mat],
        out_shape=[jax.ShapeDtypeStruct((B, H, 1, DV), BF16),
                   jax.ShapeDtypeStruct((B, H, DK, DV), F32)],
        compiler_params=_params("arbitrary"),
    )(q, k, v, r, la, s0, gn)


def _outproj_kernel(ya_ref, yb_ref, wa_ref, wb_ref, x_ref, gt_ref, o_ref, wab, wbb):
    @pl.when((pl.program_id(1) == 0) & (pl.program_id(2) == 0))
    def _():
        wab[...] = wa_ref[...].astype(BF16)
        wbb[...] = wb_ref[...].astype(BF16)

    mix = (jnp.dot(ya_ref[0], wab[...], preferred_element_type=F32)
           + jnp.dot(yb_ref[0], wbb[...], preferred_element_type=F32))
    o_ref[0] = x_ref[0] + gt_ref[0] * mix


def _outproj(ya, yb, w, x, gt):
    B, L, D = x.shape
    Ka, Kb = ya.shape[2], yb.shape[2]
    assert Ka == Kb
    tm = _pick(L, 1024)
    tn = _pick(D, 512, LANES)
    gspec =(pl.BlockSpec((1, 1, tn), lambda j, b, l: (b, 0, j)) if gt.shape[1] == 1
             else pl.BlockSpec((1, tm, tn), lambda j, b, l: (b, l, j)))
    return pl.pallas_call(
        _outproj_kernel,
        name="outproj",
        grid=(D // tn, B, L // tm),
        in_specs=[pl.BlockSpec((1, tm, Ka), lambda j, b, l: (b, l, 0)),
                  pl.BlockSpec((1, tm, Kb), lambda j, b, l: (b, l, 0)),
                  pl.BlockSpec((Ka, tn), lambda j, b, l: (0, j)),
                  pl.BlockSpec((Kb, tn), lambda j, b, l: (1, j)),
                  pl.BlockSpec((1, tm, tn), lambda j, b, l: (b, l, j)),
                  gspec],
        out_specs=pl.BlockSpec((1, tm, tn), lambda j, b, l: (b, l, j)),
        out_shape=jax.ShapeDtypeStruct((B, L, D), F32),
        scratch_shapes=[pltpu.VMEM((Ka, tn), BF16), pltpu.VMEM((Kb, tn), BF16)],
        compiler_params=_params("arbitrary", "arbitrary", "arbitrary"),
    )(ya, yb, w, w, x, gt)


def _topk_rows(problems, ids, kk, fill):
    ss = [p[0] for p in problems]
    for j in range(kk):
        for n, (_, val_ref, idx_ref) in enumerate(problems):
            s = ss[n]
            m = jnp.max(s, axis=0, keepdims=True)
            idx = jnp.min(jnp.where(s == m, ids, fill), axis=0, keepdims=True)
            val_ref[j:j + 1, :] = m
            idx_ref[j:j + 1, :] = idx
            ss[n] = jnp.where(ids == idx, -jnp.inf, s)


def _pair_layout(K):
    segs, ids, r0 = [], [], 0
    b = 0
    while b < K and K // (b + 1) > 1:
        na = K // (b + 1)
        segs.append((r0, (0, na), (b, b + 1)))
        rows = -(-na // SUBLANES) * SUBLANES
        ids += [a * K + b for a in range(na)] + [K * K] * (rows - na)
        r0 += rows
        b += 1
    if b < K:
        nb = K - b
        segs.append((r0, (0, 1), (b, K)))
        rows = -(-nb // SUBLANES) * SUBLANES
        ids += list(range(b, K)) + [K * K] * (rows - nb)
        r0 += rows
    return segs, ids, r0


def _peer_topk_kernel(q_ref, keys_ref, pid_ref, e1_ref, e2_ref, g_ref, sv, si, cand, sc, pos,
                      *, segs):
    tT = q_ref.shape[0]
    H, _, NK, dq = keys_ref.shape
    K = PEER_TOPK
    hp = sv.shape[0]
    ids = lax.broadcasted_iota(jnp.int32, (NK, tT), 0).astype(F32)
    pids = pid_ref[...]

    def heads(i, carry):
        halves = []
        for n in range(hp):
            for p in range(2):
                h = i * hp + n
                off = pl.multiple_of((2 * h + p) * dq, dq)
                qp = q_ref[:, pl.ds(off, dq)].astype(BF16)
                s = lax.dot_general(keys_ref[h, p].astype(BF16), qp, (((1,), (1,)), ((), ())),
                                    preferred_element_type=F32)
                halves.append((s, sv.at[n, p], si.at[n, p]))
        _topk_rows(halves, ids, K, float(NK))
        pairs = []
        for n in range(hp):
            v0, v1 = sv[n, 0], sv[n, 1]
            cand[n] = jnp.full(cand.shape[1:], -jnp.inf, F32)
            for r0, (a0, a1), (b0, b1) in segs:
                rows = max(a1 - a0, b1 - b0)
                cand[n, r0:r0 + rows, :] = v0[a0:a1] + v1[b0:b1]
            pairs.append((cand[n], sc.at[n], pos.at[n]))
        _topk_rows(pairs, pids, K, float(K * K))
        for n in range(hp):
            h = i * hp + n
            pa = jnp.floor(pos[n] * (1.0 / K))
            pb = pos[n] - pa * K
            i0, i1 = si[n, 0], si[n, 1]
            e1 = jnp.zeros((K, tT), F32)
            e2 = jnp.zeros((K, tT), F32)
            for a in range(K):
                e1 = jnp.where(pa == a, i0[a:a + 1], e1)
                e2 = jnp.where(pb == a, i1[a:a + 1], e2)
            scv = sc[n]
            ex = jnp.exp(scv - jnp.max(scv, axis=0, keepdims=True))
            e1_ref[h] = e1
            e2_ref[h] = e2
            g_ref[h] = ex / jnp.sum(ex, axis=0, keepdims=True)
        return carry

    lax.fori_loop(0, H // hp, heads, 0)


def _peer_topk(q, keys):
    T = q.shape[0]
    H, _, NK, dq = keys.shape
    K = PEER_TOPK
    tT = _pick(T, LANES, LANES)
    hp = 4 if H % 4 == 0 else 1
    segs, ids, nr = _pair_layout(K)
    pids = jnp.broadcast_to(jnp.asarray(ids, F32)[:, None], (nr, tT))
    out = jax.ShapeDtypeStruct((H, K, T), F32)
    ospec = pl.BlockSpec((H, K, tT), lambda i: (0, 0, i))
    return pl.pallas_call(
        functools.partial(_peer_topk_kernel, segs=segs),
        name="peer_topk",
        grid=(T // tT,),
        in_specs=[pl.BlockSpec((tT, 2 * H * dq), lambda i: (i, 0)),
                  pl.BlockSpec((H, 2, NK, dq), lambda i: (0, 0, 0, 0)),
                  pl.BlockSpec((nr, tT), lambda i: (0, 0))],
        out_specs=[ospec, ospec, ospec],
        out_shape=[out, out, out],
        scratch_shapes=[pltpu.VMEM((hp, 2, K, tT), F32), pltpu.VMEM((hp, 2, K, tT), F32),
                        pltpu.VMEM((hp, nr, tT), F32), pltpu.VMEM((hp, K, tT), F32),
                        pltpu.VMEM((hp, K, tT), F32)],
        compiler_params=_params("arbitrary"),
    )(q, keys, pids)


def _peer_gate_kernel(e1_ref, e2_ref, g_ref, o_ref, e1t, e2t, gt, gbuf, *, unroll, pitch):
    NS, tT = e1_ref.shape
    NK = gbuf.shape[1]
    e1t[...] = e1_ref[...].T
    e2t[...] = e2_ref[...].T
    gt[...] = g_ref[...].T
    kid = lax.broadcasted_iota(jnp.int32, (NK, NS), 0).astype(F32)

    def body(i, carry):
        for j in range(unroll):
            t = i * unroll + j
            a = jnp.where(kid == e1t[pl.ds(t, 1), :], gt[pl.ds(t, 1), :], 0.0).astype(BF16)
            b = jnp.where(kid == e2t[pl.ds(t, 1), :], 1.0, 0.0).astype(BF16)
            gbuf[pl.ds(pl.multiple_of(t * pitch, SUBLANES), NK), :] = lax.dot_general(
                a, b, (((1,), (1,)), ((), ())), preferred_element_type=F32)
        return carry

    lax.fori_loop(0, tT // unroll, body, 0)
    for i1 in range(NK):
        o_ref[:, i1 * NK:(i1 + 1) * NK] = gbuf[pl.ds(i1, tT, stride=pitch), :].astype(o_ref.dtype)


def _peer_gates(e1, e2, g, NK):
    NS, T = e1.shape
    tT = _pick(T, 128, LANES)
    unroll = 16 if tT % 16 == 0 else 1
    pitch = NK + SUBLANES
    ispec = pl.BlockSpec((NS, tT), lambda i: (0, i))
    return pl.pallas_call(
        functools.partial(_peer_gate_kernel, unroll=unroll, pitch=pitch),
        name="peer_gate",
        grid=(T // tT,),
        in_specs=[ispec, ispec, ispec],
        out_specs=pl.BlockSpec((tT, NK * NK), lambda i: (i, 0)),
        out_shape=jax.ShapeDtypeStruct((T, NK * NK), BF16),
        scratch_shapes=[pltpu.VMEM((tT, NS), F32)] * 3 + [pltpu.VMEM((tT * pitch, NK), F32)],
        compiler_params=_params("arbitrary"),
    )(e1, e2, g)


def _peer_dense_kernel(x_ref, u_ref, v_ref, g_ref, o_ref, *, nsplit):
    e = pl.program_id(1)

    @pl.when(e == 0)
    def _():
        o_ref[...] = jnp.zeros(o_ref.shape, o_ref.dtype)

    rs = x_ref.shape[0] // nsplit
    for r in range(nsplit):
        rows = slice(r * rs, (r + 1) * rs)
        act = lax.dot_general(x_ref[rows, :], u_ref[...], (((1,), (1,)), ((), ())),
                              preferred_element_type=F32)
        p = (g_ref[rows, :].astype(F32) * jax.nn.gelu(act)).astype(BF16)
        o_ref[rows, :] += jnp.dot(p, v_ref[...], preferred_element_type=F32)


def _peer_dense(x, u, v, G):
    T, D = x.shape
    E = u.shape[0]
    tT = _pick(T, 640, 16)
    tE = _pick(E, 512, LANES)
    nsplit = 2 if tT % 32 == 0 else 1
    return pl.pallas_call(
        functools.partial(_peer_dense_kernel, nsplit=nsplit),
        name="peer_dense",
        grid=(T // tT, E // tE),
        in_specs=[pl.BlockSpec((tT, D), lambda i, e: (i, 0)),
                  pl.BlockSpec((tE, D), lambda i, e: (e, 0)),
                  pl.BlockSpec((tE, D), lambda i, e: (e, 0)),
                  pl.BlockSpec((tT, tE), lambda i, e: (i, e))],
        out_specs=pl.BlockSpec((tT, D), lambda i, e: (i, 0)),
        out_shape=jax.ShapeDtypeStruct((T, D), F32),
        compiler_params=_params("arbitrary", "arbitrary"),
    )(x, u, v, G)


def _resid_kernel(x_ref, y_ref, gt_ref, o_ref):
    o_ref[0] = x_ref[0] + gt_ref[0] * y_ref[...]


def _resid_norm_kernel(x_ref, y_ref, gt_ref, g_ref, o_ref):
    x = x_ref[0] + gt_ref[0] * y_ref[...]
    o_ref[0] = x * lax.rsqrt(jnp.mean(x * x, axis=-1, keepdims=True) + RMS_EPS) * g_ref[...]


def _resid(x, y, row0, gt, g=None):
    B, L, D = x.shape
    tl = _pick(L, 256)
    assert row0 % tl == 0
    lb = L // tl
    blk = pl.BlockSpec((1, tl, D), lambda b, l: (b, l, 0))
    in_specs = [blk, pl.BlockSpec((tl, D), lambda b, l: (row0 // tl + b * lb + l, 0)),
                _mod_spec(gt, tl, D)]
    args = [x, y, gt]
    if g is not None:
        in_specs.append(pl.BlockSpec((1, D), lambda b, l: (0, 0)))
        args.append(g)
    return pl.pallas_call(
        _resid_kernel if g is None else _resid_norm_kernel,
        name="resid",
        grid=(B, L // tl),
        in_specs=in_specs,
        out_specs=blk,
        out_shape=jax.ShapeDtypeStruct((B, L, D), F32),
        compiler_params=_params("arbitrary", "arbitrary"),
    )(*args)


def kernel(x_prompt, x_sample, state_conv, state_lru_h, state_gla, c_prompt, c_sample, w_ada, b_ada, g_mix, w_in, conv_w, conv_b, lru_w_a, lru_b_a, lru_w_x, lru_b_x, lru_lambda, gla_w_alpha, gla_b_alpha, gla_g_norm, w_out, g_ffn, peer_w_q, peer_sub_keys, peer_u, peer_v, g_final):
    depth = w_ada.shape[0]
    Bp, L, D = x_prompt.shape
    Bs = x_sample.shape[0]
    assert x_sample.shape[1] == 1
    W = conv_w.shape[2]
    nt = conv_w.shape[1] - 1
    _, _, H, DK, DV = state_gla.shape
    rank = gla_w_alpha.shape[1]
    PH, _, NK, _ = peer_sub_keys.shape[1:]
    n_main = 2 * W + 2 * H * DK + 2 * H * DV
    assert w_in.shape[2] == n_main + rank and rank <= LANES
    offs = (2 * W, 2 * W + H * DK, 2 * W + 2 * H * DK, 2 * W + 2 * H * DK + H * DV)
    Tp = Bp * L

    R = Bp + Bs
    Rp = -(-R // SUBLANES) * SUBLANES
    c_all = jnp.pad(jnp.concatenate([c_prompt, c_sample], axis=0), ((0, Rp - R), (0, 0)))

    x_p = x_prompt
    x_s = x_sample.reshape(1, Bs, D)
    outs = [[] for _ in range(6)]
    for l in range(depth):
        ada = _ada(c_all, w_ada[l], b_ada[l][None])
        mods_p = [ada[:Bp, i * D:(i + 1) * D][:, None, :] for i in range(6)]
        mods_s = [ada[Bp:R, i * D:(i + 1) * D][None] for i in range(6)]

        w_in_t = jnp.transpose(w_in[l])
        in_proj = functools.partial(_matmul, w=w_in_t, N=n_main, w_rows_are_outputs=True,
                                    vmem_limit=IN_PROJ_VMEM_LIMIT)
        gate = functools.partial(_gla_gate, w_t=w_in_t, row0=n_main, w2=gla_w_alpha[l],
                                 b2=gla_b_alpha[l][None])
        wa = lru_w_a[l].astype(BF16)
        wx = lru_w_x[l].astype(BF16)
        lru_vecs = (conv_w[l], conv_b[l][None], wa, lru_b_a[l][None], wx, lru_b_x[l][None],
                    lru_lambda[l][None])
        gn = gla_g_norm[l]

        sh1, sc1, gt1, sh2, sc2, gt2 = mods_p
        hn = _norm_mod(x_p, g_mix[l][None], sc1, sh1, BF16).reshape(Tp, D)
        proj = in_proj(hn).reshape(Bp, L, n_main)
        la = gate(hn).reshape(Bp, L, H * DK)
        y_lru, h_p, conv_p = _lru_prompt(
            proj, jnp.zeros((Bp, nt, W), F32), jnp.zeros((Bp, 1, W), F32), *lru_vecs, W)
        y_gla, s_p = _gla_prompt(proj, la, jnp.zeros((Bp, H, DK, DV), F32), gn[None], offs)
        x1_p = _outproj(y_lru, y_gla, w_out[l], x_p, gt1)
        sc2_p, sh2_p, gt2_p = sc2, sh2, gt2

        sh1, sc1, gt1, sh2, sc2, gt2 = mods_s
        hn = _norm_mod(x_s, g_mix[l][None], sc1, sh1, BF16).reshape(Bs, D)
        proj = in_proj(hn)
        la = gate(hn)
        tail = jnp.transpose(state_conv[l], (1, 0, 2))
        y_lru, h_s = _lru_decode(proj, tail, state_lru_h[l], *lru_vecs, W)
        conv_s = jnp.concatenate([state_conv[l][:, 1:], proj[:, None, :W]], axis=1)
        qo, ko, vo, ro = offs
        heads = lambda a, d: a.reshape(Bs, H, 1, d)
        y_gla, s_s = _gla_decode(
            heads(proj[:, qo:ko], DK), heads(proj[:, ko:vo], DK), heads(proj[:, vo:ro], DV),
            heads(proj[:, ro:n_main], DV), heads(la, DK), state_gla[l], gn.reshape(H, 1, DV))
        x1_s = _outproj(y_lru[None], y_gla.reshape(1, Bs, H * DV), w_out[l], x_s, gt1)
        gt2_s = gt2

        hn2 = _norm_mod_cat(x1_p, x1_s, g_ffn[l][None], sc2_p, sh2_p, sc2, sh2, BF16)
        T = Tp + Bs
        q = _matmul(hn2, peer_w_q[l])
        e1, e2, gate = _peer_topk(q, peer_sub_keys[l])
        G = _peer_gates(e1.reshape(PH * PEER_TOPK, T), e2.reshape(PH * PEER_TOPK, T),
                        gate.reshape(PH * PEER_TOPK, T), NK)
        y = _peer_dense(hn2, peer_u[l].astype(BF16), peer_v[l].astype(BF16), G)
        last = l == depth - 1
        x_p = _resid(x1_p, y, 0, gt2_p, g_final[None] if last else None)
        x_s = _resid(x1_s, y, Tp, gt2_s, g_final[None] if last else None)

        for lst, val in zip(outs, (conv_p, h_p.reshape(Bp, W), s_p, conv_s, h_s, s_s)):
            lst.append(val)

    return (x_p, x_s.reshape(Bs, 1, D)) + tuple(jnp.stack(o) for o in outs)
```

```python
import functools

import jax
import jax.numpy as jnp
from jax import lax
from jax.experimental import pallas as pl
from jax.experimental.pallas import tpu as pltpu

F32 = jnp.float32
BF16 = jnp.bfloat16

RMS_EPS = 1e-6
RG_C = 8.0
GLA_TAU = 16.0
PEER_TOPK = 16
GLA_CHUNK = 128
GLA_SUB = 16
LANES = 128
SUBLANES = 8
VMEM_LIMIT = 56 * 1024 * 1024
IN_PROJ_VMEM_LIMIT = 60 * 1024 * 1024


def _params(*sem):
    return pltpu.CompilerParams(dimension_semantics=sem, vmem_limit_bytes=VMEM_LIMIT)


def _pick(n, pref, mult=SUBLANES):
    best = None
    for d in range(mult, min(n, pref) + 1, mult):
        if n % d == 0:
            best = d
    return n if best is None else best


def _softplus(y):
    return jnp.maximum(y, 0.0) + jnp.log1p(jnp.exp(-jnp.abs(y)))


def _log_sigmoid(z):
    return jnp.minimum(z, 0.0) - jnp.log1p(jnp.exp(-jnp.abs(z)))


def _row_to_col(row, n):
    r = lax.broadcasted_iota(jnp.int32, (n, n), 0)
    c = lax.broadcasted_iota(jnp.int32, (n, n), 1)
    return jnp.sum(jnp.where(r == c, jnp.broadcast_to(row, (n, n)), 0.0), axis=1, keepdims=True)


def _ada_kernel(c_ref, w_ref, b_ref, o_ref):
    a = jax.nn.silu(c_ref[...]).astype(BF16)
    o_ref[...] = jnp.dot(a, w_ref[...].astype(BF16), preferred_element_type=F32) + b_ref[...]


def _ada(c, w, b):
    R, D = c.shape
    N = w.shape[1]
    tn = _pick(N, 512, LANES)
    return pl.pallas_call(
        _ada_kernel,
        name="ada",
        grid=(N // tn,),
        in_specs=[pl.BlockSpec((R, D), lambda j: (0, 0)),
                  pl.BlockSpec((D, tn), lambda j: (0, j)),
                  pl.BlockSpec((1, tn), lambda j: (0, j))],
        out_specs=pl.BlockSpec((R, tn), lambda j: (0, j)),
        out_shape=jax.ShapeDtypeStruct((R, N), F32),
        compiler_params=_params("arbitrary"),
    )(c, w, b)


def _norm_mod_kernel(x_ref, g_ref, sc_ref, sh_ref, o_ref):
    x = x_ref[0]
    y = x * lax.rsqrt(jnp.mean(x * x, axis=-1, keepdims=True) + RMS_EPS) * g_ref[...]
    o_ref[0] = (y * (1.0 + sc_ref[0]) + sh_ref[0]).astype(o_ref.dtype)


def _mod_spec(m, tl, D):
    if m.shape[1] == 1:
        return pl.BlockSpec((1, 1, D), lambda b, l: (b, 0, 0))
    return pl.BlockSpec((1, tl, D), lambda b, l: (b, l, 0))


def _norm_mod(x, g, sc, sh, out_dtype):
    B, L, D = x.shape
    tl = _pick(L, 256)
    return pl.pallas_call(
        _norm_mod_kernel,
        name="norm_mod",
        grid=(B, L // tl),
        in_specs=[pl.BlockSpec((1, tl, D), lambda b, l: (b, l, 0)),
                  pl.BlockSpec((1, D), lambda b, l: (0, 0)),
                  _mod_spec(sc, tl, D), _mod_spec(sh, tl, D)],
        out_specs=pl.BlockSpec((1, tl, D), lambda b, l: (b, l, 0)),
        out_shape=jax.ShapeDtypeStruct((B, L, D), out_dtype),
        compiler_params=_params("arbitrary", "arbitrary"),
    )(x, g, sc, sh)


def _norm_mod_cat_kernel(xp_ref, xs_ref, g_ref, scp_ref, shp_ref, scs_ref, shs_ref, o_ref, *, n_p):
    i = pl.program_id(0)

    def f(x, sc, sh):
        y = x * lax.rsqrt(jnp.mean(x * x, axis=-1, keepdims=True) + RMS_EPS) * g_ref[...]
        return (y * (1.0 + sc) + sh).astype(o_ref.dtype)

    @pl.when(i < n_p)
    def _():
        o_ref[...] = f(xp_ref[0], scp_ref[0], shp_ref[0])

    @pl.when(i >= n_p)
    def _():
        o_ref[...] = f(xs_ref[0], scs_ref[0], shs_ref[0])


def _norm_mod_cat(xp, xs, g, scp, shp, scs, shs, out_dtype):
    Bp, L, D = xp.shape
    Bs = xs.shape[1]
    tl = _pick(Bs, 256)
    assert L % tl == 0
    lb = L // tl
    n_p = Bp * lb
    n_s = Bs // tl
    pb = lambda i: jnp.minimum(i, n_p - 1)
    sb = lambda i: jnp.maximum(i - n_p, 0)
    pmod = pl.BlockSpec((1, 1, D), lambda i: (pb(i) // lb, 0, 0))
    smod = pl.BlockSpec((1, tl, D), lambda i: (0, sb(i), 0))
    return pl.pallas_call(
        functools.partial(_norm_mod_cat_kernel, n_p=n_p),
        name="norm_mod_cat",
        grid=(n_p + n_s,),
        in_specs=[pl.BlockSpec((1, tl, D), lambda i: (pb(i) // lb, pb(i) % lb, 0)),
                  pl.BlockSpec((1, tl, D), lambda i: (0, sb(i), 0)),
                  pl.BlockSpec((1, D), lambda i: (0, 0)),
                  pmod, pmod, smod, smod],
        out_specs=pl.BlockSpec((tl, D), lambda i: (i, 0)),
        out_shape=jax.ShapeDtypeStruct((Bp * L + Bs, D), out_dtype),
        compiler_params=_params("arbitrary"),
    )(xp, xs, g, scp, shp, scs, shs)


def _mm_kernel(x_ref, w_ref, o_ref, wb, *, w_rows_are_outputs):
    @pl.when(pl.program_id(1) == 0)
    def _():
        wb[...] = w_ref[...].astype(BF16)

    dims = (((1,), (1,)), ((), ())) if w_rows_are_outputs else (((1,), (0,)), ((), ()))
    o_ref[...] = lax.dot_general(x_ref[...], wb[...], dims, preferred_element_type=F32)


def _matmul(x, w, N=None, *, w_rows_are_outputs=False, tm_pref=512, tn_pref=1024,
            vmem_limit=VMEM_LIMIT):
    M, K = x.shape
    n_all = w.shape[0] if w_rows_are_outputs else w.shape[1]
    N = n_all if N is None else N
    tm = _pick(M, tm_pref, 16)
    tn = _pick(N, tn_pref, LANES)
    wblk = (tn, K) if w_rows_are_outputs else (K, tn)
    wmap = (lambda j, i: (j, 0)) if w_rows_are_outputs else (lambda j, i: (0, j))
    return pl.pallas_call(
        functools.partial(_mm_kernel, w_rows_are_outputs=w_rows_are_outputs),
        name="matmul",
        grid=(N // tn, M // tm),
        in_specs=[pl.BlockSpec((tm, K), lambda j, i: (i, 0)), pl.BlockSpec(wblk, wmap)],
        out_specs=pl.BlockSpec((tm, tn), lambda j, i: (i, j)),
        out_shape=jax.ShapeDtypeStruct((M, N), F32),
        scratch_shapes=[pltpu.VMEM(wblk, BF16)],
        compiler_params=pltpu.CompilerParams(dimension_semantics=("arbitrary", "arbitrary"),
                                             vmem_limit_bytes=vmem_limit),
    )(x, w)


def _gate_kernel(x_ref, w1_ref, w2_ref, b2_ref, o_ref):
    a = lax.dot_general(x_ref[...], w1_ref[...].astype(BF16), (((1,), (1,)), ((), ())),
                        preferred_element_type=F32)
    z = jnp.dot(a.astype(BF16), w2_ref[...].astype(BF16), preferred_element_type=F32) + b2_ref[...]
    o_ref[...] = _log_sigmoid(z) / GLA_TAU


def _gla_gate(hn, w_t, row0, w2, b2):
    M, K = hn.shape
    R, N = w2.shape
    assert row0 % R == 0 and R % SUBLANES == 0
    tm = _pick(M, 512)
    return pl.pallas_call(
        _gate_kernel,
        name="gla_gate",
        grid=(M // tm,),
        in_specs=[pl.BlockSpec((tm, K), lambda i: (i, 0)),
                  pl.BlockSpec((R, K), lambda i: (row0 // R, 0)),
                  pl.BlockSpec((R, N), lambda i: (0, 0)),
                  pl.BlockSpec((1, N), lambda i: (0, 0))],
        out_specs=pl.BlockSpec((tm, N), lambda i: (i, 0)),
        out_shape=jax.ShapeDtypeStruct((M, N), F32),
        compiler_params=_params("arbitrary"),
    )(hn, w_t, w2, b2)


def _lru_gates(xc, wa_ref, ba, wx_ref, bx, lam):
    nb, bd = wa_ref.shape[0], wa_ref.shape[1]
    xcb = xc.astype(BF16)
    ra, ri = [], []
    for n in range(nb):
        xs = xcb[:, n * bd:(n + 1) * bd]
        ra.append(jnp.dot(xs, wa_ref[n], preferred_element_type=F32))
        ri.append(jnp.dot(xs, wx_ref[n], preferred_element_type=F32))
    r = jax.nn.sigmoid(jnp.concatenate(ra, axis=1) + ba)
    i = jax.nn.sigmoid(jnp.concatenate(ri, axis=1) + bx)
    log_a = -RG_C * r * _softplus(-lam)
    a = jnp.exp(log_a)
    u = jnp.sqrt(-jnp.tanh(log_a) * (a * a + 1.0)) * i * xc
    return a, u


def _lru_prompt_kernel(xb_ref, gb_ref, tail_ref, h0_ref, cw_ref, cb_ref, wa_ref, ba_ref, wx_ref,
                       bx_ref, lam_ref, y_ref, hl_ref, conv_ref, buf, hcar):
    l = pl.program_id(2)
    nl = pl.num_programs(2)
    tl, tw = xb_ref.shape[1], xb_ref.shape[2]
    nt = tail_ref.shape[1]
    base = SUBLANES

    @pl.when(l == 0)
    def _():
        buf[base - nt:base, :] = tail_ref[0]
        hcar[...] = h0_ref[0]

    xb = xb_ref[0]
    buf[base:base + tl, :] = xb
    cw = cw_ref[...]
    xc = cb_ref[...] + xb * cw[nt:nt + 1]
    for j in range(nt):
        xc = xc + buf[base - nt + j:base - nt + j + tl, :] * cw[j:j + 1]
    new_tail = buf[base + tl - nt:base + tl, :]

    a, u = _lru_gates(xc, wa_ref, ba_ref[...], wx_ref, bx_ref[...], lam_ref[...])

    row = lax.broadcasted_iota(jnp.int32, (tl, tw), 0) & (SUBLANES - 1)
    A, Bv = a, u
    s = 1
    while s < SUBLANES:
        As = pltpu.roll(A, s, axis=0)
        Bs = pltpu.roll(Bv, s, axis=0)
        m = row >= s
        Bv = jnp.where(m, A * Bs + Bv, Bv)
        A = jnp.where(m, A * As, A)
        s *= 2
    h = hcar[...]
    outs = []
    for g in range(tl // SUBLANES):
        hg = A[g * SUBLANES:(g + 1) * SUBLANES] * h + Bv[g * SUBLANES:(g + 1) * SUBLANES]
        outs.append(hg)
        h = hg[SUBLANES - 1:SUBLANES]
    hseq = jnp.concatenate(outs, axis=0)
    hcar[...] = h
    buf[base - nt:base, :] = new_tail
    y_ref[0] = (hseq * jax.nn.gelu(gb_ref[0])).astype(y_ref.dtype)

    @pl.when(l == nl - 1)
    def _():
        hl_ref[0] = h
        conv_ref[0] = new_tail


def _lru_prompt(proj, tail, h0, cw, cb, wa, ba, wx, bx, lam, W):
    B, L, _ = proj.shape
    nt = tail.shape[1]
    bd = wa.shape[1]
    tw = _pick(W, 512, bd)
    tl = _pick(L, 256)
    nwb = W // tw
    col = lambda b, j, l: (0, j)
    return pl.pallas_call(
        _lru_prompt_kernel,
        name="lru_prompt",
        grid=(B, nwb, L // tl),
        in_specs=[pl.BlockSpec((1, tl, tw), lambda b, j, l: (b, l, j)),
                  pl.BlockSpec((1, tl, tw), lambda b, j, l: (b, l, j + nwb)),
                  pl.BlockSpec((1, nt, tw), lambda b, j, l: (b, 0, j)),
                  pl.BlockSpec((1, 1, tw), lambda b, j, l: (b, 0, j)),
                  pl.BlockSpec((nt + 1, tw), col),
                  pl.BlockSpec((1, tw), col),
                  pl.BlockSpec((tw // bd, bd, bd), lambda b, j, l: (j, 0, 0)),
                  pl.BlockSpec((1, tw), col),
                  pl.BlockSpec((tw // bd, bd, bd), lambda b, j, l: (j, 0, 0)),
                  pl.BlockSpec((1, tw), col),
                  pl.BlockSpec((1, tw), col)],
        out_specs=[pl.BlockSpec((1, tl, tw), lambda b, j, l: (b, l, j)),
                   pl.BlockSpec((1, 1, tw), lambda b, j, l: (b, 0, j)),
                   pl.BlockSpec((1, nt, tw), lambda b, j, l: (b, 0, j))],
        out_shape=[jax.ShapeDtypeStruct((B, L, W), BF16),
                   jax.ShapeDtypeStruct((B, 1, W), F32),
                   jax.ShapeDtypeStruct((B, nt, W), F32)],
        scratch_shapes=[pltpu.VMEM((tl + SUBLANES, tw), F32), pltpu.VMEM((1, tw), F32)],
        compiler_params=_params("arbitrary", "arbitrary", "arbitrary"),
    )(proj, proj, tail, h0, cw, cb, wa, ba, wx, bx, lam)


def _lru_decode_kernel(xb_ref, gb_ref, tail_ref, h0_ref, cw_ref, cb_ref, wa_ref, ba_ref, wx_ref,
                       bx_ref, lam_ref, y_ref, h_ref):
    nt = tail_ref.shape[0]
    xb = xb_ref[...]
    cw = cw_ref[...]
    xc = cb_ref[...] + xb * cw[nt:nt + 1]
    for j in range(nt):
        xc = xc + tail_ref[j] * cw[j:j + 1]
    a, u = _lru_gates(xc, wa_ref, ba_ref[...], wx_ref, bx_ref[...], lam_ref[...])
    h = a * h0_ref[...] + u
    h_ref[...] = h
    y_ref[...] = (h * jax.nn.gelu(gb_ref[...])).astype(y_ref.dtype)


def _lru_decode(proj, tail, h0, cw, cb, wa, ba, wx, bx, lam, W):
    B = proj.shape[0]
    nt = tail.shape[0]
    bd = wa.shape[1]
    tw = _pick(W, 512, bd)
    nwb = W // tw
    col = lambda j: (0, j)
    return pl.pallas_call(
        _lru_decode_kernel,
        name="lru_decode",
        grid=(nwb,),
        in_specs=[pl.BlockSpec((B, tw), col),
                  pl.BlockSpec((B, tw), lambda j: (0, j + nwb)),
                  pl.BlockSpec((nt, B, tw), lambda j: (0, 0, j)),
                  pl.BlockSpec((B, tw), col),
                  pl.BlockSpec((nt + 1, tw), col),
                  pl.BlockSpec((1, tw), col),
                  pl.BlockSpec((tw // bd, bd, bd), lambda j: (j, 0, 0)),
                  pl.BlockSpec((1, tw), col),
                  pl.BlockSpec((tw // bd, bd, bd), lambda j: (j, 0, 0)),
                  pl.BlockSpec((1, tw), col),
                  pl.BlockSpec((1, tw), col)],
        out_specs=[pl.BlockSpec((B, tw), col), pl.BlockSpec((B, tw), col)],
        out_shape=[jax.ShapeDtypeStruct((B, W), BF16), jax.ShapeDtypeStruct((B, W), F32)],
        compiler_params=_params("arbitrary"),
    )(proj, proj, tail, h0, cw, cb, wa, ba, wx, bx, lam)


def _head_norm_gate(o, gn, r):
    on = o * lax.rsqrt(jnp.mean(o * o, axis=-1, keepdims=True) + RMS_EPS) * gn
    return on * jax.nn.silu(r)


def _gla_chunk(q, k, v, r, g, gn, S, att):
    C, DK = q.shape
    sub = min(GLA_SUB, C)
    q = q * (DK ** -0.5)
    vb = v.astype(BF16)
    rr = lax.broadcasted_iota(jnp.int32, (C, C), 0)
    cc = lax.broadcasted_iota(jnp.int32, (C, C), 1)
    b = jnp.dot((rr >= cc).astype(F32), g, precision=lax.Precision.HIGHEST,
                preferred_element_type=F32)
    b_last = b[C - 1:C]

    inter = jnp.dot((q * jnp.exp(b)).astype(BF16), S[...].astype(BF16), preferred_element_type=F32)

    lane = lax.broadcasted_iota(jnp.int32, (sub, sub), 1)
    srow = lax.broadcasted_iota(jnp.int32, (sub, sub), 0)
    for I in range(C // sub):
        lo = I * sub
        qI, kI, bI = q[lo:lo + sub], k[lo:lo + sub], b[lo:lo + sub]
        acc = jnp.zeros((sub, sub), F32)
        for s in range(sub):
            w = qI * kI[s:s + 1] * jnp.exp(jnp.minimum(bI - bI[s:s + 1], 0.0))
            acc = jnp.where(lane == s, jnp.sum(w, axis=1, keepdims=True), acc)
        att[lo:lo + sub, lo:lo + sub] = jnp.where(srow >= lane, acc, 0.0)
        if lo > 0:
            ref = b[lo - 1:lo]
            qt = (qI * jnp.exp(bI - ref)).astype(BF16)
            kt = (k[0:lo] * jnp.exp(ref - b[0:lo])).astype(BF16)
            att[lo:lo + sub, 0:lo] = lax.dot_general(qt, kt, (((1,), (1,)), ((), ())),
                                                     preferred_element_type=F32)
        if lo + sub < C:
            att[lo:lo + sub, lo + sub:C] = jnp.zeros((sub, C - lo - sub), F32)
    o = inter + jnp.dot(att[...].astype(BF16), vb, preferred_element_type=F32)

    kh = (k * jnp.exp(b_last - b)).astype(BF16)
    upd = lax.dot_general(kh, vb, (((0,), (0,)), ((), ())), preferred_element_type=F32)
    S[...] = _row_to_col(jnp.exp(b_last), DK) * S[...] + upd
    return _head_norm_gate(o, gn, r)


def _gla_prompt_kernel(q_ref, k_ref, v_ref, r_ref, la_ref, s0_ref, gn_ref, y_ref, so_ref, S, att):
    c = pl.program_id(2)
    nc = pl.num_programs(2)
    hb, DK, DV = S.shape

    @pl.when(c == 0)
    def _():
        S[...] = s0_ref[0]

    for h in range(hb):
        kk = slice(h * DK, (h + 1) * DK)
        vv = slice(h * DV, (h + 1) * DV)
        y = _gla_chunk(q_ref[0, :, kk], k_ref[0, :, kk], v_ref[0, :, vv], r_ref[0, :, vv],
                       la_ref[0, :, kk], gn_ref[:, vv], S.at[h], att.at[h])
        y_ref[0, :, vv] = y.astype(y_ref.dtype)

    @pl.when(c == nc - 1)
    def _():
        so_ref[0] = S[...]


def _gla_prompt(proj, la, s0, gn, offs):
    B, L, _ = proj.shape
    _, H, DK, DV = s0.shape
    C = _pick(L, GLA_CHUNK)
    hb = 4 if H % 4 == 0 else (2 if H % 2 == 0 else 1)
    wk, wv = hb * DK, hb * DV
    qo, ko, vo, ro = offs
    assert all(o % wk == 0 for o in (qo, ko)) and all(o % wv == 0 for o in (vo, ro))
    return pl.pallas_call(
        _gla_prompt_kernel,
        name="gla_prompt",
        grid=(B, H // hb, L // C),
        in_specs=[pl.BlockSpec((1, C, wk), lambda b, h, c: (b, c, qo // wk + h)),
                  pl.BlockSpec((1, C, wk), lambda b, h, c: (b, c, ko // wk + h)),
                  pl.BlockSpec((1, C, wv), lambda b, h, c: (b, c, vo // wv + h)),
                  pl.BlockSpec((1, C, wv), lambda b, h, c: (b, c, ro // wv + h)),
                  pl.BlockSpec((1, C, wk), lambda b, h, c: (b, c, h)),
                  pl.BlockSpec((1, hb, DK, DV), lambda b, h, c: (b, h, 0, 0)),
                  pl.BlockSpec((1, wv), lambda b, h, c: (0, h))],
        out_specs=[pl.BlockSpec((1, C, wv), lambda b, h, c: (b, c, h)),
                   pl.BlockSpec((1, hb, DK, DV), lambda b, h, c: (b, h, 0, 0))],
        out_shape=[jax.ShapeDtypeStruct((B, L, H * DV), BF16),
                   jax.ShapeDtypeStruct((B, H, DK, DV), F32)],
        scratch_shapes=[pltpu.VMEM((hb, DK, DV), F32), pltpu.VMEM((hb, C, C), F32)],
        compiler_params=_params("arbitrary", "arbitrary", "arbitrary"),
    )(proj, proj, proj, proj, la, s0, gn)


def _gla_decode_kernel(q_ref, k_ref, v_ref, r_ref, la_ref, s0_ref, gn_ref, y_ref, so_ref):
    bb, H, _, DK = q_ref.shape
    for b in range(bb):
        for h in range(H):
            q = q_ref[b, h] * (DK ** -0.5)
            S = (_row_to_col(jnp.exp(la_ref[b, h]), DK) * s0_ref[b, h]
                 + _row_to_col(k_ref[b, h], DK) * v_ref[b, h])
            so_ref[b, h] = S
            o = jnp.sum(_row_to_col(q, DK) * S, axis=0, keepdims=True)
            y_ref[b, h] = _head_norm_gate(o, gn_ref[h], r_ref[b, h]).astype(y_ref.dtype)


def _gla_decode(q, k, v, r, la, s0, gn):
    B, H, DK, DV = s0.shape
    bb = 2 if B % 2 == 0 else 1
    vec = lambda d: pl.BlockSpec((bb, H, 1, d), lambda b: (b, 0, 0, 0))
    mat = pl.BlockSpec((bb, H, DK, DV), lambda b: (b, 0, 0, 0))
    return pl.pallas_call(
        _gla_decode_kernel,
        name="gla_decode",
        grid=(B // bb,),
        in_specs=[vec(DK), vec(DK), vec(DV), vec(DV), vec(DK), mat,
                  pl.BlockSpec((H, 1, DV), lambda b: (0, 0, 0))],
        out_specs=[vec(DV), mat],
        out_shape=[jax.ShapeDtypeStruct((B, H, 1, DV), BF16),
                   jax.ShapeDtypeStruct((B, H, DK, DV), F32)],
        compiler_params=_params("arbitrary"),
    )(q, k, v, r, la, s0, gn)


def _outproj_kernel(ya_ref, yb_ref, wa_ref, wb_ref, x_ref, gt_ref, o_ref, wab, wbb):
    @pl.when((pl.program_id(1) == 0) & (pl.program_id(2) == 0))
    def _():
        wab[...] = wa_ref[...].astype(BF16)
        wbb[...] = wb_ref[...].astype(BF16)

    mix = (jnp.dot(ya_ref[0], wab[...], preferred_element_type=F32)
           + jnp.dot(yb_ref[0], wbb[...], preferred_element_type=F32))
    o_ref[0] = x_ref[0] + gt_ref[0] * mix


def _outproj(ya, yb, w, x, gt):
    B, L, D = x.shape
    Ka, Kb = ya.shape[2], yb.shape[2]
    assert Ka == Kb
    tm = _pick(L, 1024)
    tn = _pick(D, 512, LANES)
    gspec =(pl.BlockSpec((1, 1, tn), lambda j, b, l: (b, 0, j)) if gt.shape[1] == 1
             else pl.BlockSpec((1, tm, tn), lambda j, b, l: (b, l, j)))
    return pl.pallas_call(
        _outproj_kernel,
        name="outproj",
        grid=(D // tn, B, L // tm),
        in_specs=[pl.BlockSpec((1, tm, Ka), lambda j, b, l: (b, l, 0)),
                  pl.BlockSpec((1, tm, Kb), lambda j, b, l: (b, l, 0)),
                  pl.BlockSpec((Ka, tn), lambda j, b, l: (0, j)),
                  pl.BlockSpec((Kb, tn), lambda j, b, l: (1, j)),
                  pl.BlockSpec((1, tm, tn), lambda j, b, l: (b, l, j)),
                  gspec],
        out_specs=pl.BlockSpec((1, tm, tn), lambda j, b, l: (b, l, j)),
        out_shape=jax.ShapeDtypeStruct((B, L, D), F32),
        scratch_shapes=[pltpu.VMEM((Ka, tn), BF16), pltpu.VMEM((Kb, tn), BF16)],
        compiler_params=_params("arbitrary", "arbitrary", "arbitrary"),
    )(ya, yb, w, w, x, gt)


def _topk_rows(problems, ids, kk, fill):
    ss = [p[0] for p in problems]
    for j in range(kk):
        for n, (_, val_ref, idx_ref) in enumerate(problems):
            s = ss[n]
            m = jnp.max(s, axis=0, keepdims=True)
            idx = jnp.min(jnp.where(s == m, ids, fill), axis=0, keepdims=True)
            val_ref[j:j + 1, :] = m
            idx_ref[j:j + 1, :] = idx
            ss[n] = jnp.where(ids == idx, -jnp.inf, s)


def _pair_layout(K):
    segs, ids, r0 = [], [], 0
    b = 0
    while b < K and K // (b + 1) > 1:
        na = K // (b + 1)
        segs.append((r0, (0, na), (b, b + 1)))
        rows = -(-na // SUBLANES) * SUBLANES
        ids += [a * K + b for a in range(na)] + [K * K] * (rows - na)
        r0 += rows
        b += 1
    if b < K:
        nb = K - b
        segs.append((r0, (0, 1), (b, K)))
        rows = -(-nb // SUBLANES) * SUBLANES
        ids += list(range(b, K)) + [K * K] * (rows - nb)
        r0 += rows
    return segs, ids, r0


def _peer_topk_kernel(q_ref, keys_ref, pid_ref, e1_ref, e2_ref, g_ref, sv, si, cand, sc, pos,
                      *, segs):
    tT = q_ref.shape[0]
    H, _, NK, dq = keys_ref.shape
    K = PEER_TOPK
    hp = sv.shape[0]
    ids = lax.broadcasted_iota(jnp.int32, (NK, tT), 0).astype(F32)
    pids = pid_ref[...]

    def heads(i, carry):
        halves = []
        for n in range(hp):
            for p in range(2):
                h = i * hp + n
                off = pl.multiple_of((2 * h + p) * dq, dq)
                qp = q_ref[:, pl.ds(off, dq)].astype(BF16)
                s = lax.dot_general(keys_ref[h, p].astype(BF16), qp, (((1,), (1,)), ((), ())),
                                    preferred_element_type=F32)
                halves.append((s, sv.at[n, p], si.at[n, p]))
        _topk_rows(halves, ids, K, float(NK))
        pairs = []
        for n in range(hp):
            v0, v1 = sv[n, 0], sv[n, 1]
            cand[n] = jnp.full(cand.shape[1:], -jnp.inf, F32)
            for r0, (a0, a1), (b0, b1) in segs:
                rows = max(a1 - a0, b1 - b0)
                cand[n, r0:r0 + rows, :] = v0[a0:a1] + v1[b0:b1]
            pairs.append((cand[n], sc.at[n], pos.at[n]))
        _topk_rows(pairs, pids, K, float(K * K))
        for n in range(hp):
            h = i * hp + n
            pa = jnp.floor(pos[n] * (1.0 / K))
            pb = pos[n] - pa * K
            i0, i1 = si[n, 0], si[n, 1]
            e1 = jnp.zeros((K, tT), F32)
            e2 = jnp.zeros((K, tT), F32)
            for a in range(K):
                e1 = jnp.where(pa == a, i0[a:a + 1], e1)
                e2 = jnp.where(pb == a, i1[a:a + 1], e2)
            scv = sc[n]
            ex = jnp.exp(scv - jnp.max(scv, axis=0, keepdims=True))
            e1_ref[h] = e1
            e2_ref[h] = e2
            g_ref[h] = ex / jnp.sum(ex, axis=0, keepdims=True)
        return carry

    lax.fori_loop(0, H // hp, heads, 0)


def _peer_topk(q, keys):
    T = q.shape[0]
    H, _, NK, dq = keys.shape
    K = PEER_TOPK
    tT = _pick(T, LANES, LANES)
    hp = 4 if H % 4 == 0 else 1
    segs, ids, nr = _pair_layout(K)
    pids = jnp.broadcast_to(jnp.asarray(ids, F32)[:, None], (nr, tT))
    out = jax.ShapeDtypeStruct((H, K, T), F32)
    ospec = pl.BlockSpec((H, K, tT), lambda i: (0, 0, i))
    return pl.pallas_call(
        functools.partial(_peer_topk_kernel, segs=segs),
        name="peer_topk",
        grid=(T // tT,),
        in_specs=[pl.BlockSpec((tT, 2 * H * dq), lambda i: (i, 0)),
                  pl.BlockSpec((H, 2, NK, dq), lambda i: (0, 0, 0, 0)),
                  pl.BlockSpec((nr, tT), lambda i: (0, 0))],
        out_specs=[ospec, ospec, ospec],
        out_shape=[out, out, out],
        scratch_shapes=[pltpu.VMEM((hp, 2, K, tT), F32), pltpu.VMEM((hp, 2, K, tT), F32),
                        pltpu.VMEM((hp, nr, tT), F32), pltpu.VMEM((hp, K, tT), F32),
                        pltpu.VMEM((hp, K, tT), F32)],
        compiler_params=_params("arbitrary"),
    )(q, keys, pids)


def _peer_gate_kernel(e1_ref, e2_ref, g_ref, o_ref, e1t, e2t, gt, gbuf, *, unroll, pitch):
    NS, tT = e1_ref.shape
    NK = gbuf.shape[1]
    e1t[...] = e1_ref[...].T
    e2t[...] = e2_ref[...].T
    gt[...] = g_ref[...].T
    kid = lax.broadcasted_iota(jnp.int32, (NK, NS), 0).astype(F32)

    def body(i, carry):
        for j in range(unroll):
            t = i * unroll + j
            a = jnp.where(kid == e1t[pl.ds(t, 1), :], gt[pl.ds(t, 1), :], 0.0).astype(BF16)
            b = jnp.where(kid == e2t[pl.ds(t, 1), :], 1.0, 0.0).astype(BF16)
            gbuf[pl.ds(pl.multiple_of(t * pitch, SUBLANES), NK), :] = lax.dot_general(
                a, b, (((1,), (1,)), ((), ())), preferred_element_type=F32)
        return carry

    lax.fori_loop(0, tT // unroll, body, 0)
    for i1 in range(NK):
        o_ref[:, i1 * NK:(i1 + 1) * NK] = gbuf[pl.ds(i1, tT, stride=pitch), :].astype(o_ref.dtype)


def _peer_gates(e1, e2, g, NK):
    NS, T = e1.shape
    tT = _pick(T, 128, LANES)
    unroll = 32 if tT % 32 == 0 else 1
    pitch = NK + SUBLANES
    ispec = pl.BlockSpec((NS, tT), lambda i: (0, i))
    return pl.pallas_call(
        functools.partial(_peer_gate_kernel, unroll=unroll, pitch=pitch),
        name="peer_gate",
        grid=(T // tT,),
        in_specs=[ispec, ispec, ispec],
        out_specs=pl.BlockSpec((tT, NK * NK), lambda i: (i, 0)),
        out_shape=jax.ShapeDtypeStruct((T, NK * NK), BF16),
        scratch_shapes=[pltpu.VMEM((tT, NS), F32)] * 3 + [pltpu.VMEM((tT * pitch, NK), F32)],
        compiler_params=_params("arbitrary"),
    )(e1, e2, g)


def _peer_dense_kernel(x_ref, u_ref, v_ref, g_ref, o_ref, *, nsplit):
    e = pl.program_id(1)

    @pl.when(e == 0)
    def _():
        o_ref[...] = jnp.zeros(o_ref.shape, o_ref.dtype)

    rs = x_ref.shape[0] // nsplit
    for r in range(nsplit):
        rows = slice(r * rs, (r + 1) * rs)
        act = lax.dot_general(x_ref[rows, :], u_ref[...], (((1,), (1,)), ((), ())),
                              preferred_element_type=F32)
        p = (g_ref[rows, :].astype(F32) * jax.nn.gelu(act)).astype(BF16)
        o_ref[rows, :] += jnp.dot(p, v_ref[...], preferred_element_type=F32)


def _peer_dense(x, u, v, G):
    T, D = x.shape
    E = u.shape[0]
    tT = _pick(T, 640, 16)
    tE = _pick(E, 512, LANES)
    nsplit = 2 if tT % 32 == 0 else 1
    return pl.pallas_call(
        functools.partial(_peer_dense_kernel, nsplit=nsplit),
        name="peer_dense",
        grid=(T // tT, E // tE),
        in_specs=[pl.BlockSpec((tT, D), lambda i, e: (i, 0)),
                  pl.BlockSpec((tE, D), lambda i, e: (e, 0)),
                  pl.BlockSpec((tE, D), lambda i, e: (e, 0)),
                  pl.BlockSpec((tT, tE), lambda i, e: (i, e))],
        out_specs=pl.BlockSpec((tT, D), lambda i, e: (i, 0)),
        out_shape=jax.ShapeDtypeStruct((T, D), F32),
        compiler_params=_params("arbitrary", "arbitrary"),
    )(x, u, v, G)


def _row_groups(rows):
    for n in (5, 4, 2):
        if rows % (16 * n) == 0:
            return n
    return 1


def _peer_act_kernel(x_ref, u_ref, g_ref, p_ref, *, nsplit):
    ub = u_ref[...].astype(BF16)
    rs = x_ref.shape[0] // nsplit
    for r in range(nsplit):
        rows = slice(r * rs, (r + 1) * rs)
        act = lax.dot_general(x_ref[rows, :], ub, (((1,), (1,)), ((), ())),
                              preferred_element_type=F32)
        p_ref[rows, :] = (g_ref[rows, :].astype(F32) * jax.nn.gelu(act)).astype(p_ref.dtype)


def _peer_act(x, u, G):
    T, D = x.shape
    E = u.shape[0]
    tT = _pick(T, 2080, 16)
    tE = _pick(E, 512, LANES)
    nsplit = _row_groups(tT)
    return pl.pallas_call(
        functools.partial(_peer_act_kernel, nsplit=nsplit),
        name="peer_act",
        grid=(T // tT, E // tE),
        in_specs=[pl.BlockSpec((tT, D), lambda i, e: (i, 0), pipeline_mode=pl.Buffered(1)),
                  pl.BlockSpec((tE, D), lambda i, e: (e, 0)),
                  pl.BlockSpec((tT, tE), lambda i, e: (i, e))],
        out_specs=pl.BlockSpec((tT, tE), lambda i, e: (i, e)),
        out_shape=jax.ShapeDtypeStruct((T, E), BF16),
        compiler_params=_params("arbitrary", "arbitrary"),
    )(x, u, G)


def _peer_out_kernel(p_ref, v_ref, o_ref, *, nsplit):
    e = pl.program_id(1)

    @pl.when(e == 0)
    def _():
        o_ref[...] = jnp.zeros(o_ref.shape, o_ref.dtype)

    vb = v_ref[...].astype(BF16)
    rs = p_ref.shape[0] // nsplit
    for r in range(nsplit):
        rows = slice(r * rs, (r + 1) * rs)
        o_ref[rows, :] += jnp.dot(p_ref[rows, :], vb, preferred_element_type=F32)


def _peer_out(p, v):
    T, E = p.shape
    D = v.shape[1]
    tT = _pick(T, 1040, 16)
    tE = _pick(E, 512, LANES)
    nsplit = _row_groups(tT)
    return pl.pallas_call(
        functools.partial(_peer_out_kernel, nsplit=nsplit),
        name="peer_out",
        grid=(T // tT, E // tE),
        in_specs=[pl.BlockSpec((tT, tE), lambda i, e: (i, e)),
                  pl.BlockSpec((tE, D), lambda i, e: (e, 0))],
        out_specs=pl.BlockSpec((tT, D), lambda i, e: (i, 0), pipeline_mode=pl.Buffered(1)),
        out_shape=jax.ShapeDtypeStruct((T, D), F32),
        compiler_params=_params("arbitrary", "arbitrary"),
    )(p, v)


def _resid_kernel(x_ref, y_ref, gt_ref, o_ref):
    o_ref[0] = x_ref[0] + gt_ref[0] * y_ref[...]


def _resid_norm_kernel(x_ref, y_ref, gt_ref, g_ref, o_ref):
    x = x_ref[0] + gt_ref[0] * y_ref[...]
    o_ref[0] = x * lax.rsqrt(jnp.mean(x * x, axis=-1, keepdims=True) + RMS_EPS) * g_ref[...]


def _resid(x, y, row0, gt, g=None):
    B, L, D = x.shape
    tl = _pick(L, 256)
    assert row0 % tl == 0
    lb = L // tl
    blk = pl.BlockSpec((1, tl, D), lambda b, l: (b, l, 0))
    in_specs = [blk, pl.BlockSpec((tl, D), lambda b, l: (row0 // tl + b * lb + l, 0)),
                _mod_spec(gt, tl, D)]
    args = [x, y, gt]
    if g is not None:
        in_specs.append(pl.BlockSpec((1, D), lambda b, l: (0, 0)))
        args.append(g)
    return pl.pallas_call(
        _resid_kernel if g is None else _resid_norm_kernel,
        name="resid",
        grid=(B, L // tl),
        in_specs=in_specs,
        out_specs=blk,
        out_shape=jax.ShapeDtypeStruct((B, L, D), F32),
        compiler_params=_params("arbitrary", "arbitrary"),
    )(*args)


def kernel(x_prompt, x_sample, state_conv, state_lru_h, state_gla, c_prompt, c_sample, w_ada, b_ada, g_mix, w_in, conv_w, conv_b, lru_w_a, lru_b_a, lru_w_x, lru_b_x, lru_lambda, gla_w_alpha, gla_b_alpha, gla_g_norm, w_out, g_ffn, peer_w_q, peer_sub_keys, peer_u, peer_v, g_final):
    depth = w_ada.shape[0]
    Bp, L, D = x_prompt.shape
    Bs = x_sample.shape[0]
    assert x_sample.shape[1] == 1
    W = conv_w.shape[2]
    nt = conv_w.shape[1] - 1
    _, _, H, DK, DV = state_gla.shape
    rank = gla_w_alpha.shape[1]
    PH, _, NK, _ = peer_sub_keys.shape[1:]
    n_main = 2 * W + 2 * H * DK + 2 * H * DV
    assert w_in.shape[2] == n_main + rank and rank <= LANES
    offs = (2 * W, 2 * W + H * DK, 2 * W + 2 * H * DK, 2 * W + 2 * H * DK + H * DV)
    Tp = Bp * L

    R = Bp + Bs
    Rp = -(-R // SUBLANES) * SUBLANES
    c_all = jnp.pad(jnp.concatenate([c_prompt, c_sample], axis=0), ((0, Rp - R), (0, 0)))

    x_p = x_prompt
    x_s = x_sample.reshape(1, Bs, D)
    outs = [[] for _ in range(6)]
    for l in range(depth):
        ada = _ada(c_all, w_ada[l], b_ada[l][None])
        mods_p = [ada[:Bp, i * D:(i + 1) * D][:, None, :] for i in range(6)]
        mods_s = [ada[Bp:R, i * D:(i + 1) * D][None] for i in range(6)]

        w_in_t = jnp.transpose(w_in[l])
        in_proj = functools.partial(_matmul, w=w_in_t, N=n_main, w_rows_are_outputs=True,
                                    vmem_limit=IN_PROJ_VMEM_LIMIT)
        gate = functools.partial(_gla_gate, w_t=w_in_t, row0=n_main, w2=gla_w_alpha[l],
                                 b2=gla_b_alpha[l][None])
        wa = lru_w_a[l].astype(BF16)
        wx = lru_w_x[l].astype(BF16)
        lru_vecs = (conv_w[l], conv_b[l][None], wa, lru_b_a[l][None], wx, lru_b_x[l][None],
                    lru_lambda[l][None])
        gn = gla_g_norm[l]

        sh1, sc1, gt1, sh2, sc2, gt2 = mods_p
        hn = _norm_mod(x_p, g_mix[l][None], sc1, sh1, BF16).reshape(Tp, D)
        proj = in_proj(hn).reshape(Bp, L, n_main)
        la = gate(hn).reshape(Bp, L, H * DK)
        y_lru, h_p, conv_p = _lru_prompt(
            proj, jnp.zeros((Bp, nt, W), F32), jnp.zeros((Bp, 1, W), F32), *lru_vecs, W)
        y_gla, s_p = _gla_prompt(proj, la, jnp.zeros((Bp, H, DK, DV), F32), gn[None], offs)
        x1_p = _outproj(y_lru, y_gla, w_out[l], x_p, gt1)
        sc2_p, sh2_p, gt2_p = sc2, sh2, gt2

        sh1, sc1, gt1, sh2, sc2, gt2 = mods_s
        hn = _norm_mod(x_s, g_mix[l][None], sc1, sh1, BF16).reshape(Bs, D)
        proj = in_proj(hn)
        la = gate(hn)
        tail = jnp.transpose(state_conv[l], (1, 0, 2))
        y_lru, h_s = _lru_decode(proj, tail, state_lru_h[l], *lru_vecs, W)
        conv_s = jnp.concatenate([state_conv[l][:, 1:], proj[:, None, :W]], axis=1)
        qo, ko, vo, ro = offs
        heads = lambda a, d: a.reshape(Bs, H, 1, d)
        y_gla, s_s = _gla_decode(
            heads(proj[:, qo:ko], DK), heads(proj[:, ko:vo], DK), heads(proj[:, vo:ro], DV),
            heads(proj[:, ro:n_main], DV), heads(la, DK), state_gla[l], gn.reshape(H, 1, DV))
        x1_s = _outproj(y_lru[None], y_gla.reshape(1, Bs, H * DV), w_out[l], x_s, gt1)
        gt2_s = gt2

        hn2 = _norm_mod_cat(x1_p, x1_s, g_ffn[l][None], sc2_p, sh2_p, sc2, sh2, BF16)
        T = Tp + Bs
        q = _matmul(hn2, peer_w_q[l])
        e1, e2, gate = _peer_topk(q, peer_sub_keys[l])
        G = _peer_gates(e1.reshape(PH * PEER_TOPK, T), e2.reshape(PH * PEER_TOPK, T),
                        gate.reshape(PH * PEER_TOPK, T), NK)
        y = _peer_out(_peer_act(hn2, peer_u[l], G), peer_v[l])
        last = l == depth - 1
        x_p = _resid(x1_p, y, 0, gt2_p, g_final[None] if last else None)
        x_s = _resid(x1_s, y, Tp, gt2_s, g_final[None] if last else None)

        for lst, val in zip(outs, (conv_p, h_p.reshape(Bp, W), s_p, conv_s, h_s, s_s)):
            lst.append(val)

    return (x_p, x_s.reshape(Bs, 1, D)) + tuple(jnp.stack(o) for o in outs)
```

```python
import functools

import jax
import jax.numpy as jnp
from jax import lax
from jax.experimental import pallas as pl
from jax.experimental.pallas import tpu as pltpu

F32 = jnp.float32
BF16 = jnp.bfloat16

RMS_EPS = 1e-6
RG_C = 8.0
GLA_TAU = 16.0
PEER_TOPK = 16
GLA_CHUNK = 128
GLA_SUB = 8
LANES = 128
SUBLANES = 8
VMEM_LIMIT = 56 * 1024 * 1024
BIG_VMEM_LIMIT = 60 * 1024 * 1024


def _params(*sem):
    return pltpu.CompilerParams(dimension_semantics=sem, vmem_limit_bytes=VMEM_LIMIT)


def _pick(n, pref, mult=SUBLANES):
    best = None
    for d in range(mult, min(n, pref) + 1, mult):
        if n % d == 0:
            best = d
    return n if best is None else best


def _softplus(y):
    return jnp.maximum(y, 0.0) + jnp.log1p(jnp.exp(-jnp.abs(y)))


def _log_sigmoid(z):
    return jnp.minimum(z, 0.0) - jnp.log1p(jnp.exp(-jnp.abs(z)))


def _row_to_col(row, n):
    r = lax.broadcasted_iota(jnp.int32, (n, n), 0)
    c = lax.broadcasted_iota(jnp.int32, (n, n), 1)
    return jnp.sum(jnp.where(r == c, jnp.broadcast_to(row, (n, n)), 0.0), axis=1, keepdims=True)


def _ada_kernel(c_ref, w_ref, b_ref, o_ref):
    a = jax.nn.silu(c_ref[...]).astype(BF16)
    o_ref[...] = jnp.dot(a, w_ref[...].astype(BF16), preferred_element_type=F32) + b_ref[...]


def _ada(c, w, b):
    R, D = c.shape
    N = w.shape[1]
    tn = _pick(N, 512, LANES)
    return pl.pallas_call(
        _ada_kernel,
        name="ada",
        grid=(N // tn,),
        in_specs=[pl.BlockSpec((R, D), lambda j: (0, 0)),
                  pl.BlockSpec((D, tn), lambda j: (0, j)),
                  pl.BlockSpec((1, tn), lambda j: (0, j))],
        out_specs=pl.BlockSpec((R, tn), lambda j: (0, j)),
        out_shape=jax.ShapeDtypeStruct((R, N), F32),
        compiler_params=_params("arbitrary"),
    )(c, w, b)


def _norm_mod_kernel(x_ref, g_ref, sc_ref, sh_ref, o_ref):
    x = x_ref[0]
    y = x * lax.rsqrt(jnp.mean(x * x, axis=-1, keepdims=True) + RMS_EPS) * g_ref[...]
    o_ref[0] = (y * (1.0 + sc_ref[0]) + sh_ref[0]).astype(o_ref.dtype)


def _mod_spec(m, tl, D):
    if m.shape[1] == 1:
        return pl.BlockSpec((1, 1, D), lambda b, l: (b, 0, 0))
    return pl.BlockSpec((1, tl, D), lambda b, l: (b, l, 0))


def _norm_mod(x, g, sc, sh, out_dtype):
    B, L, D = x.shape
    tl = _pick(L, 256)
    return pl.pallas_call(
        _norm_mod_kernel,
        name="norm_mod",
        grid=(B, L // tl),
        in_specs=[pl.BlockSpec((1, tl, D), lambda b, l: (b, l, 0)),
                  pl.BlockSpec((1, D), lambda b, l: (0, 0)),
                  _mod_spec(sc, tl, D), _mod_spec(sh, tl, D)],
        out_specs=pl.BlockSpec((1, tl, D), lambda b, l: (b, l, 0)),
        out_shape=jax.ShapeDtypeStruct((B, L, D), out_dtype),
        compiler_params=_params("arbitrary", "arbitrary"),
    )(x, g, sc, sh)


def _norm_mod_cat_kernel(xp_ref, xs_ref, g_ref, scp_ref, shp_ref, scs_ref, shs_ref, o_ref, *, n_p):
    i = pl.program_id(0)

    def f(x, sc, sh):
        y = x * lax.rsqrt(jnp.mean(x * x, axis=-1, keepdims=True) + RMS_EPS) * g_ref[...]
        return (y * (1.0 + sc) + sh).astype(o_ref.dtype)

    @pl.when(i < n_p)
    def _():
        o_ref[...] = f(xp_ref[0], scp_ref[0], shp_ref[0])

    @pl.when(i >= n_p)
    def _():
        o_ref[...] = f(xs_ref[0], scs_ref[0], shs_ref[0])


def _norm_mod_cat(xp, xs, g, scp, shp, scs, shs, out_dtype):
    Bp, L, D = xp.shape
    Bs = xs.shape[1]
    tl = _pick(Bs, 256)
    assert L % tl == 0
    lb = L // tl
    n_p = Bp * lb
    n_s = Bs // tl
    pb = lambda i: jnp.minimum(i, n_p - 1)
    sb = lambda i: jnp.maximum(i - n_p, 0)
    pmod = pl.BlockSpec((1, 1, D), lambda i: (pb(i) // lb, 0, 0))
    smod = pl.BlockSpec((1, tl, D), lambda i: (0, sb(i), 0))
    return pl.pallas_call(
        functools.partial(_norm_mod_cat_kernel, n_p=n_p),
        name="norm_mod_cat",
        grid=(n_p + n_s,),
        in_specs=[pl.BlockSpec((1, tl, D), lambda i: (pb(i) // lb, pb(i) % lb, 0)),
                  pl.BlockSpec((1, tl, D), lambda i: (0, sb(i), 0)),
                  pl.BlockSpec((1, D), lambda i: (0, 0)),
                  pmod, pmod, smod, smod],
        out_specs=pl.BlockSpec((tl, D), lambda i: (i, 0)),
        out_shape=jax.ShapeDtypeStruct((Bp * L + Bs, D), out_dtype),
        compiler_params=_params("arbitrary"),
    )(xp, xs, g, scp, shp, scs, shs)


def _mm_kernel(x_ref, w_ref, o_ref, wb, *, w_rows_are_outputs):
    @pl.when(pl.program_id(1) == 0)
    def _():
        wb[...] = w_ref[...].astype(BF16)

    dims = (((1,), (1,)), ((), ())) if w_rows_are_outputs else (((1,), (0,)), ((), ()))
    o_ref[...] = lax.dot_general(x_ref[...], wb[...], dims, preferred_element_type=F32)


def _matmul(x, w, N=None, *, w_rows_are_outputs=False, tm_pref=512, tn_pref=1024,
            vmem_limit=VMEM_LIMIT):
    M, K = x.shape
    n_all = w.shape[0] if w_rows_are_outputs else w.shape[1]
    N = n_all if N is None else N
    tm = _pick(M, tm_pref, 16)
    tn = _pick(N, tn_pref, LANES)
    wblk = (tn, K) if w_rows_are_outputs else (K, tn)
    wmap = (lambda j, i: (j, 0)) if w_rows_are_outputs else (lambda j, i: (0, j))
    return pl.pallas_call(
        functools.partial(_mm_kernel, w_rows_are_outputs=w_rows_are_outputs),
        name="matmul",
        grid=(N // tn, M // tm),
        in_specs=[pl.BlockSpec((tm, K), lambda j, i: (i, 0)), pl.BlockSpec(wblk, wmap)],
        out_specs=pl.BlockSpec((tm, tn), lambda j, i: (i, j)),
        out_shape=jax.ShapeDtypeStruct((M, N), F32),
        scratch_shapes=[pltpu.VMEM(wblk, BF16)],
        compiler_params=pltpu.CompilerParams(dimension_semantics=("arbitrary", "arbitrary"),
                                             vmem_limit_bytes=vmem_limit),
    )(x, w)


def _gate_kernel(x_ref, w1_ref, w2_ref, b2_ref, o_ref):
    a = lax.dot_general(x_ref[...], w1_ref[...].astype(BF16), (((1,), (1,)), ((), ())),
                        preferred_element_type=F32)
    z = jnp.dot(a.astype(BF16), w2_ref[...].astype(BF16), preferred_element_type=F32) + b2_ref[...]
    o_ref[...] = _log_sigmoid(z) / GLA_TAU


def _gla_gate(hn, w_t, row0, w2, b2):
    M, K = hn.shape
    R, N = w2.shape
    assert row0 % R == 0 and R % SUBLANES == 0
    tm = _pick(M, 512)
    return pl.pallas_call(
        _gate_kernel,
        name="gla_gate",
        grid=(M // tm,),
        in_specs=[pl.BlockSpec((tm, K), lambda i: (i, 0)),
                  pl.BlockSpec((R, K), lambda i: (row0 // R, 0)),
                  pl.BlockSpec((R, N), lambda i: (0, 0)),
                  pl.BlockSpec((1, N), lambda i: (0, 0))],
        out_specs=pl.BlockSpec((tm, N), lambda i: (i, 0)),
        out_shape=jax.ShapeDtypeStruct((M, N), F32),
        compiler_params=_params("arbitrary"),
    )(hn, w_t, w2, b2)


def _lru_gates(xc, wa_ref, ba, wx_ref, bx, lam):
    nb, bd = wa_ref.shape[0], wa_ref.shape[1]
    xcb = xc.astype(BF16)
    ra, ri = [], []
    for n in range(nb):
        xs = xcb[:, n * bd:(n + 1) * bd]
        ra.append(jnp.dot(xs, wa_ref[n], preferred_element_type=F32))
        ri.append(jnp.dot(xs, wx_ref[n], preferred_element_type=F32))
    r = jax.nn.sigmoid(jnp.concatenate(ra, axis=1) + ba)
    i = jax.nn.sigmoid(jnp.concatenate(ri, axis=1) + bx)
    log_a = -RG_C * r * _softplus(-lam)
    a = jnp.exp(log_a)
    u = jnp.sqrt(-jnp.tanh(log_a) * (a * a + 1.0)) * i * xc
    return a, u


def _lru_prompt_kernel(xb_ref, gb_ref, tail_ref, h0_ref, cw_ref, cb_ref, wa_ref, ba_ref, wx_ref,
                       bx_ref, lam_ref, y_ref, hl_ref, conv_ref, buf, hcar):
    l = pl.program_id(2)
    nl = pl.num_programs(2)
    tl, tw = xb_ref.shape[1], xb_ref.shape[2]
    nt = tail_ref.shape[1]
    base = SUBLANES

    @pl.when(l == 0)
    def _():
        buf[base - nt:base, :] = tail_ref[0]
        hcar[...] = h0_ref[0]

    xb = xb_ref[0]
    buf[base:base + tl, :] = xb
    cw = cw_ref[...]
    xc = cb_ref[...] + xb * cw[nt:nt + 1]
    for j in range(nt):
        xc = xc + buf[base - nt + j:base - nt + j + tl, :] * cw[j:j + 1]
    new_tail = buf[base + tl - nt:base + tl, :]

    a, u = _lru_gates(xc, wa_ref, ba_ref[...], wx_ref, bx_ref[...], lam_ref[...])

    ng = tl // SUBLANES
    A = a.reshape(ng, SUBLANES, tw)
    Bv = u.reshape(ng, SUBLANES, tw)
    row = lax.broadcasted_iota(jnp.int32, (1, SUBLANES, tw), 1)
    s = 1
    while s < SUBLANES:
        As = pltpu.roll(A, s, axis=1)
        Bs = pltpu.roll(Bv, s, axis=1)
        m = row >= s
        Bv = jnp.where(m, A * Bs + Bv, Bv)
        A = jnp.where(m, A * As, A)
        s *= 2
    h = hcar[...]
    outs = []
    for g in range(ng):
        hg = A[g] * h + Bv[g]
        outs.append(hg)
        h = hg[SUBLANES - 1:SUBLANES]
    hseq = jnp.concatenate(outs, axis=0)
    hcar[...] = h
    buf[base - nt:base, :] = new_tail
    y_ref[0] = (hseq * jax.nn.gelu(gb_ref[0])).astype(y_ref.dtype)

    @pl.when(l == nl - 1)
    def _():
        hl_ref[0] = h
        conv_ref[0] = new_tail


def _lru_prompt(proj, tail, h0, cw, cb, wa, ba, wx, bx, lam, W):
    B, L, _ = proj.shape
    nt = tail.shape[1]
    bd = wa.shape[1]
    tw = _pick(W, 512, bd)
    tl = _pick(L, 256)
    nwb = W // tw
    col = lambda b, j, l: (0, j)
    return pl.pallas_call(
        _lru_prompt_kernel,
        name="lru_prompt",
        grid=(B, nwb, L // tl),
        in_specs=[pl.BlockSpec((1, tl, tw), lambda b, j, l: (b, l, j)),
                  pl.BlockSpec((1, tl, tw), lambda b, j, l: (b, l, j + nwb)),
                  pl.BlockSpec((1, nt, tw), lambda b, j, l: (b, 0, j)),
                  pl.BlockSpec((1, 1, tw), lambda b, j, l: (b, 0, j)),
                  pl.BlockSpec((nt + 1, tw), col),
                  pl.BlockSpec((1, tw), col),
                  pl.BlockSpec((tw // bd, bd, bd), lambda b, j, l: (j, 0, 0)),
                  pl.BlockSpec((1, tw), col),
                  pl.BlockSpec((tw // bd, bd, bd), lambda b, j, l: (j, 0, 0)),
                  pl.BlockSpec((1, tw), col),
                  pl.BlockSpec((1, tw), col)],
        out_specs=[pl.BlockSpec((1, tl, tw), lambda b, j, l: (b, l, j)),
                   pl.BlockSpec((1, 1, tw), lambda b, j, l: (b, 0, j)),
                   pl.BlockSpec((1, nt, tw), lambda b, j, l: (b, 0, j))],
        out_shape=[jax.ShapeDtypeStruct((B, L, W), BF16),
                   jax.ShapeDtypeStruct((B, 1, W), F32),
                   jax.ShapeDtypeStruct((B, nt, W), F32)],
        scratch_shapes=[pltpu.VMEM((tl + SUBLANES, tw), F32), pltpu.VMEM((1, tw), F32)],
        compiler_params=_params("arbitrary", "arbitrary", "arbitrary"),
    )(proj, proj, tail, h0, cw, cb, wa, ba, wx, bx, lam)


def _lru_decode_kernel(xb_ref, gb_ref, tail_ref, h0_ref, cw_ref, cb_ref, wa_ref, ba_ref, wx_ref,
                       bx_ref, lam_ref, y_ref, h_ref):
    nt = tail_ref.shape[0]
    xb = xb_ref[...]
    cw = cw_ref[...]
    xc = cb_ref[...] + xb * cw[nt:nt + 1]
    for j in range(nt):
        xc = xc + tail_ref[j] * cw[j:j + 1]
    a, u = _lru_gates(xc, wa_ref, ba_ref[...], wx_ref, bx_ref[...], lam_ref[...])
    h = a * h0_ref[...] + u
    h_ref[...] = h
    y_ref[...] = (h * jax.nn.gelu(gb_ref[...])).astype(y_ref.dtype)


def _lru_decode(proj, tail, h0, cw, cb, wa, ba, wx, bx, lam, W):
    B = proj.shape[0]
    nt = tail.shape[0]
    bd = wa.shape[1]
    tw = _pick(W, 512, bd)
    nwb = W // tw
    col = lambda j: (0, j)
    return pl.pallas_call(
        _lru_decode_kernel,
        name="lru_decode",
        grid=(nwb,),
        in_specs=[pl.BlockSpec((B, tw), col),
                  pl.BlockSpec((B, tw), lambda j: (0, j + nwb)),
                  pl.BlockSpec((nt, B, tw), lambda j: (0, 0, j)),
                  pl.BlockSpec((B, tw), col),
                  pl.BlockSpec((nt + 1, tw), col),
                  pl.BlockSpec((1, tw), col),
                  pl.BlockSpec((tw // bd, bd, bd), lambda j: (j, 0, 0)),
                  pl.BlockSpec((1, tw), col),
                  pl.BlockSpec((tw // bd, bd, bd), lambda j: (j, 0, 0)),
                  pl.BlockSpec((1, tw), col),
                  pl.BlockSpec((1, tw), col)],
        out_specs=[pl.BlockSpec((B, tw), col), pl.BlockSpec((B, tw), col)],
        out_shape=[jax.ShapeDtypeStruct((B, W), BF16), jax.ShapeDtypeStruct((B, W), F32)],
        compiler_params=_params("arbitrary"),
    )(proj, proj, tail, h0, cw, cb, wa, ba, wx, bx, lam)


def _head_norm_gate(o, gn, r):
    on = o * lax.rsqrt(jnp.mean(o * o, axis=-1, keepdims=True) + RMS_EPS) * gn
    return on * jax.nn.silu(r)


def _gla_chunk(q, k, v, r, g, gn, S, att):
    C, DK = q.shape
    sub = min(GLA_SUB, C)
    q = q * (DK ** -0.5)
    vb = v.astype(BF16)
    rr = lax.broadcasted_iota(jnp.int32, (C, C), 0)
    cc = lax.broadcasted_iota(jnp.int32, (C, C), 1)
    b = jnp.dot((rr >= cc).astype(F32), g, precision=lax.Precision.HIGHEST,
                preferred_element_type=F32)
    b_last = b[C - 1:C]

    inter = jnp.dot((q * jnp.exp(b)).astype(BF16), S[...].astype(BF16), preferred_element_type=F32)

    lane = lax.broadcasted_iota(jnp.int32, (sub, sub), 1)
    srow = lax.broadcasted_iota(jnp.int32, (sub, sub), 0)
    for I in range(C // sub):
        lo = I * sub
        qI, kI, bI = q[lo:lo + sub], k[lo:lo + sub], b[lo:lo + sub]
        acc = jnp.zeros((sub, sub), F32)
        for s in range(sub):
            w = qI * kI[s:s + 1] * jnp.exp(jnp.minimum(bI - bI[s:s + 1], 0.0))
            acc = jnp.where(lane == s, jnp.sum(w, axis=1, keepdims=True), acc)
        att[lo:lo + sub, lo:lo + sub] = jnp.where(srow >= lane, acc, 0.0)
        if lo > 0:
            ref = b[lo - 1:lo]
            qt = (qI * jnp.exp(bI - ref)).astype(BF16)
            kt = (k[0:lo] * jnp.exp(ref - b[0:lo])).astype(BF16)
            att[lo:lo + sub, 0:lo] = lax.dot_general(qt, kt, (((1,), (1,)), ((), ())),
                                                     preferred_element_type=F32)
        if lo + sub < C:
            att[lo:lo + sub, lo + sub:C] = jnp.zeros((sub, C - lo - sub), F32)
    o = inter + jnp.dot(att[...].astype(BF16), vb, preferred_element_type=F32)

    kh = (k * jnp.exp(b_last - b)).astype(BF16)
    upd = lax.dot_general(kh, vb, (((0,), (0,)), ((), ())), preferred_element_type=F32)
    S[...] = _row_to_col(jnp.exp(b_last), DK) * S[...] + upd
    return _head_norm_gate(o, gn, r)


def _gla_prompt_kernel(q_ref, k_ref, v_ref, r_ref, la_ref, s0_ref, gn_ref, y_ref, so_ref, S, att):
    c = pl.program_id(2)
    nc = pl.num_programs(2)
    hb, DK, DV = S.shape

    @pl.when(c == 0)
    def _():
        S[...] = s0_ref[0]

    for h in range(hb):
        kk = slice(h * DK, (h + 1) * DK)
        vv = slice(h * DV, (h + 1) * DV)
        y = _gla_chunk(q_ref[0, :, kk], k_ref[0, :, kk], v_ref[0, :, vv], r_ref[0, :, vv],
                       la_ref[0, :, kk], gn_ref[:, vv], S.at[h], att.at[h])
        y_ref[0, :, vv] = y.astype(y_ref.dtype)

    @pl.when(c == nc - 1)
    def _():
        so_ref[0] = S[...]


def _gla_prompt(proj, la, s0, gn, offs):
    B, L, _ = proj.shape
    _, H, DK, DV = s0.shape
    C = _pick(L, GLA_CHUNK)
    hb = 4 if H % 4 == 0 else (2 if H % 2 == 0 else 1)
    wk, wv = hb * DK, hb * DV
    qo, ko, vo, ro = offs
    assert all(o % wk == 0 for o in (qo, ko)) and all(o % wv == 0 for o in (vo, ro))
    return pl.pallas_call(
        _gla_prompt_kernel,
        name="gla_prompt",
        grid=(B, H // hb, L // C),
        in_specs=[pl.BlockSpec((1, C, wk), lambda b, h, c: (b, c, qo // wk + h)),
                  pl.BlockSpec((1, C, wk), lambda b, h, c: (b, c, ko // wk + h)),
                  pl.BlockSpec((1, C, wv), lambda b, h, c: (b, c, vo // wv + h)),
                  pl.BlockSpec((1, C, wv), lambda b, h, c: (b, c, ro // wv + h)),
                  pl.BlockSpec((1, C, wk), lambda b, h, c: (b, c, h)),
                  pl.BlockSpec((1, hb, DK, DV), lambda b, h, c: (b, h, 0, 0)),
                  pl.BlockSpec((1, wv), lambda b, h, c: (0, h))],
        out_specs=[pl.BlockSpec((1, C, wv), lambda b, h, c: (b, c, h)),
                   pl.BlockSpec((1, hb, DK, DV), lambda b, h, c: (b, h, 0, 0))],
        out_shape=[jax.ShapeDtypeStruct((B, L, H * DV), BF16),
                   jax.ShapeDtypeStruct((B, H, DK, DV), F32)],
        scratch_shapes=[pltpu.VMEM((hb, DK, DV), F32), pltpu.VMEM((hb, C, C), F32)],
        compiler_params=_params("arbitrary", "arbitrary", "arbitrary"),
    )(proj, proj, proj, proj, la, s0, gn)


def _gla_decode_kernel(q_ref, k_ref, v_ref, r_ref, la_ref, s0_ref, gn_ref, y_ref, so_ref):
    bb, H, _, DK = q_ref.shape
    for b in range(bb):
        for h in range(H):
            q = q_ref[b, h] * (DK ** -0.5)
            S = (_row_to_col(jnp.exp(la_ref[b, h]), DK) * s0_ref[b, h]
                 + _row_to_col(k_ref[b, h], DK) * v_ref[b, h])
            so_ref[b, h] = S
            o = jnp.sum(_row_to_col(q, DK) * S, axis=0, keepdims=True)
            y_ref[b, h] = _head_norm_gate(o, gn_ref[h], r_ref[b, h]).astype(y_ref.dtype)


def _gla_decode(q, k, v, r, la, s0, gn):
    B, H, DK, DV = s0.shape
    bb = 2 if B % 2 == 0 else 1
    vec = lambda d: pl.BlockSpec((bb, H, 1, d), lambda b: (b, 0, 0, 0))
    mat = pl.BlockSpec((bb, H, DK, DV), lambda b: (b, 0, 0, 0))
    return pl.pallas_call(
        _gla_decode_kernel,
        name="gla_decode",
        grid=(B // bb,),
        in_specs=[vec(DK), vec(DK), vec(DV), vec(DV), vec(DK), mat,
                  pl.BlockSpec((H, 1, DV), lambda b: (0, 0, 0))],
        out_specs=[vec(DV), mat],
        out_shape=[jax.ShapeDtypeStruct((B, H, 1, DV), BF16),
                   jax.ShapeDtypeStruct((B, H, DK, DV), F32)],
        compiler_params=_params("arbitrary"),
    )(q, k, v, r, la, s0, gn)


def _outproj_kernel(ya_ref, yb_ref, wa_ref, wb_ref, x_ref, gt_ref, o_ref, wab, wbb):
    @pl.when((pl.program_id(1) == 0) & (pl.program_id(2) == 0))
    def _():
        wab[...] = wa_ref[...].astype(BF16)
        wbb[...] = wb_ref[...].astype(BF16)

    mix = (jnp.dot(ya_ref[0], wab[...], preferred_element_type=F32)
           + jnp.dot(yb_ref[0], wbb[...], preferred_element_type=F32))
    o_ref[0] = x_ref[0] + gt_ref[0] * mix


def _outproj(ya, yb, w, x, gt):
    B, L, D = x.shape
    Ka, Kb = ya.shape[2], yb.shape[2]
    assert Ka == Kb
    tm = _pick(L, 1024)
    tn = _pick(D, 512, LANES)
    gspec =(pl.BlockSpec((1, 1, tn), lambda j, b, l: (b, 0, j)) if gt.shape[1] == 1
             else pl.BlockSpec((1, tm, tn), lambda j, b, l: (b, l, j)))
    return pl.pallas_call(
        _outproj_kernel,
        name="outproj",
        grid=(D // tn, B, L // tm),
        in_specs=[pl.BlockSpec((1, tm, Ka), lambda j, b, l: (b, l, 0)),
                  pl.BlockSpec((1, tm, Kb), lambda j, b, l: (b, l, 0)),
                  pl.BlockSpec((Ka, tn), lambda j, b, l: (0, j)),
                  pl.BlockSpec((Kb, tn), lambda j, b, l: (1, j)),
                  pl.BlockSpec((1, tm, tn), lambda j, b, l: (b, l, j)),
                  gspec],
        out_specs=pl.BlockSpec((1, tm, tn), lambda j, b, l: (b, l, j)),
        out_shape=jax.ShapeDtypeStruct((B, L, D), F32),
        scratch_shapes=[pltpu.VMEM((Ka, tn), BF16), pltpu.VMEM((Kb, tn), BF16)],
        compiler_params=_params("arbitrary", "arbitrary", "arbitrary"),
    )(ya, yb, w, w, x, gt)


def _topk_rows(problems, ids, kk, fill):
    ss = [p[0] for p in problems]
    for j in range(kk):
        for n, (_, val_ref, idx_ref) in enumerate(problems):
            s = ss[n]
            m = jnp.max(s, axis=0, keepdims=True)
            idx = jnp.min(jnp.where(s == m, ids, fill), axis=0, keepdims=True)
            val_ref[j:j + 1, :] = m
            idx_ref[j:j + 1, :] = idx
            ss[n] = jnp.where(ids == idx, -jnp.inf, s)


def _pair_layout(K):
    segs, ids, r0 = [], [], 0
    b = 0
    while b < K and K // (b + 1) > 1:
        na = K // (b + 1)
        segs.append((r0, (0, na), (b, b + 1)))
        rows = -(-na // SUBLANES) * SUBLANES
        ids += [a * K + b for a in range(na)] + [K * K] * (rows - na)
        r0 += rows
        b += 1
    if b < K:
        nb = K - b
        segs.append((r0, (0, 1), (b, K)))
        rows = -(-nb // SUBLANES) * SUBLANES
        ids += list(range(b, K)) + [K * K] * (rows - nb)
        r0 += rows
    return segs, ids, r0


def _peer_topk_kernel(q_ref, keys_ref, pid_ref, e1_ref, e2_ref, g_ref, sv, si, cand, sc, pos,
                      *, segs):
    tT = q_ref.shape[0]
    H, _, NK, dq = keys_ref.shape
    K = PEER_TOPK
    hp = sv.shape[0]
    ids = lax.broadcasted_iota(jnp.int32, (NK, tT), 0).astype(F32)
    pids = pid_ref[...]

    def heads(i, carry):
        halves = []
        for n in range(hp):
            for p in range(2):
                h = i * hp + n
                off = pl.multiple_of((2 * h + p) * dq, dq)
                qp = q_ref[:, pl.ds(off, dq)].astype(BF16)
                s = lax.dot_general(keys_ref[h, p].astype(BF16), qp, (((1,), (1,)), ((), ())),
                                    preferred_element_type=F32)
                halves.append((s, sv.at[n, p], si.at[n, p]))
        _topk_rows(halves, ids, K, float(NK))
        pairs = []
        for n in range(hp):
            v0, v1 = sv[n, 0], sv[n, 1]
            cand[n] = jnp.full(cand.shape[1:], -jnp.inf, F32)
            for r0, (a0, a1), (b0, b1) in segs:
                rows = max(a1 - a0, b1 - b0)
                cand[n, r0:r0 + rows, :] = v0[a0:a1] + v1[b0:b1]
            pairs.append((cand[n], sc.at[n], pos.at[n]))
        _topk_rows(pairs, pids, K, float(K * K))
        for n in range(hp):
            h = i * hp + n
            pa = jnp.floor(pos[n] * (1.0 / K))
            pb = pos[n] - pa * K
            i0, i1 = si[n, 0], si[n, 1]
            e1 = jnp.zeros((K, tT), F32)
            e2 = jnp.zeros((K, tT), F32)
            for a in range(K):
                e1 = jnp.where(pa == a, i0[a:a + 1], e1)
                e2 = jnp.where(pb == a, i1[a:a + 1], e2)
            scv = sc[n]
            ex = jnp.exp(scv - jnp.max(scv, axis=0, keepdims=True))
            e1_ref[h] = e1
            e2_ref[h] = e2
            g_ref[h] = ex / jnp.sum(ex, axis=0, keepdims=True)
        return carry

    lax.fori_loop(0, H // hp, heads, 0)


def _peer_topk(q, keys):
    T = q.shape[0]
    H, _, NK, dq = keys.shape
    K = PEER_TOPK
    tT = _pick(T, LANES, LANES)
    hp = 4 if H % 4 == 0 else 1
    segs, ids, nr = _pair_layout(K)
    pids = jnp.broadcast_to(jnp.asarray(ids, F32)[:, None], (nr, tT))
    out = jax.ShapeDtypeStruct((H, K, T), F32)
    ospec = pl.BlockSpec((H, K, tT), lambda i: (0, 0, i))
    return pl.pallas_call(
        functools.partial(_peer_topk_kernel, segs=segs),
        name="peer_topk",
        grid=(T // tT,),
        in_specs=[pl.BlockSpec((tT, 2 * H * dq), lambda i: (i, 0)),
                  pl.BlockSpec((H, 2, NK, dq), lambda i: (0, 0, 0, 0)),
                  pl.BlockSpec((nr, tT), lambda i: (0, 0))],
        out_specs=[ospec, ospec, ospec],
        out_shape=[out, out, out],
        scratch_shapes=[pltpu.VMEM((hp, 2, K, tT), F32), pltpu.VMEM((hp, 2, K, tT), F32),
                        pltpu.VMEM((hp, nr, tT), F32), pltpu.VMEM((hp, K, tT), F32),
                        pltpu.VMEM((hp, K, tT), F32)],
        compiler_params=_params("arbitrary"),
    )(q, keys, pids)


def _peer_gate_kernel(e1_ref, e2_ref, g_ref, o_ref, e1t, e2t, gt, gbuf, *, unroll, pitch):
    NS, tT = e1_ref.shape
    NK = gbuf.shape[1]
    e1t[...] = e1_ref[...].T
    e2t[...] = e2_ref[...].T
    gt[...] = g_ref[...].T
    kid = lax.broadcasted_iota(jnp.int32, (NK, NS), 0).astype(F32)

    def body(i, carry):
        for j in range(unroll):
            t = i * unroll + j
            a = jnp.where(kid == e1t[pl.ds(t, 1), :], gt[pl.ds(t, 1), :], 0.0).astype(BF16)
            b = jnp.where(kid == e2t[pl.ds(t, 1), :], 1.0, 0.0).astype(BF16)
            gbuf[pl.ds(pl.multiple_of(t * pitch, SUBLANES), NK), :] = lax.dot_general(
                a, b, (((1,), (1,)), ((), ())), preferred_element_type=F32)
        return carry

    lax.fori_loop(0, tT // unroll, body, 0)
    for i1 in range(NK):
        o_ref[:, i1 * NK:(i1 + 1) * NK] = gbuf[pl.ds(i1, tT, stride=pitch), :].astype(o_ref.dtype)


def _peer_gates(e1, e2, g, NK):
    NS, T = e1.shape
    tT = _pick(T, 128, LANES)
    unroll = 32 if tT % 32 == 0 else 1
    pitch = NK + SUBLANES
    ispec = pl.BlockSpec((NS, tT), lambda i: (0, i))
    return pl.pallas_call(
        functools.partial(_peer_gate_kernel, unroll=unroll, pitch=pitch),
        name="peer_gate",
        grid=(T // tT,),
        in_specs=[ispec, ispec, ispec],
        out_specs=pl.BlockSpec((tT, NK * NK), lambda i: (i, 0)),
        out_shape=jax.ShapeDtypeStruct((T, NK * NK), BF16),
        scratch_shapes=[pltpu.VMEM((tT, NS), F32)] * 3 + [pltpu.VMEM((tT * pitch, NK), F32)],
        compiler_params=_params("arbitrary"),
    )(e1, e2, g)


def _peer_dense_kernel(x_ref, u_ref, v_ref, g_ref, o_ref, *, nsplit):
    e = pl.program_id(1)

    @pl.when(e == 0)
    def _():
        o_ref[...] = jnp.zeros(o_ref.shape, o_ref.dtype)

    rs = x_ref.shape[0] // nsplit
    for r in range(nsplit):
        rows = slice(r * rs, (r + 1) * rs)
        act = lax.dot_general(x_ref[rows, :], u_ref[...], (((1,), (1,)), ((), ())),
                              preferred_element_type=F32)
        p = (g_ref[rows, :].astype(F32) * jax.nn.gelu(act)).astype(BF16)
        o_ref[rows, :] += jnp.dot(p, v_ref[...], preferred_element_type=F32)


def _peer_dense(x, u, v, G):
    T, D = x.shape
    E = u.shape[0]
    tT = _pick(T, 640, 16)
    tE = _pick(E, 512, LANES)
    nsplit = 2 if tT % 32 == 0 else 1
    return pl.pallas_call(
        functools.partial(_peer_dense_kernel, nsplit=nsplit),
        name="peer_dense",
        grid=(T // tT, E // tE),
        in_specs=[pl.BlockSpec((tT, D), lambda i, e: (i, 0)),
                  pl.BlockSpec((tE, D), lambda i, e: (e, 0)),
                  pl.BlockSpec((tE, D), lambda i, e: (e, 0)),
                  pl.BlockSpec((tT, tE), lambda i, e: (i, e))],
        out_specs=pl.BlockSpec((tT, D), lambda i, e: (i, 0)),
        out_shape=jax.ShapeDtypeStruct((T, D), F32),
        compiler_params=_params("arbitrary", "arbitrary"),
    )(x, u, v, G)


def _row_groups(rows):
    for n in (5, 4, 2):
        if rows % (16 * n) == 0:
            return n
    return 1


def _peer_act_kernel(x_ref, u_ref, g_ref, p_ref, *, nsplit):
    ub = u_ref[...].astype(BF16)
    rs = x_ref.shape[0] // nsplit
    for r in range(nsplit):
        rows = slice(r * rs, (r + 1) * rs)
        act = lax.dot_general(x_ref[rows, :], ub, (((1,), (1,)), ((), ())),
                              preferred_element_type=F32)
        p_ref[rows, :] = (g_ref[rows, :].astype(F32) * jax.nn.gelu(act)).astype(p_ref.dtype)


def _peer_act(x, u, G):
    T, D = x.shape
    E = u.shape[0]
    tT = _pick(T, 2080, 16)
    tE = _pick(E, 512, LANES)
    nsplit = _row_groups(tT)
    return pl.pallas_call(
        functools.partial(_peer_act_kernel, nsplit=nsplit),
        name="peer_act",
        grid=(T // tT, E // tE),
        in_specs=[pl.BlockSpec((tT, D), lambda i, e: (i, 0), pipeline_mode=pl.Buffered(1)),
                  pl.BlockSpec((tE, D), lambda i, e: (e, 0)),
                  pl.BlockSpec((tT, tE), lambda i, e: (i, e))],
        out_specs=pl.BlockSpec((tT, tE), lambda i, e: (i, e)),
        out_shape=jax.ShapeDtypeStruct((T, E), BF16),
        compiler_params=_params("arbitrary", "arbitrary"),
    )(x, u, G)


def _peer_out_kernel(p_ref, v_ref, o_ref, *, nsplit):
    e = pl.program_id(1)

    @pl.when(e == 0)
    def _():
        o_ref[...] = jnp.zeros(o_ref.shape, o_ref.dtype)

    vb = v_ref[...].astype(BF16)
    rs = p_ref.shape[0] // nsplit
    for r in range(nsplit):
        rows = slice(r * rs, (r + 1) * rs)
        o_ref[rows, :] += jnp.dot(p_ref[rows, :], vb, preferred_element_type=F32)


def _peer_out(p, v):
    T, E = p.shape
    D = v.shape[1]
    tT = _pick(T, 1664, 16)
    tE = _pick(E, 512, LANES)
    nsplit = 8 if tT % 128 == 0 else _row_groups(tT)
    return pl.pallas_call(
        functools.partial(_peer_out_kernel, nsplit=nsplit),
        name="peer_out",
        grid=(T // tT, E // tE),
        in_specs=[pl.BlockSpec((tT, tE), lambda i, e: (i, e)),
                  pl.BlockSpec((tE, D), lambda i, e: (e, 0))],
        out_specs=pl.BlockSpec((tT, D), lambda i, e: (i, 0), pipeline_mode=pl.Buffered(1)),
        out_shape=jax.ShapeDtypeStruct((T, D), F32),
        compiler_params=pltpu.CompilerParams(dimension_semantics=("arbitrary", "arbitrary"),
                                             vmem_limit_bytes=BIG_VMEM_LIMIT),
    )(p, v)


def _resid_kernel(x_ref, y_ref, gt_ref, o_ref):
    o_ref[0] = x_ref[0] + gt_ref[0] * y_ref[...]


def _resid_norm_kernel(x_ref, y_ref, gt_ref, g_ref, o_ref):
    x = x_ref[0] + gt_ref[0] * y_ref[...]
    o_ref[0] = x * lax.rsqrt(jnp.mean(x * x, axis=-1, keepdims=True) + RMS_EPS) * g_ref[...]


def _resid(x, y, row0, gt, g=None):
    B, L, D = x.shape
    tl = _pick(L, 256)
    assert row0 % tl == 0
    lb = L // tl
    blk = pl.BlockSpec((1, tl, D), lambda b, l: (b, l, 0))
    in_specs = [blk, pl.BlockSpec((tl, D), lambda b, l: (row0 // tl + b * lb + l, 0)),
                _mod_spec(gt, tl, D)]
    args = [x, y, gt]
    if g is not None:
        in_specs.append(pl.BlockSpec((1, D), lambda b, l: (0, 0)))
        args.append(g)
    return pl.pallas_call(
        _resid_kernel if g is None else _resid_norm_kernel,
        name="resid",
        grid=(B, L // tl),
        in_specs=in_specs,
        out_specs=blk,
        out_shape=jax.ShapeDtypeStruct((B, L, D), F32),
        compiler_params=_params("arbitrary", "arbitrary"),
    )(*args)


def kernel(x_prompt, x_sample, state_conv, state_lru_h, state_gla, c_prompt, c_sample, w_ada, b_ada, g_mix, w_in, conv_w, conv_b, lru_w_a, lru_b_a, lru_w_x, lru_b_x, lru_lambda, gla_w_alpha, gla_b_alpha, gla_g_norm, w_out, g_ffn, peer_w_q, peer_sub_keys, peer_u, peer_v, g_final):
    depth = w_ada.shape[0]
    Bp, L, D = x_prompt.shape
    Bs = x_sample.shape[0]
    assert x_sample.shape[1] == 1
    W = conv_w.shape[2]
    nt = conv_w.shape[1] - 1
    _, _, H, DK, DV = state_gla.shape
    rank = gla_w_alpha.shape[1]
    PH, _, NK, _ = peer_sub_keys.shape[1:]
    n_main = 2 * W + 2 * H * DK + 2 * H * DV
    assert w_in.shape[2] == n_main + rank and rank <= LANES
    offs = (2 * W, 2 * W + H * DK, 2 * W + 2 * H * DK, 2 * W + 2 * H * DK + H * DV)
    Tp = Bp * L

    R = Bp + Bs
    Rp = -(-R // SUBLANES) * SUBLANES
    c_all = jnp.pad(jnp.concatenate([c_prompt, c_sample], axis=0), ((0, Rp - R), (0, 0)))

    x_p = x_prompt
    x_s = x_sample.reshape(1, Bs, D)
    outs = [[] for _ in range(6)]
    for l in range(depth):
        ada = _ada(c_all, w_ada[l], b_ada[l][None])
        mods_p = [ada[:Bp, i * D:(i + 1) * D][:, None, :] for i in range(6)]
        mods_s = [ada[Bp:R, i * D:(i + 1) * D][None] for i in range(6)]

        w_in_t = jnp.transpose(w_in[l])
        in_proj = functools.partial(_matmul, w=w_in_t, N=n_main, w_rows_are_outputs=True,
                                    vmem_limit=BIG_VMEM_LIMIT)
        gate = functools.partial(_gla_gate, w_t=w_in_t, row0=n_main, w2=gla_w_alpha[l],
                                 b2=gla_b_alpha[l][None])
        wa = lru_w_a[l].astype(BF16)
        wx = lru_w_x[l].astype(BF16)
        lru_vecs = (conv_w[l], conv_b[l][None], wa, lru_b_a[l][None], wx, lru_b_x[l][None],
                    lru_lambda[l][None])
        gn = gla_g_norm[l]

        sh1, sc1, gt1, sh2, sc2, gt2 = mods_p
        hn = _norm_mod(x_p, g_mix[l][None], sc1, sh1, BF16).reshape(Tp, D)
        proj = in_proj(hn).reshape(Bp, L, n_main)
        la = gate(hn).reshape(Bp, L, H * DK)
        y_lru, h_p, conv_p = _lru_prompt(
            proj, jnp.zeros((Bp, nt, W), F32), jnp.zeros((Bp, 1, W), F32), *lru_vecs, W)
        y_gla, s_p = _gla_prompt(proj, la, jnp.zeros((Bp, H, DK, DV), F32), gn[None], offs)
        x1_p = _outproj(y_lru, y_gla, w_out[l], x_p, gt1)
        sc2_p, sh2_p, gt2_p = sc2, sh2, gt2

        sh1, sc1, gt1, sh2, sc2, gt2 = mods_s
        hn = _norm_mod(x_s, g_mix[l][None], sc1, sh1, BF16).reshape(Bs, D)
        proj = in_proj(hn)
        la = gate(hn)
        tail = jnp.transpose(state_conv[l], (1, 0, 2))
        y_lru, h_s = _lru_decode(proj, tail, state_lru_h[l], *lru_vecs, W)
        conv_s = jnp.concatenate([state_conv[l][:, 1:], proj[:, None, :W]], axis=1)
        qo, ko, vo, ro = offs
        heads = lambda a, d: a.reshape(Bs, H, 1, d)
        y_gla, s_s = _gla_decode(
            heads(proj[:, qo:ko], DK), heads(proj[:, ko:vo], DK), heads(proj[:, vo:ro], DV),
            heads(proj[:, ro:n_main], DV), heads(la, DK), state_gla[l], gn.reshape(H, 1, DV))
        x1_s = _outproj(y_lru[None], y_gla.reshape(1, Bs, H * DV), w_out[l], x_s, gt1)
        gt2_s = gt2

        hn2 = _norm_mod_cat(x1_p, x1_s, g_ffn[l][None], sc2_p, sh2_p, sc2, sh2, BF16)
        T = Tp + Bs
        q = _matmul(hn2, peer_w_q[l])
        e1, e2, gate = _peer_topk(q, peer_sub_keys[l])
        G = _peer_gates(e1.reshape(PH * PEER_TOPK, T), e2.reshape(PH * PEER_TOPK, T),
                        gate.reshape(PH * PEER_TOPK, T), NK)
        y = _peer_out(_peer_act(hn2, peer_u[l], G), peer_v[l])
        last = l == depth - 1
        x_p = _resid(x1_p, y, 0, gt2_p, g_final[None] if last else None)
        x_s = _resid(x1_s, y, Tp, gt2_s, g_final[None] if last else None)

        for lst, val in zip(outs, (conv_p, h_p.reshape(Bp, W), s_p, conv_s, h_s, s_s)):
            lst.append(val)

    return (x_p, x_s.reshape(Bs, 1, D)) + tuple(jnp.stack(o) for o in outs)
```

```python
import functools

import jax
import jax.numpy as jnp
from jax import lax
from jax.experimental import pallas as pl
from jax.experimental.pallas import tpu as pltpu

F32 = jnp.float32
BF16 = jnp.bfloat16

RMS_EPS = 1e-6
RG_C = 8.0
GLA_TAU = 16.0
PEER_TOPK = 16
GLA_CHUNK = 128
GLA_SUB = 8
LANES = 128
SUBLANES = 8
VMEM_LIMIT = 56 * 1024 * 1024
BIG_VMEM_LIMIT = 60 * 1024 * 1024


def _params(*sem):
    return pltpu.CompilerParams(dimension_semantics=sem, vmem_limit_bytes=VMEM_LIMIT)


def _pick(n, pref, mult=SUBLANES):
    best = None
    for d in range(mult, min(n, pref) + 1, mult):
        if n % d == 0:
            best = d
    return n if best is None else best


def _softplus(y):
    return jnp.maximum(y, 0.0) + jnp.log1p(jnp.exp(-jnp.abs(y)))


def _log_sigmoid(z):
    return jnp.minimum(z, 0.0) - jnp.log1p(jnp.exp(-jnp.abs(z)))


def _row_to_col(row, n):
    r = lax.broadcasted_iota(jnp.int32, (n, n), 0)
    c = lax.broadcasted_iota(jnp.int32, (n, n), 1)
    return jnp.sum(jnp.where(r == c, jnp.broadcast_to(row, (n, n)), 0.0), axis=1, keepdims=True)


def _ada_kernel(c_ref, w_ref, b_ref, o_ref):
    a = jax.nn.silu(c_ref[...]).astype(BF16)
    o_ref[...] = jnp.dot(a, w_ref[...].astype(BF16), preferred_element_type=F32) + b_ref[...]


def _ada(c, w, b):
    R, D = c.shape
    N = w.shape[1]
    tn = _pick(N, 512, LANES)
    return pl.pallas_call(
        _ada_kernel,
        name="ada",
        grid=(N // tn,),
        in_specs=[pl.BlockSpec((R, D), lambda j: (0, 0)),
                  pl.BlockSpec((D, tn), lambda j: (0, j)),
                  pl.BlockSpec((1, tn), lambda j: (0, j))],
        out_specs=pl.BlockSpec((R, tn), lambda j: (0, j)),
        out_shape=jax.ShapeDtypeStruct((R, N), F32),
        compiler_params=_params("arbitrary"),
    )(c, w, b)


def _norm_mod_gate_kernel(x_ref, g_ref, sc_ref, sh_ref, w1_ref, w2_ref, b2_ref, o_ref, la_ref):
    x = x_ref[0]
    y = x * lax.rsqrt(jnp.mean(x * x, axis=-1, keepdims=True) + RMS_EPS) * g_ref[...]
    hn = (y * (1.0 + sc_ref[0]) + sh_ref[0]).astype(o_ref.dtype)
    o_ref[0] = hn
    a = lax.dot_general(hn, w1_ref[...].astype(BF16), (((1,), (1,)), ((), ())),
                        preferred_element_type=F32)
    z = jnp.dot(a.astype(BF16), w2_ref[...].astype(BF16), preferred_element_type=F32) + b2_ref[...]
    la_ref[0] = _log_sigmoid(z) / GLA_TAU


def _mod_spec(m, tl, D):
    if m.shape[1] == 1:
        return pl.BlockSpec((1, 1, D), lambda b, l: (b, 0, 0))
    return pl.BlockSpec((1, tl, D), lambda b, l: (b, l, 0))


def _norm_mod_gate(x, g, sc, sh, w_t, row0, w2, b2):
    B, L, D = x.shape
    R, N = w2.shape
    assert row0 % R == 0 and R % SUBLANES == 0
    tl = _pick(L, 256)
    const = lambda b, l: (0, 0)
    return pl.pallas_call(
        _norm_mod_gate_kernel,
        name="norm_mod_gate",
        grid=(B, L // tl),
        in_specs=[pl.BlockSpec((1, tl, D), lambda b, l: (b, l, 0)),
                  pl.BlockSpec((1, D), const),
                  _mod_spec(sc, tl, D), _mod_spec(sh, tl, D),
                  pl.BlockSpec((R, D), lambda b, l: (row0 // R, 0)),
                  pl.BlockSpec((R, N), const),
                  pl.BlockSpec((1, N), const)],
        out_specs=[pl.BlockSpec((1, tl, D), lambda b, l: (b, l, 0)),
                   pl.BlockSpec((1, tl, N), lambda b, l: (b, l, 0))],
        out_shape=[jax.ShapeDtypeStruct((B, L, D), BF16), jax.ShapeDtypeStruct((B, L, N), F32)],
        compiler_params=_params("arbitrary", "arbitrary"),
    )(x, g, sc, sh, w_t, w2, b2)


def _norm_mod_cat_kernel(xp_ref, xs_ref, g_ref, scp_ref, shp_ref, scs_ref, shs_ref, o_ref, *, n_p):
    i = pl.program_id(0)

    def f(x, sc, sh):
        y = x * lax.rsqrt(jnp.mean(x * x, axis=-1, keepdims=True) + RMS_EPS) * g_ref[...]
        return (y * (1.0 + sc) + sh).astype(o_ref.dtype)

    @pl.when(i < n_p)
    def _():
        o_ref[...] = f(xp_ref[0], scp_ref[0], shp_ref[0])

    @pl.when(i >= n_p)
    def _():
        o_ref[...] = f(xs_ref[0], scs_ref[0], shs_ref[0])


def _norm_mod_cat(xp, xs, g, scp, shp, scs, shs, out_dtype):
    Bp, L, D = xp.shape
    Bs = xs.shape[1]
    tl = _pick(Bs, 256)
    assert L % tl == 0
    lb = L // tl
    n_p = Bp * lb
    n_s = Bs // tl
    pb = lambda i: jnp.minimum(i, n_p - 1)
    sb = lambda i: jnp.maximum(i - n_p, 0)
    pmod = pl.BlockSpec((1, 1, D), lambda i: (pb(i) // lb, 0, 0))
    smod = pl.BlockSpec((1, tl, D), lambda i: (0, sb(i), 0))
    return pl.pallas_call(
        functools.partial(_norm_mod_cat_kernel, n_p=n_p),
        name="norm_mod_cat",
        grid=(n_p + n_s,),
        in_specs=[pl.BlockSpec((1, tl, D), lambda i: (pb(i) // lb, pb(i) % lb, 0)),
                  pl.BlockSpec((1, tl, D), lambda i: (0, sb(i), 0)),
                  pl.BlockSpec((1, D), lambda i: (0, 0)),
                  pmod, pmod, smod, smod],
        out_specs=pl.BlockSpec((tl, D), lambda i: (i, 0)),
        out_shape=jax.ShapeDtypeStruct((Bp * L + Bs, D), out_dtype),
        compiler_params=_params("arbitrary"),
    )(xp, xs, g, scp, shp, scs, shs)


def _mm_kernel(x_ref, w_ref, o_ref, wb, *, w_rows_are_outputs):
    @pl.when(pl.program_id(1) == 0)
    def _():
        wb[...] = w_ref[...].astype(BF16)

    dims = (((1,), (1,)), ((), ())) if w_rows_are_outputs else (((1,), (0,)), ((), ()))
    o_ref[...] = lax.dot_general(x_ref[...], wb[...], dims, preferred_element_type=F32)


def _matmul(x, w, N=None, *, w_rows_are_outputs=False, tm_pref=512, tn_pref=1024,
            vmem_limit=VMEM_LIMIT):
    M, K = x.shape
    n_all = w.shape[0] if w_rows_are_outputs else w.shape[1]
    N = n_all if N is None else N
    tm = _pick(M, tm_pref, 16)
    tn = _pick(N, tn_pref, LANES)
    wblk = (tn, K) if w_rows_are_outputs else (K, tn)
    wmap = (lambda j, i: (j, 0)) if w_rows_are_outputs else (lambda j, i: (0, j))
    return pl.pallas_call(
        functools.partial(_mm_kernel, w_rows_are_outputs=w_rows_are_outputs),
        name="matmul",
        grid=(N // tn, M // tm),
        in_specs=[pl.BlockSpec((tm, K), lambda j, i: (i, 0)), pl.BlockSpec(wblk, wmap)],
        out_specs=pl.BlockSpec((tm, tn), lambda j, i: (i, j)),
        out_shape=jax.ShapeDtypeStruct((M, N), F32),
        scratch_shapes=[pltpu.VMEM(wblk, BF16)],
        compiler_params=pltpu.CompilerParams(dimension_semantics=("arbitrary", "arbitrary"),
                                             vmem_limit_bytes=vmem_limit),
    )(x, w)


def _lru_gates(xc, wa_ref, ba, wx_ref, bx, lam):
    nb, bd = wa_ref.shape[0], wa_ref.shape[1]
    xcb = xc.astype(BF16)
    ra, ri = [], []
    for n in range(nb):
        xs = xcb[:, n * bd:(n + 1) * bd]
        ra.append(jnp.dot(xs, wa_ref[n], preferred_element_type=F32))
        ri.append(jnp.dot(xs, wx_ref[n], preferred_element_type=F32))
    r = jax.nn.sigmoid(jnp.concatenate(ra, axis=1) + ba)
    i = jax.nn.sigmoid(jnp.concatenate(ri, axis=1) + bx)
    log_a = -RG_C * r * _softplus(-lam)
    a = jnp.exp(log_a)
    u = jnp.sqrt(-jnp.tanh(log_a) * (a * a + 1.0)) * i * xc
    return a, u


def _lru_prompt_kernel(xb_ref, gb_ref, tail_ref, h0_ref, cw_ref, cb_ref, wa_ref, ba_ref, wx_ref,
                       bx_ref, lam_ref, y_ref, hl_ref, conv_ref, buf, hcar):
    l = pl.program_id(2)
    nl = pl.num_programs(2)
    tl, tw = xb_ref.shape[1], xb_ref.shape[2]
    nt = tail_ref.shape[1]
    base = SUBLANES

    @pl.when(l == 0)
    def _():
        buf[base - nt:base, :] = tail_ref[0]
        hcar[...] = h0_ref[0]

    xb = xb_ref[0]
    buf[base:base + tl, :] = xb
    cw = cw_ref[...]
    xc = cb_ref[...] + xb * cw[nt:nt + 1]
    for j in range(nt):
        xc = xc + buf[base - nt + j:base - nt + j + tl, :] * cw[j:j + 1]
    new_tail = buf[base + tl - nt:base + tl, :]

    a, u = _lru_gates(xc, wa_ref, ba_ref[...], wx_ref, bx_ref[...], lam_ref[...])

    ng = tl // SUBLANES
    A = a.reshape(ng, SUBLANES, tw)
    Bv = u.reshape(ng, SUBLANES, tw)
    row = lax.broadcasted_iota(jnp.int32, (1, SUBLANES, tw), 1)
    s = 1
    while s < SUBLANES:
        As = pltpu.roll(A, s, axis=1)
        Bs = pltpu.roll(Bv, s, axis=1)
        m = row >= s
        Bv = jnp.where(m, A * Bs + Bv, Bv)
        A = jnp.where(m, A * As, A)
        s *= 2
    h = hcar[...]
    outs = []
    for g in range(ng):
        hg = A[g] * h + Bv[g]
        outs.append(hg)
        h = hg[SUBLANES - 1:SUBLANES]
    hseq = jnp.concatenate(outs, axis=0)
    hcar[...] = h
    buf[base - nt:base, :] = new_tail
    y_ref[0] = (hseq * jax.nn.gelu(gb_ref[0])).astype(y_ref.dtype)

    @pl.when(l == nl - 1)
    def _():
        hl_ref[0] = h
        conv_ref[0] = new_tail


def _lru_prompt(proj, tail, h0, cw, cb, wa, ba, wx, bx, lam, W):
    B, L, _ = proj.shape
    nt = tail.shape[1]
    bd = wa.shape[1]
    tw = _pick(W, 512, bd)
    tl = _pick(L, 512)
    nwb = W // tw
    col = lambda b, j, l: (0, j)
    return pl.pallas_call(
        _lru_prompt_kernel,
        name="lru_prompt",
        grid=(B, nwb, L // tl),
        in_specs=[pl.BlockSpec((1, tl, tw), lambda b, j, l: (b, l, j)),
                  pl.BlockSpec((1, tl, tw), lambda b, j, l: (b, l, j + nwb)),
                  pl.BlockSpec((1, nt, tw), lambda b, j, l: (b, 0, j)),
                  pl.BlockSpec((1, 1, tw), lambda b, j, l: (b, 0, j)),
                  pl.BlockSpec((nt + 1, tw), col),
                  pl.BlockSpec((1, tw), col),
                  pl.BlockSpec((tw // bd, bd, bd), lambda b, j, l: (j, 0, 0)),
                  pl.BlockSpec((1, tw), col),
                  pl.BlockSpec((tw // bd, bd, bd), lambda b, j, l: (j, 0, 0)),
                  pl.BlockSpec((1, tw), col),
                  pl.BlockSpec((1, tw), col)],
        out_specs=[pl.BlockSpec((1, tl, tw), lambda b, j, l: (b, l, j)),
                   pl.BlockSpec((1, 1, tw), lambda b, j, l: (b, 0, j)),
                   pl.BlockSpec((1, nt, tw), lambda b, j, l: (b, 0, j))],
        out_shape=[jax.ShapeDtypeStruct((B, L, W), BF16),
                   jax.ShapeDtypeStruct((B, 1, W), F32),
                   jax.ShapeDtypeStruct((B, nt, W), F32)],
        scratch_shapes=[pltpu.VMEM((tl + SUBLANES, tw), F32), pltpu.VMEM((1, tw), F32)],
        compiler_params=_params("arbitrary", "arbitrary", "arbitrary"),
    )(proj, proj, tail, h0, cw, cb, wa, ba, wx, bx, lam)


def _lru_decode_kernel(xb_ref, gb_ref, tail_ref, h0_ref, cw_ref, cb_ref, wa_ref, ba_ref, wx_ref,
                       bx_ref, lam_ref, y_ref, h_ref):
    nt = tail_ref.shape[0]
    xb = xb_ref[...]
    cw = cw_ref[...]
    xc = cb_ref[...] + xb * cw[nt:nt + 1]
    for j in range(nt):
        xc = xc + tail_ref[j] * cw[j:j + 1]
    a, u = _lru_gates(xc, wa_ref, ba_ref[...], wx_ref, bx_ref[...], lam_ref[...])
    h = a * h0_ref[...] + u
    h_ref[...] = h
    y_ref[...] = (h * jax.nn.gelu(gb_ref[...])).astype(y_ref.dtype)


def _lru_decode(proj, tail, h0, cw, cb, wa, ba, wx, bx, lam, W):
    B = proj.shape[0]
    nt = tail.shape[0]
    bd = wa.shape[1]
    tw = _pick(W, 512, bd)
    nwb = W // tw
    col = lambda j: (0, j)
    return pl.pallas_call(
        _lru_decode_kernel,
        name="lru_decode",
        grid=(nwb,),
        in_specs=[pl.BlockSpec((B, tw), col),
                  pl.BlockSpec((B, tw), lambda j: (0, j + nwb)),
                  pl.BlockSpec((nt, B, tw), lambda j: (0, 0, j)),
                  pl.BlockSpec((B, tw), col),
                  pl.BlockSpec((nt + 1, tw), col),
                  pl.BlockSpec((1, tw), col),
                  pl.BlockSpec((tw // bd, bd, bd), lambda j: (j, 0, 0)),
                  pl.BlockSpec((1, tw), col),
                  pl.BlockSpec((tw // bd, bd, bd), lambda j: (j, 0, 0)),
                  pl.BlockSpec((1, tw), col),
                  pl.BlockSpec((1, tw), col)],
        out_specs=[pl.BlockSpec((B, tw), col), pl.BlockSpec((B, tw), col)],
        out_shape=[jax.ShapeDtypeStruct((B, W), BF16), jax.ShapeDtypeStruct((B, W), F32)],
        compiler_params=_params("arbitrary"),
    )(proj, proj, tail, h0, cw, cb, wa, ba, wx, bx, lam)


def _head_norm_gate(o, gn, r):
    on = o * lax.rsqrt(jnp.mean(o * o, axis=-1, keepdims=True) + RMS_EPS) * gn
    return on * jax.nn.silu(r)


def _gla_chunk(q, k, v, r, g, gn, S, att):
    C, DK = q.shape
    sub = min(GLA_SUB, C)
    q = q * (DK ** -0.5)
    vb = v.astype(BF16)
    rr = lax.broadcasted_iota(jnp.int32, (C, C), 0)
    cc = lax.broadcasted_iota(jnp.int32, (C, C), 1)
    b = jnp.dot((rr >= cc).astype(F32), g, precision=lax.Precision.HIGHEST,
                preferred_element_type=F32)
    b_last = b[C - 1:C]

    inter = jnp.dot((q * jnp.exp(b)).astype(BF16), S[...].astype(BF16), preferred_element_type=F32)

    lane = lax.broadcasted_iota(jnp.int32, (sub, sub), 1)
    srow = lax.broadcasted_iota(jnp.int32, (sub, sub), 0)
    for I in range(C // sub):
        lo = I * sub
        qI, kI, bI = q[lo:lo + sub], k[lo:lo + sub], b[lo:lo + sub]
        acc = jnp.zeros((sub, sub), F32)
        for s in range(sub):
            w = qI * kI[s:s + 1] * jnp.exp(jnp.minimum(bI - bI[s:s + 1], 0.0))
            acc = jnp.where(lane == s, jnp.sum(w, axis=1, keepdims=True), acc)
        att[lo:lo + sub, lo:lo + sub] = jnp.where(srow >= lane, acc, 0.0)
        if lo > 0:
            ref = b[lo - 1:lo]
            qt = (qI * jnp.exp(bI - ref)).astype(BF16)
            kt = (k[0:lo] * jnp.exp(ref - b[0:lo])).astype(BF16)
            att[lo:lo + sub, 0:lo] = lax.dot_general(qt, kt, (((1,), (1,)), ((), ())),
                                                     preferred_element_type=F32)
        if lo + sub < C:
            att[lo:lo + sub, lo + sub:C] = jnp.zeros((sub, C - lo - sub), F32)
    o = inter + jnp.dot(att[...].astype(BF16), vb, preferred_element_type=F32)

    kh = (k * jnp.exp(b_last - b)).astype(BF16)
    upd = lax.dot_general(kh, vb, (((0,), (0,)), ((), ())), preferred_element_type=F32)
    S[...] = _row_to_col(jnp.exp(b_last), DK) * S[...] + upd
    return _head_norm_gate(o, gn, r)


def _gla_prompt_kernel(q_ref, k_ref, v_ref, r_ref, la_ref, s0_ref, gn_ref, y_ref, so_ref, S, att):
    c = pl.program_id(2)
    nc = pl.num_programs(2)
    hb, DK, DV = S.shape

    @pl.when(c == 0)
    def _():
        S[...] = s0_ref[0]

    for h in range(hb):
        kk = slice(h * DK, (h + 1) * DK)
        vv = slice(h * DV, (h + 1) * DV)
        y = _gla_chunk(q_ref[0, :, kk], k_ref[0, :, kk], v_ref[0, :, vv], r_ref[0, :, vv],
                       la_ref[0, :, kk], gn_ref[:, vv], S.at[h], att.at[h])
        y_ref[0, :, vv] = y.astype(y_ref.dtype)

    @pl.when(c == nc - 1)
    def _():
        so_ref[0] = S[...]


def _gla_prompt(proj, la, s0, gn, offs):
    B, L, _ = proj.shape
    _, H, DK, DV = s0.shape
    C = _pick(L, GLA_CHUNK)
    hb = 4 if H % 4 == 0 else (2 if H % 2 == 0 else 1)
    wk, wv = hb * DK, hb * DV
    qo, ko, vo, ro = offs
    assert all(o % wk == 0 for o in (qo, ko)) and all(o % wv == 0 for o in (vo, ro))
    return pl.pallas_call(
        _gla_prompt_kernel,
        name="gla_prompt",
        grid=(B, H // hb, L // C),
        in_specs=[pl.BlockSpec((1, C, wk), lambda b, h, c: (b, c, qo // wk + h)),
                  pl.BlockSpec((1, C, wk), lambda b, h, c: (b, c, ko // wk + h)),
                  pl.BlockSpec((1, C, wv), lambda b, h, c: (b, c, vo // wv + h)),
                  pl.BlockSpec((1, C, wv), lambda b, h, c: (b, c, ro // wv + h)),
                  pl.BlockSpec((1, C, wk), lambda b, h, c: (b, c, h)),
                  pl.BlockSpec((1, hb, DK, DV), lambda b, h, c: (b, h, 0, 0)),
                  pl.BlockSpec((1, wv), lambda b, h, c: (0, h))],
        out_specs=[pl.BlockSpec((1, C, wv), lambda b, h, c: (b, c, h)),
                   pl.BlockSpec((1, hb, DK, DV), lambda b, h, c: (b, h, 0, 0))],
        out_shape=[jax.ShapeDtypeStruct((B, L, H * DV), BF16),
                   jax.ShapeDtypeStruct((B, H, DK, DV), F32)],
        scratch_shapes=[pltpu.VMEM((hb, DK, DV), F32), pltpu.VMEM((hb, C, C), F32)],
        compiler_params=_params("arbitrary", "arbitrary", "arbitrary"),
    )(proj, proj, proj, proj, la, s0, gn)


def _gla_decode_kernel(q_ref, k_ref, v_ref, r_ref, la_ref, s0_ref, gn_ref, y_ref, so_ref):
    bb, H, _, DK = q_ref.shape
    for b in range(bb):
        for h in range(H):
            q = q_ref[b, h] * (DK ** -0.5)
            S = (_row_to_col(jnp.exp(la_ref[b, h]), DK) * s0_ref[b, h]
                 + _row_to_col(k_ref[b, h], DK) * v_ref[b, h])
            so_ref[b, h] = S
            o = jnp.sum(_row_to_col(q, DK) * S, axis=0, keepdims=True)
            y_ref[b, h] = _head_norm_gate(o, gn_ref[h], r_ref[b, h]).astype(y_ref.dtype)


def _gla_decode(q, k, v, r, la, s0, gn):
    B, H, DK, DV = s0.shape
    bb = 4 if B % 4 == 0 else 1
    vec = lambda d: pl.BlockSpec((bb, H, 1, d), lambda b: (b, 0, 0, 0))
    mat = pl.BlockSpec((bb, H, DK, DV), lambda b: (b, 0, 0, 0))
    return pl.pallas_call(
        _gla_decode_kernel,
        name="gla_decode",
        grid=(B // bb,),
        in_specs=[vec(DK), vec(DK), vec(DV), vec(DV), vec(DK), mat,
                  pl.BlockSpec((H, 1, DV), lambda b: (0, 0, 0))],
        out_specs=[vec(DV), mat],
        out_shape=[jax.ShapeDtypeStruct((B, H, 1, DV), BF16),
                   jax.ShapeDtypeStruct((B, H, DK, DV), F32)],
        compiler_params=_params("arbitrary"),
    )(q, k, v, r, la, s0, gn)


def _outproj_kernel(ya_ref, yb_ref, wa_ref, wb_ref, x_ref, gt_ref, o_ref, wab, wbb):
    @pl.when((pl.program_id(1) == 0) & (pl.program_id(2) == 0))
    def _():
        wab[...] = wa_ref[...].astype(BF16)
        wbb[...] = wb_ref[...].astype(BF16)

    mix = (jnp.dot(ya_ref[0], wab[...], preferred_element_type=F32)
           + jnp.dot(yb_ref[0], wbb[...], preferred_element_type=F32))
    o_ref[0] = x_ref[0] + gt_ref[0] * mix


def _outproj(ya, yb, w, x, gt):
    B, L, D = x.shape
    Ka, Kb = ya.shape[2], yb.shape[2]
    assert Ka == Kb
    tm = _pick(L, 1024)
    tn = _pick(D, 512, LANES)
    gspec =(pl.BlockSpec((1, 1, tn), lambda j, b, l: (b, 0, j)) if gt.shape[1] == 1
             else pl.BlockSpec((1, tm, tn), lambda j, b, l: (b, l, j)))
    return pl.pallas_call(
        _outproj_kernel,
        name="outproj",
        grid=(D // tn, B, L // tm),
        in_specs=[pl.BlockSpec((1, tm, Ka), lambda j, b, l: (b, l, 0)),
                  pl.BlockSpec((1, tm, Kb), lambda j, b, l: (b, l, 0)),
                  pl.BlockSpec((Ka, tn), lambda j, b, l: (0, j)),
                  pl.BlockSpec((Kb, tn), lambda j, b, l: (1, j)),
                  pl.BlockSpec((1, tm, tn), lambda j, b, l: (b, l, j)),
                  gspec],
        out_specs=pl.BlockSpec((1, tm, tn), lambda j, b, l: (b, l, j)),
        out_shape=jax.ShapeDtypeStruct((B, L, D), F32),
        scratch_shapes=[pltpu.VMEM((Ka, tn), BF16), pltpu.VMEM((Kb, tn), BF16)],
        compiler_params=_params("arbitrary", "arbitrary", "arbitrary"),
    )(ya, yb, w, w, x, gt)


def _topk_rows(problems, ids, kk, fill):
    ss = [p[0] for p in problems]
    for j in range(kk):
        for n, (_, val_ref, idx_ref) in enumerate(problems):
            s = ss[n]
            m = jnp.max(s, axis=0, keepdims=True)
            idx = jnp.min(jnp.where(s == m, ids, fill), axis=0, keepdims=True)
            val_ref[j:j + 1, :] = m
            idx_ref[j:j + 1, :] = idx
            ss[n] = jnp.where(ids == idx, -jnp.inf, s)


def _pair_layout(K):
    segs, ids, r0 = [], [], 0
    b = 0
    while b < K and K // (b + 1) > 1:
        na = K // (b + 1)
        segs.append((r0, (0, na), (b, b + 1)))
        rows = -(-na // SUBLANES) * SUBLANES
        ids += [a * K + b for a in range(na)] + [K * K] * (rows - na)
        r0 += rows
        b += 1
    if b < K:
        nb = K - b
        segs.append((r0, (0, 1), (b, K)))
        rows = -(-nb // SUBLANES) * SUBLANES
        ids += list(range(b, K)) + [K * K] * (rows - nb)
        r0 += rows
    return segs, ids, r0


def _peer_topk_kernel(q_ref, keys_ref, pid_ref, e1_ref, e2_ref, g_ref, sv, si, cand, sc, pos,
                      *, segs):
    tT = q_ref.shape[0]
    H, _, NK, dq = keys_ref.shape
    K = PEER_TOPK
    hp = sv.shape[0]
    ids = lax.broadcasted_iota(jnp.int32, (NK, tT), 0).astype(F32)
    pids = pid_ref[...]

    def heads(i, carry):
        halves = []
        for n in range(hp):
            for p in range(2):
                h = i * hp + n
                off = pl.multiple_of((2 * h + p) * dq, dq)
                qp = q_ref[:, pl.ds(off, dq)].astype(BF16)
                s = lax.dot_general(keys_ref[h, p].astype(BF16), qp, (((1,), (1,)), ((), ())),
                                    preferred_element_type=F32)
                halves.append((s, sv.at[n, p], si.at[n, p]))
        _topk_rows(halves, ids, K, float(NK))
        pairs = []
        for n in range(hp):
            v0, v1 = sv[n, 0], sv[n, 1]
            cand[n] = jnp.full(cand.shape[1:], -jnp.inf, F32)
            for r0, (a0, a1), (b0, b1) in segs:
                rows = max(a1 - a0, b1 - b0)
                cand[n, r0:r0 + rows, :] = v0[a0:a1] + v1[b0:b1]
            pairs.append((cand[n], sc.at[n], pos.at[n]))
        _topk_rows(pairs, pids, K, float(K * K))
        for n in range(hp):
            h = i * hp + n
            pa = jnp.floor(pos[n] * (1.0 / K))
            pb = pos[n] - pa * K
            i0, i1 = si[n, 0], si[n, 1]
            e1 = jnp.zeros((K, tT), F32)
            e2 = jnp.zeros((K, tT), F32)
            for a in range(K):
                e1 = jnp.where(pa == a, i0[a:a + 1], e1)
                e2 = jnp.where(pb == a, i1[a:a + 1], e2)
            scv = sc[n]
            ex = jnp.exp(scv - jnp.max(scv, axis=0, keepdims=True))
            e1_ref[h] = e1
            e2_ref[h] = e2
            g_ref[h] = ex / jnp.sum(ex, axis=0, keepdims=True)
        return carry

    lax.fori_loop(0, H // hp, heads, 0)


def _peer_topk(q, keys):
    T = q.shape[0]
    H, _, NK, dq = keys.shape
    K = PEER_TOPK
    tT = _pick(T, LANES, LANES)
    hp = 8 if H % 8 == 0 else 1
    segs, ids, nr = _pair_layout(K)
    pids = jnp.broadcast_to(jnp.asarray(ids, F32)[:, None], (nr, tT))
    out = jax.ShapeDtypeStruct((H, K, T), F32)
    ospec = pl.BlockSpec((H, K, tT), lambda i: (0, 0, i))
    return pl.pallas_call(
        functools.partial(_peer_topk_kernel, segs=segs),
        name="peer_topk",
        grid=(T // tT,),
        in_specs=[pl.BlockSpec((tT, 2 * H * dq), lambda i: (i, 0)),
                  pl.BlockSpec((H, 2, NK, dq), lambda i: (0, 0, 0, 0)),
                  pl.BlockSpec((nr, tT), lambda i: (0, 0))],
        out_specs=[ospec, ospec, ospec],
        out_shape=[out, out, out],
        scratch_shapes=[pltpu.VMEM((hp, 2, K, tT), F32), pltpu.VMEM((hp, 2, K, tT), F32),
                        pltpu.VMEM((hp, nr, tT), F32), pltpu.VMEM((hp, K, tT), F32),
                        pltpu.VMEM((hp, K, tT), F32)],
        compiler_params=_params("arbitrary"),
    )(q, keys, pids)


def _peer_gate_kernel(e1_ref, e2_ref, g_ref, o_ref, e1t, e2t, gt, gbuf, *, unroll, pitch):
    NS, tT = e1_ref.shape
    NK = gbuf.shape[1]
    e1t[...] = e1_ref[...].T
    e2t[...] = e2_ref[...].T
    gt[...] = g_ref[...].T
    kid = lax.broadcasted_iota(jnp.int32, (NK, NS), 0).astype(F32)

    def body(i, carry):
        for j in range(unroll):
            t = i * unroll + j
            a = jnp.where(kid == e1t[pl.ds(t, 1), :], gt[pl.ds(t, 1), :], 0.0).astype(BF16)
            b = jnp.where(kid == e2t[pl.ds(t, 1), :], 1.0, 0.0).astype(BF16)
            gbuf[pl.ds(pl.multiple_of(t * pitch, SUBLANES), NK), :] = lax.dot_general(
                a, b, (((1,), (1,)), ((), ())), preferred_element_type=F32)
        return carry

    lax.fori_loop(0, tT // unroll, body, 0)
    for i1 in range(NK):
        o_ref[:, i1 * NK:(i1 + 1) * NK] = gbuf[pl.ds(i1, tT, stride=pitch), :].astype(o_ref.dtype)


def _peer_gates(e1, e2, g, NK):
    NS, T = e1.shape
    tT = _pick(T, 128, LANES)
    unroll = 32 if tT % 32 == 0 else 1
    pitch = NK + SUBLANES
    ispec = pl.BlockSpec((NS, tT), lambda i: (0, i))
    return pl.pallas_call(
        functools.partial(_peer_gate_kernel, unroll=unroll, pitch=pitch),
        name="peer_gate",
        grid=(T // tT,),
        in_specs=[ispec, ispec, ispec],
        out_specs=pl.BlockSpec((tT, NK * NK), lambda i: (i, 0)),
        out_shape=jax.ShapeDtypeStruct((T, NK * NK), BF16),
        scratch_shapes=[pltpu.VMEM((tT, NS), F32)] * 3 + [pltpu.VMEM((tT * pitch, NK), F32)],
        compiler_params=_params("arbitrary"),
    )(e1, e2, g)


def _peer_dense_kernel(x_ref, u_ref, v_ref, g_ref, o_ref, *, nsplit):
    e = pl.program_id(1)

    @pl.when(e == 0)
    def _():
        o_ref[...] = jnp.zeros(o_ref.shape, o_ref.dtype)

    rs = x_ref.shape[0] // nsplit
    for r in range(nsplit):
        rows = slice(r * rs, (r + 1) * rs)
        act = lax.dot_general(x_ref[rows, :], u_ref[...], (((1,), (1,)), ((), ())),
                              preferred_element_type=F32)
        p = (g_ref[rows, :].astype(F32) * jax.nn.gelu(act)).astype(BF16)
        o_ref[rows, :] += jnp.dot(p, v_ref[...], preferred_element_type=F32)


def _peer_dense(x, u, v, G):
    T, D = x.shape
    E = u.shape[0]
    tT = _pick(T, 640, 16)
    tE = _pick(E, 512, LANES)
    nsplit = 2 if tT % 32 == 0 else 1
    return pl.pallas_call(
        functools.partial(_peer_dense_kernel, nsplit=nsplit),
        name="peer_dense",
        grid=(T // tT, E // tE),
        in_specs=[pl.BlockSpec((tT, D), lambda i, e: (i, 0)),
                  pl.BlockSpec((tE, D), lambda i, e: (e, 0)),
                  pl.BlockSpec((tE, D), lambda i, e: (e, 0)),
                  pl.BlockSpec((tT, tE), lambda i, e: (i, e))],
        out_specs=pl.BlockSpec((tT, D), lambda i, e: (i, 0)),
        out_shape=jax.ShapeDtypeStruct((T, D), F32),
        compiler_params=_params("arbitrary", "arbitrary"),
    )(x, u, v, G)


def _row_groups(rows):
    for n in (5, 4, 2):
        if rows % (16 * n) == 0:
            return n
    return 1


def _peer_act_kernel(x_ref, u_ref, g_ref, p_ref, *, nsplit):
    ub = u_ref[...].astype(BF16)
    rs = x_ref.shape[0] // nsplit
    for r in range(nsplit):
        rows = slice(r * rs, (r + 1) * rs)
        act = lax.dot_general(x_ref[rows, :], ub, (((1,), (1,)), ((), ())),
                              preferred_element_type=F32)
        p_ref[rows, :] = (g_ref[rows, :].astype(F32) * jax.nn.gelu(act)).astype(p_ref.dtype)


def _peer_act(x, u, G):
    T, D = x.shape
    E = u.shape[0]
    tT = _pick(T, 2080, 16)
    tE = _pick(E, 512, LANES)
    nsplit = _row_groups(tT)
    return pl.pallas_call(
        functools.partial(_peer_act_kernel, nsplit=nsplit),
        name="peer_act",
        grid=(T // tT, E // tE),
        in_specs=[pl.BlockSpec((tT, D), lambda i, e: (i, 0), pipeline_mode=pl.Buffered(1)),
                  pl.BlockSpec((tE, D), lambda i, e: (e, 0)),
                  pl.BlockSpec((tT, tE), lambda i, e: (i, e))],
        out_specs=pl.BlockSpec((tT, tE), lambda i, e: (i, e)),
        out_shape=jax.ShapeDtypeStruct((T, E), BF16),
        compiler_params=_params("arbitrary", "arbitrary"),
    )(x, u, G)


def _peer_out_kernel(p_ref, v_ref, o_ref, *, nsplit):
    e = pl.program_id(1)

    @pl.when(e == 0)
    def _():
        o_ref[...] = jnp.zeros(o_ref.shape, o_ref.dtype)

    vb = v_ref[...].astype(BF16)
    rs = p_ref.shape[0] // nsplit
    for r in range(nsplit):
        rows = slice(r * rs, (r + 1) * rs)
        o_ref[rows, :] += jnp.dot(p_ref[rows, :], vb, preferred_element_type=F32)


def _peer_out(p, v):
    T, E = p.shape
    D = v.shape[1]
    tT = _pick(T, 1664, 16)
    tE = _pick(E, 512, LANES)
    nsplit = 8 if tT % 128 == 0 else _row_groups(tT)
    return pl.pallas_call(
        functools.partial(_peer_out_kernel, nsplit=nsplit),
        name="peer_out",
        grid=(T // tT, E // tE),
        in_specs=[pl.BlockSpec((tT, tE), lambda i, e: (i, e)),
                  pl.BlockSpec((tE, D), lambda i, e: (e, 0))],
        out_specs=pl.BlockSpec((tT, D), lambda i, e: (i, 0), pipeline_mode=pl.Buffered(1)),
        out_shape=jax.ShapeDtypeStruct((T, D), F32),
        compiler_params=pltpu.CompilerParams(dimension_semantics=("arbitrary", "arbitrary"),
                                             vmem_limit_bytes=BIG_VMEM_LIMIT),
    )(p, v)


def _resid_kernel(x_ref, y_ref, gt_ref, o_ref):
    o_ref[0] = x_ref[0] + gt_ref[0] * y_ref[...]


def _resid_norm_kernel(x_ref, y_ref, gt_ref, g_ref, o_ref):
    x = x_ref[0] + gt_ref[0] * y_ref[...]
    o_ref[0] = x * lax.rsqrt(jnp.mean(x * x, axis=-1, keepdims=True) + RMS_EPS) * g_ref[...]


def _resid(x, y, row0, gt, g=None):
    B, L, D = x.shape
    tl = _pick(L, 256)
    assert row0 % tl == 0
    lb = L // tl
    blk = pl.BlockSpec((1, tl, D), lambda b, l: (b, l, 0))
    in_specs = [blk, pl.BlockSpec((tl, D), lambda b, l: (row0 // tl + b * lb + l, 0)),
                _mod_spec(gt, tl, D)]
    args = [x, y, gt]
    if g is not None:
        in_specs.append(pl.BlockSpec((1, D), lambda b, l: (0, 0)))
        args.append(g)
    return pl.pallas_call(
        _resid_kernel if g is None else _resid_norm_kernel,
        name="resid",
        grid=(B, L // tl),
        in_specs=in_specs,
        out_specs=blk,
        out_shape=jax.ShapeDtypeStruct((B, L, D), F32),
        compiler_params=_params("arbitrary", "arbitrary"),
    )(*args)


def kernel(x_prompt, x_sample, state_conv, state_lru_h, state_gla, c_prompt, c_sample, w_ada, b_ada, g_mix, w_in, conv_w, conv_b, lru_w_a, lru_b_a, lru_w_x, lru_b_x, lru_lambda, gla_w_alpha, gla_b_alpha, gla_g_norm, w_out, g_ffn, peer_w_q, peer_sub_keys, peer_u, peer_v, g_final):
    depth = w_ada.shape[0]
    Bp, L, D = x_prompt.shape
    Bs = x_sample.shape[0]
    assert x_sample.shape[1] == 1
    W = conv_w.shape[2]
    nt = conv_w.shape[1] - 1
    _, _, H, DK, DV = state_gla.shape
    rank = gla_w_alpha.shape[1]
    PH, _, NK, _ = peer_sub_keys.shape[1:]
    n_main = 2 * W + 2 * H * DK + 2 * H * DV
    assert w_in.shape[2] == n_main + rank and rank <= LANES
    offs = (2 * W, 2 * W + H * DK, 2 * W + 2 * H * DK, 2 * W + 2 * H * DK + H * DV)
    Tp = Bp * L

    R = Bp + Bs
    Rp = -(-R // SUBLANES) * SUBLANES
    c_all = jnp.pad(jnp.concatenate([c_prompt, c_sample], axis=0), ((0, Rp - R), (0, 0)))

    x_p = x_prompt
    x_s = x_sample.reshape(1, Bs, D)
    outs = [[] for _ in range(6)]
    for l in range(depth):
        ada = _ada(c_all, w_ada[l], b_ada[l][None])
        mods_p = [ada[:Bp, i * D:(i + 1) * D][:, None, :] for i in range(6)]
        mods_s = [ada[Bp:R, i * D:(i + 1) * D][None] for i in range(6)]

        w_in_t = jnp.transpose(w_in[l])
        in_proj = functools.partial(_matmul, w=w_in_t, N=n_main, w_rows_are_outputs=True,
                                    vmem_limit=BIG_VMEM_LIMIT)
        norm_gate = functools.partial(_norm_mod_gate, g=g_mix[l][None], w_t=w_in_t, row0=n_main,
                                      w2=gla_w_alpha[l], b2=gla_b_alpha[l][None])
        wa = lru_w_a[l].astype(BF16)
        wx = lru_w_x[l].astype(BF16)
        lru_vecs = (conv_w[l], conv_b[l][None], wa, lru_b_a[l][None], wx, lru_b_x[l][None],
                    lru_lambda[l][None])
        gn = gla_g_norm[l]

        sh1, sc1, gt1, sh2, sc2, gt2 = mods_p
        hn, la = norm_gate(x_p, sc=sc1, sh=sh1)
        proj = in_proj(hn.reshape(Tp, D)).reshape(Bp, L, n_main)
        y_lru, h_p, conv_p = _lru_prompt(
            proj, jnp.zeros((Bp, nt, W), F32), jnp.zeros((Bp, 1, W), F32), *lru_vecs, W)
        y_gla, s_p = _gla_prompt(proj, la, jnp.zeros((Bp, H, DK, DV), F32), gn[None], offs)
        x1_p = _outproj(y_lru, y_gla, w_out[l], x_p, gt1)
        sc2_p, sh2_p, gt2_p = sc2, sh2, gt2

        sh1, sc1, gt1, sh2, sc2, gt2 = mods_s
        hn, la = norm_gate(x_s, sc=sc1, sh=sh1)
        proj = in_proj(hn.reshape(Bs, D))
        la = la.reshape(Bs, H * DK)
        tail = jnp.transpose(state_conv[l], (1, 0, 2))
        y_lru, h_s = _lru_decode(proj, tail, state_lru_h[l], *lru_vecs, W)
        conv_s = jnp.concatenate([state_conv[l][:, 1:], proj[:, None, :W]], axis=1)
        qo, ko, vo, ro = offs
        heads = lambda a, d: a.reshape(Bs, H, 1, d)
        y_gla, s_s = _gla_decode(
            heads(proj[:, qo:ko], DK), heads(proj[:, ko:vo], DK), heads(proj[:, vo:ro], DV),
            heads(proj[:, ro:n_main], DV), heads(la, DK), state_gla[l], gn.reshape(H, 1, DV))
        x1_s = _outproj(y_lru[None], y_gla.reshape(1, Bs, H * DV), w_out[l], x_s, gt1)
        gt2_s = gt2

        hn2 = _norm_mod_cat(x1_p, x1_s, g_ffn[l][None], sc2_p, sh2_p, sc2, sh2, BF16)
        T = Tp + Bs
        q = _matmul(hn2, peer_w_q[l])
        e1, e2, gate = _peer_topk(q, peer_sub_keys[l])
        G = _peer_gates(e1.reshape(PH * PEER_TOPK, T), e2.reshape(PH * PEER_TOPK, T),
                        gate.reshape(PH * PEER_TOPK, T), NK)
        y = _peer_out(_peer_act(hn2, peer_u[l], G), peer_v[l])
        last = l == depth - 1
        x_p = _resid(x1_p, y, 0, gt2_p, g_final[None] if last else None)
        x_s = _resid(x1_s, y, Tp, gt2_s, g_final[None] if last else None)

        for lst, val in zip(outs, (conv_p, h_p.reshape(Bp, W), s_p, conv_s, h_s, s_s)):
            lst.append(val)

    return (x_p, x_s.reshape(Bs, 1, D)) + tuple(jnp.stack(o) for o in outs)
```

```python
import functools

import jax
import jax.numpy as jnp
from jax import lax
from jax.experimental import pallas as pl
from jax.experimental.pallas import tpu as pltpu

F32 = jnp.float32
BF16 = jnp.bfloat16

RMS_EPS = 1e-6
RG_C = 8.0
GLA_TAU = 16.0
PEER_TOPK = 16
GLA_CHUNK = 128
GLA_SUB = 8
LANES = 128
SUBLANES = 8
VMEM_LIMIT = 56 * 1024 * 1024
BIG_VMEM_LIMIT = 60 * 1024 * 1024


def _params(*sem):
    return pltpu.CompilerParams(dimension_semantics=sem, vmem_limit_bytes=VMEM_LIMIT)


def _pick(n, pref, mult=SUBLANES):
    best = None
    for d in range(mult, min(n, pref) + 1, mult):
        if n % d == 0:
            best = d
    return n if best is None else best


def _softplus(y):
    return jnp.maximum(y, 0.0) + jnp.log1p(jnp.exp(-jnp.abs(y)))


def _log_sigmoid(z):
    return jnp.minimum(z, 0.0) - jnp.log1p(jnp.exp(-jnp.abs(z)))


def _row_to_col(row, n):
    r = lax.broadcasted_iota(jnp.int32, (n, n), 0)
    c = lax.broadcasted_iota(jnp.int32, (n, n), 1)
    return jnp.sum(jnp.where(r == c, jnp.broadcast_to(row, (n, n)), 0.0), axis=1, keepdims=True)


def _ada_kernel(c_ref, w_ref, b_ref, o_ref):
    a = jax.nn.silu(c_ref[...]).astype(BF16)
    o_ref[...] = jnp.dot(a, w_ref[...].astype(BF16), preferred_element_type=F32) + b_ref[...]


def _ada(c, w, b):
    R, D = c.shape
    N = w.shape[1]
    tn = _pick(N, 512, LANES)
    return pl.pallas_call(
        _ada_kernel,
        name="ada",
        grid=(N // tn,),
        in_specs=[pl.BlockSpec((R, D), lambda j: (0, 0)),
                  pl.BlockSpec((D, tn), lambda j: (0, j)),
                  pl.BlockSpec((1, tn), lambda j: (0, j))],
        out_specs=pl.BlockSpec((R, tn), lambda j: (0, j)),
        out_shape=jax.ShapeDtypeStruct((R, N), F32),
        compiler_params=_params("arbitrary"),
    )(c, w, b)


def _norm_mod_gate_kernel(x_ref, g_ref, sc_ref, sh_ref, w1_ref, w2_ref, b2_ref, o_ref, la_ref):
    x = x_ref[0]
    y = x * lax.rsqrt(jnp.mean(x * x, axis=-1, keepdims=True) + RMS_EPS) * g_ref[...]
    hn = (y * (1.0 + sc_ref[0]) + sh_ref[0]).astype(o_ref.dtype)
    o_ref[0] = hn
    a = lax.dot_general(hn, w1_ref[...].astype(BF16), (((1,), (1,)), ((), ())),
                        preferred_element_type=F32)
    z = jnp.dot(a.astype(BF16), w2_ref[...].astype(BF16), preferred_element_type=F32) + b2_ref[...]
    la_ref[0] = _log_sigmoid(z) / GLA_TAU


def _mod_spec(m, tl, D):
    if m.shape[1] == 1:
        return pl.BlockSpec((1, 1, D), lambda b, l: (b, 0, 0))
    return pl.BlockSpec((1, tl, D), lambda b, l: (b, l, 0))


def _norm_mod_gate(x, g, sc, sh, w_t, row0, w2, b2):
    B, L, D = x.shape
    R, N = w2.shape
    assert row0 % R == 0 and R % SUBLANES == 0
    tl = _pick(L, 256)
    const = lambda b, l: (0, 0)
    return pl.pallas_call(
        _norm_mod_gate_kernel,
        name="norm_mod_gate",
        grid=(B, L // tl),
        in_specs=[pl.BlockSpec((1, tl, D), lambda b, l: (b, l, 0)),
                  pl.BlockSpec((1, D), const),
                  _mod_spec(sc, tl, D), _mod_spec(sh, tl, D),
                  pl.BlockSpec((R, D), lambda b, l: (row0 // R, 0)),
                  pl.BlockSpec((R, N), const),
                  pl.BlockSpec((1, N), const)],
        out_specs=[pl.BlockSpec((1, tl, D), lambda b, l: (b, l, 0)),
                   pl.BlockSpec((1, tl, N), lambda b, l: (b, l, 0))],
        out_shape=[jax.ShapeDtypeStruct((B, L, D), BF16), jax.ShapeDtypeStruct((B, L, N), F32)],
        compiler_params=_params("arbitrary", "arbitrary"),
    )(x, g, sc, sh, w_t, w2, b2)


def _norm_mod_cat_kernel(xp_ref, xs_ref, g_ref, scp_ref, shp_ref, scs_ref, shs_ref, o_ref, *, n_p):
    i = pl.program_id(0)

    def f(x, sc, sh):
        y = x * lax.rsqrt(jnp.mean(x * x, axis=-1, keepdims=True) + RMS_EPS) * g_ref[...]
        return (y * (1.0 + sc) + sh).astype(o_ref.dtype)

    @pl.when(i < n_p)
    def _():
        o_ref[...] = f(xp_ref[0], scp_ref[0], shp_ref[0])

    @pl.when(i >= n_p)
    def _():
        o_ref[...] = f(xs_ref[0], scs_ref[0], shs_ref[0])


def _norm_mod_cat(xp, xs, g, scp, shp, scs, shs, out_dtype):
    Bp, L, D = xp.shape
    Bs = xs.shape[1]
    tl = _pick(Bs, 256)
    assert L % tl == 0
    lb = L // tl
    n_p = Bp * lb
    n_s = Bs // tl
    pb = lambda i: jnp.minimum(i, n_p - 1)
    sb = lambda i: jnp.maximum(i - n_p, 0)
    pmod = pl.BlockSpec((1, 1, D), lambda i: (pb(i) // lb, 0, 0))
    smod = pl.BlockSpec((1, tl, D), lambda i: (0, sb(i), 0))
    return pl.pallas_call(
        functools.partial(_norm_mod_cat_kernel, n_p=n_p),
        name="norm_mod_cat",
        grid=(n_p + n_s,),
        in_specs=[pl.BlockSpec((1, tl, D), lambda i: (pb(i) // lb, pb(i) % lb, 0)),
                  pl.BlockSpec((1, tl, D), lambda i: (0, sb(i), 0)),
                  pl.BlockSpec((1, D), lambda i: (0, 0)),
                  pmod, pmod, smod, smod],
        out_specs=pl.BlockSpec((tl, D), lambda i: (i, 0)),
        out_shape=jax.ShapeDtypeStruct((Bp * L + Bs, D), out_dtype),
        compiler_params=_params("arbitrary"),
    )(xp, xs, g, scp, shp, scs, shs)


def _mm_kernel(x_ref, w_ref, o_ref, wb, *, w_rows_are_outputs):
    @pl.when(pl.program_id(1) == 0)
    def _():
        wb[...] = w_ref[...].astype(BF16)

    dims = (((1,), (1,)), ((), ())) if w_rows_are_outputs else (((1,), (0,)), ((), ()))
    o_ref[...] = lax.dot_general(x_ref[...], wb[...], dims, preferred_element_type=F32)


def _matmul(x, w, N=None, *, w_rows_are_outputs=False, tm_pref=512, tn_pref=1024,
            vmem_limit=VMEM_LIMIT):
    M, K = x.shape
    n_all = w.shape[0] if w_rows_are_outputs else w.shape[1]
    N = n_all if N is None else N
    tm = _pick(M, tm_pref, 16)
    tn = _pick(N, tn_pref, LANES)
    wblk = (tn, K) if w_rows_are_outputs else (K, tn)
    wmap = (lambda j, i: (j, 0)) if w_rows_are_outputs else (lambda j, i: (0, j))
    return pl.pallas_call(
        functools.partial(_mm_kernel, w_rows_are_outputs=w_rows_are_outputs),
        name="matmul",
        grid=(N // tn, M // tm),
        in_specs=[pl.BlockSpec((tm, K), lambda j, i: (i, 0)), pl.BlockSpec(wblk, wmap)],
        out_specs=pl.BlockSpec((tm, tn), lambda j, i: (i, j)),
        out_shape=jax.ShapeDtypeStruct((M, N), F32),
        scratch_shapes=[pltpu.VMEM(wblk, BF16)],
        compiler_params=pltpu.CompilerParams(dimension_semantics=("arbitrary", "arbitrary"),
                                             vmem_limit_bytes=vmem_limit),
    )(x, w)


def _lru_gates(xc, wa_ref, ba, wx_ref, bx, lam):
    nb, bd = wa_ref.shape[0], wa_ref.shape[1]
    xcb = xc.astype(BF16)
    ra, ri = [], []
    for n in range(nb):
        xs = xcb[:, n * bd:(n + 1) * bd]
        ra.append(jnp.dot(xs, wa_ref[n], preferred_element_type=F32))
        ri.append(jnp.dot(xs, wx_ref[n], preferred_element_type=F32))
    r = jax.nn.sigmoid(jnp.concatenate(ra, axis=1) + ba)
    i = jax.nn.sigmoid(jnp.concatenate(ri, axis=1) + bx)
    log_a = -RG_C * r * _softplus(-lam)
    a = jnp.exp(log_a)
    u = jnp.sqrt(-jnp.tanh(log_a) * (a * a + 1.0)) * i * xc
    return a, u


def _lru_prompt_kernel(xb_ref, gb_ref, tail_ref, h0_ref, cw_ref, cb_ref, wa_ref, ba_ref, wx_ref,
                       bx_ref, lam_ref, y_ref, hl_ref, conv_ref, buf, hcar):
    l = pl.program_id(2)
    nl = pl.num_programs(2)
    tl, tw = xb_ref.shape[1], xb_ref.shape[2]
    nt = tail_ref.shape[1]
    base = SUBLANES

    @pl.when(l == 0)
    def _():
        buf[base - nt:base, :] = tail_ref[0]
        hcar[...] = h0_ref[0]

    xb = xb_ref[0]
    buf[base:base + tl, :] = xb
    cw = cw_ref[...]
    xc = cb_ref[...] + xb * cw[nt:nt + 1]
    for j in range(nt):
        xc = xc + buf[base - nt + j:base - nt + j + tl, :] * cw[j:j + 1]
    new_tail = buf[base + tl - nt:base + tl, :]

    a, u = _lru_gates(xc, wa_ref, ba_ref[...], wx_ref, bx_ref[...], lam_ref[...])

    ng = tl // SUBLANES
    A = a.reshape(ng, SUBLANES, tw)
    Bv = u.reshape(ng, SUBLANES, tw)
    row = lax.broadcasted_iota(jnp.int32, (1, SUBLANES, tw), 1)
    s = 1
    while s < SUBLANES:
        As = pltpu.roll(A, s, axis=1)
        Bs = pltpu.roll(Bv, s, axis=1)
        m = row >= s
        Bv = jnp.where(m, A * Bs + Bv, Bv)
        A = jnp.where(m, A * As, A)
        s *= 2
    h = hcar[...]
    outs = []
    for g in range(ng):
        hg = A[g] * h + Bv[g]
        outs.append(hg)
        h = hg[SUBLANES - 1:SUBLANES]
    hseq = jnp.concatenate(outs, axis=0)
    hcar[...] = h
    buf[base - nt:base, :] = new_tail
    y_ref[0] = (hseq * jax.nn.gelu(gb_ref[0])).astype(y_ref.dtype)

    @pl.when(l == nl - 1)
    def _():
        hl_ref[0] = h
        conv_ref[0] = new_tail


def _lru_prompt(proj, tail, h0, cw, cb, wa, ba, wx, bx, lam, W):
    B, L, _ = proj.shape
    nt = tail.shape[1]
    bd = wa.shape[1]
    tw = _pick(W, 512, bd)
    tl = _pick(L, 512)
    nwb = W // tw
    col = lambda b, j, l: (0, j)
    return pl.pallas_call(
        _lru_prompt_kernel,
        name="lru_prompt",
        grid=(B, nwb, L // tl),
        in_specs=[pl.BlockSpec((1, tl, tw), lambda b, j, l: (b, l, j)),
                  pl.BlockSpec((1, tl, tw), lambda b, j, l: (b, l, j + nwb)),
                  pl.BlockSpec((1, nt, tw), lambda b, j, l: (b, 0, j)),
                  pl.BlockSpec((1, 1, tw), lambda b, j, l: (b, 0, j)),
                  pl.BlockSpec((nt + 1, tw), col),
                  pl.BlockSpec((1, tw), col),
                  pl.BlockSpec((tw // bd, bd, bd), lambda b, j, l: (j, 0, 0)),
                  pl.BlockSpec((1, tw), col),
                  pl.BlockSpec((tw // bd, bd, bd), lambda b, j, l: (j, 0, 0)),
                  pl.BlockSpec((1, tw), col),
                  pl.BlockSpec((1, tw), col)],
        out_specs=[pl.BlockSpec((1, tl, tw), lambda b, j, l: (b, l, j)),
                   pl.BlockSpec((1, 1, tw), lambda b, j, l: (b, 0, j)),
                   pl.BlockSpec((1, nt, tw), lambda b, j, l: (b, 0, j))],
        out_shape=[jax.ShapeDtypeStruct((B, L, W), BF16),
                   jax.ShapeDtypeStruct((B, 1, W), F32),
                   jax.ShapeDtypeStruct((B, nt, W), F32)],
        scratch_shapes=[pltpu.VMEM((tl + SUBLANES, tw), F32), pltpu.VMEM((1, tw), F32)],
        compiler_params=_params("arbitrary", "arbitrary", "arbitrary"),
    )(proj, proj, tail, h0, cw, cb, wa, ba, wx, bx, lam)


def _lru_decode_kernel(xb_ref, gb_ref, tail_ref, h0_ref, cw_ref, cb_ref, wa_ref, ba_ref, wx_ref,
                       bx_ref, lam_ref, y_ref, h_ref):
    nt = tail_ref.shape[0]
    xb = xb_ref[...]
    cw = cw_ref[...]
    xc = cb_ref[...] + xb * cw[nt:nt + 1]
    for j in range(nt):
        xc = xc + tail_ref[j] * cw[j:j + 1]
    a, u = _lru_gates(xc, wa_ref, ba_ref[...], wx_ref, bx_ref[...], lam_ref[...])
    h = a * h0_ref[...] + u
    h_ref[...] = h
    y_ref[...] = (h * jax.nn.gelu(gb_ref[...])).astype(y_ref.dtype)


def _lru_decode(proj, tail, h0, cw, cb, wa, ba, wx, bx, lam, W):
    B = proj.shape[0]
    nt = tail.shape[0]
    bd = wa.shape[1]
    tw = _pick(W, 512, bd)
    nwb = W // tw
    col = lambda j: (0, j)
    return pl.pallas_call(
        _lru_decode_kernel,
        name="lru_decode",
        grid=(nwb,),
        in_specs=[pl.BlockSpec((B, tw), col),
                  pl.BlockSpec((B, tw), lambda j: (0, j + nwb)),
                  pl.BlockSpec((nt, B, tw), lambda j: (0, 0, j)),
                  pl.BlockSpec((B, tw), col),
                  pl.BlockSpec((nt + 1, tw), col),
                  pl.BlockSpec((1, tw), col),
                  pl.BlockSpec((tw // bd, bd, bd), lambda j: (j, 0, 0)),
                  pl.BlockSpec((1, tw), col),
                  pl.BlockSpec((tw // bd, bd, bd), lambda j: (j, 0, 0)),
                  pl.BlockSpec((1, tw), col),
                  pl.BlockSpec((1, tw), col)],
        out_specs=[pl.BlockSpec((B, tw), col), pl.BlockSpec((B, tw), col)],
        out_shape=[jax.ShapeDtypeStruct((B, W), BF16), jax.ShapeDtypeStruct((B, W), F32)],
        compiler_params=_params("arbitrary"),
    )(proj, proj, tail, h0, cw, cb, wa, ba, wx, bx, lam)


def _head_norm_gate(o, gn, r):
    on = o * lax.rsqrt(jnp.mean(o * o, axis=-1, keepdims=True) + RMS_EPS) * gn
    return on * jax.nn.silu(r)


def _gla_chunk(q, k, v, r, g, gn, S, att):
    C, DK = q.shape
    sub = min(GLA_SUB, C)
    q = q * (DK ** -0.5)
    vb = v.astype(BF16)
    rr = lax.broadcasted_iota(jnp.int32, (C, C), 0)
    cc = lax.broadcasted_iota(jnp.int32, (C, C), 1)
    b = jnp.dot((rr >= cc).astype(F32), g, precision=lax.Precision.HIGHEST,
                preferred_element_type=F32)
    b_last = b[C - 1:C]

    inter = jnp.dot((q * jnp.exp(b)).astype(BF16), S[...].astype(BF16), preferred_element_type=F32)

    lane = lax.broadcasted_iota(jnp.int32, (sub, sub), 1)
    srow = lax.broadcasted_iota(jnp.int32, (sub, sub), 0)
    for I in range(C // sub):
        lo = I * sub
        qI, kI, bI = q[lo:lo + sub], k[lo:lo + sub], b[lo:lo + sub]
        acc = jnp.zeros((sub, sub), F32)
        for s in range(sub):
            w = qI * kI[s:s + 1] * jnp.exp(jnp.minimum(bI - bI[s:s + 1], 0.0))
            acc = jnp.where(lane == s, jnp.sum(w, axis=1, keepdims=True), acc)
        att[lo:lo + sub, lo:lo + sub] = jnp.where(srow >= lane, acc, 0.0)
        if lo > 0:
            ref = b[lo - 1:lo]
            qt = (qI * jnp.exp(bI - ref)).astype(BF16)
            kt = (k[0:lo] * jnp.exp(ref - b[0:lo])).astype(BF16)
            att[lo:lo + sub, 0:lo] = lax.dot_general(qt, kt, (((1,), (1,)), ((), ())),
                                                     preferred_element_type=F32)
        if lo + sub < C:
            att[lo:lo + sub, lo + sub:C] = jnp.zeros((sub, C - lo - sub), F32)
    o = inter + jnp.dot(att[...].astype(BF16), vb, preferred_element_type=F32)

    kh = (k * jnp.exp(b_last - b)).astype(BF16)
    upd = lax.dot_general(kh, vb, (((0,), (0,)), ((), ())), preferred_element_type=F32)
    S[...] = _row_to_col(jnp.exp(b_last), DK) * S[...] + upd
    return _head_norm_gate(o, gn, r)


def _gla_prompt_kernel(q_ref, k_ref, v_ref, r_ref, la_ref, s0_ref, gn_ref, y_ref, so_ref, S, att):
    c = pl.program_id(2)
    nc = pl.num_programs(2)
    hb, DK, DV = S.shape

    @pl.when(c == 0)
    def _():
        S[...] = s0_ref[0]

    for h in range(hb):
        kk = slice(h * DK, (h + 1) * DK)
        vv = slice(h * DV, (h + 1) * DV)
        y = _gla_chunk(q_ref[0, :, kk], k_ref[0, :, kk], v_ref[0, :, vv], r_ref[0, :, vv],
                       la_ref[0, :, kk], gn_ref[:, vv], S.at[h], att.at[h])
        y_ref[0, :, vv] = y.astype(y_ref.dtype)

    @pl.when(c == nc - 1)
    def _():
        so_ref[0] = S[...]


def _gla_prompt(proj, la, s0, gn, offs):
    B, L, _ = proj.shape
    _, H, DK, DV = s0.shape
    C = _pick(L, GLA_CHUNK)
    hb = 4 if H % 4 == 0 else (2 if H % 2 == 0 else 1)
    wk, wv = hb * DK, hb * DV
    qo, ko, vo, ro = offs
    assert all(o % wk == 0 for o in (qo, ko)) and all(o % wv == 0 for o in (vo, ro))
    return pl.pallas_call(
        _gla_prompt_kernel,
        name="gla_prompt",
        grid=(B, H // hb, L // C),
        in_specs=[pl.BlockSpec((1, C, wk), lambda b, h, c: (b, c, qo // wk + h)),
                  pl.BlockSpec((1, C, wk), lambda b, h, c: (b, c, ko // wk + h)),
                  pl.BlockSpec((1, C, wv), lambda b, h, c: (b, c, vo // wv + h)),
                  pl.BlockSpec((1, C, wv), lambda b, h, c: (b, c, ro // wv + h)),
                  pl.BlockSpec((1, C, wk), lambda b, h, c: (b, c, h)),
                  pl.BlockSpec((1, hb, DK, DV), lambda b, h, c: (b, h, 0, 0)),
                  pl.BlockSpec((1, wv), lambda b, h, c: (0, h))],
        out_specs=[pl.BlockSpec((1, C, wv), lambda b, h, c: (b, c, h)),
                   pl.BlockSpec((1, hb, DK, DV), lambda b, h, c: (b, h, 0, 0))],
        out_shape=[jax.ShapeDtypeStruct((B, L, H * DV), BF16),
                   jax.ShapeDtypeStruct((B, H, DK, DV), F32)],
        scratch_shapes=[pltpu.VMEM((hb, DK, DV), F32), pltpu.VMEM((hb, C, C), F32)],
        compiler_params=_params("arbitrary", "arbitrary", "arbitrary"),
    )(proj, proj, proj, proj, la, s0, gn)


def _gla_decode_kernel(q_ref, k_ref, v_ref, r_ref, la_ref, s0_ref, gn_ref, y_ref, so_ref):
    bb, H, _, DK = q_ref.shape
    for b in range(bb):
        for h in range(H):
            q = q_ref[b, h] * (DK ** -0.5)
            S = (_row_to_col(jnp.exp(la_ref[b, h]), DK) * s0_ref[b, h]
                 + _row_to_col(k_ref[b, h], DK) * v_ref[b, h])
            so_ref[b, h] = S
            o = jnp.sum(_row_to_col(q, DK) * S, axis=0, keepdims=True)
            y_ref[b, h] = _head_norm_gate(o, gn_ref[h], r_ref[b, h]).astype(y_ref.dtype)


def _gla_decode(q, k, v, r, la, s0, gn):
    B, H, DK, DV = s0.shape
    bb = 4 if B % 4 == 0 else 1
    vec = lambda d: pl.BlockSpec((bb, H, 1, d), lambda b: (b, 0, 0, 0))
    mat = pl.BlockSpec((bb, H, DK, DV), lambda b: (b, 0, 0, 0))
    return pl.pallas_call(
        _gla_decode_kernel,
        name="gla_decode",
        grid=(B // bb,),
        in_specs=[vec(DK), vec(DK), vec(DV), vec(DV), vec(DK), mat,
                  pl.BlockSpec((H, 1, DV), lambda b: (0, 0, 0))],
        out_specs=[vec(DV), mat],
        out_shape=[jax.ShapeDtypeStruct((B, H, 1, DV), BF16),
                   jax.ShapeDtypeStruct((B, H, DK, DV), F32)],
        compiler_params=_params("arbitrary"),
    )(q, k, v, r, la, s0, gn)


def _outproj_kernel(ya_ref, yb_ref, wa_ref, wb_ref, x_ref, gt_ref, o_ref, wab, wbb):
    @pl.when((pl.program_id(1) == 0) & (pl.program_id(2) == 0))
    def _():
        wab[...] = wa_ref[...].astype(BF16)
        wbb[...] = wb_ref[...].astype(BF16)

    mix = (jnp.dot(ya_ref[0], wab[...], preferred_element_type=F32)
           + jnp.dot(yb_ref[0], wbb[...], preferred_element_type=F32))
    o_ref[0] = x_ref[0] + gt_ref[0] * mix


def _outproj(ya, yb, w, x, gt):
    B, L, D = x.shape
    Ka, Kb = ya.shape[2], yb.shape[2]
    assert Ka == Kb
    tm = _pick(L, 512)
    tn = _pick(D, 1024, LANES)
    gspec = (pl.BlockSpec((1, 1, tn), lambda j, b, l: (b, 0, j)) if gt.shape[1] == 1
             else pl.BlockSpec((1, tm, tn), lambda j, b, l: (b, l, j)))
    once = pl.Buffered(1)
    return pl.pallas_call(
        _outproj_kernel,
        name="outproj",
        grid=(D // tn, B, L // tm),
        in_specs=[pl.BlockSpec((1, tm, Ka), lambda j, b, l: (b, l, 0)),
                  pl.BlockSpec((1, tm, Kb), lambda j, b, l: (b, l, 0)),
                  pl.BlockSpec((Ka, tn), lambda j, b, l: (0, j), pipeline_mode=once),
                  pl.BlockSpec((Kb, tn), lambda j, b, l: (1, j), pipeline_mode=once),
                  pl.BlockSpec((1, tm, tn), lambda j, b, l: (b, l, j)),
                  gspec],
        out_specs=pl.BlockSpec((1, tm, tn), lambda j, b, l: (b, l, j)),
        out_shape=jax.ShapeDtypeStruct((B, L, D), F32),
        scratch_shapes=[pltpu.VMEM((Ka, tn), BF16), pltpu.VMEM((Kb, tn), BF16)],
        compiler_params=_params("arbitrary", "arbitrary", "arbitrary"),
    )(ya, yb, w, w, x, gt)


def _topk_rows(problems, ids, kk, fill):
    ss = [p[0] for p in problems]
    for j in range(kk):
        for n, (_, val_ref, idx_ref) in enumerate(problems):
            s = ss[n]
            m = jnp.max(s, axis=0, keepdims=True)
            idx = jnp.min(jnp.where(s == m, ids, fill), axis=0, keepdims=True)
            val_ref[j:j + 1, :] = m
            idx_ref[j:j + 1, :] = idx
            ss[n] = jnp.where(ids == idx, -jnp.inf, s)


def _sort_network(n):
    out = []
    p = 1
    while p < n:
        k = p
        while k >= 1:
            for j in range(k % p, n - k, 2 * k):
                for i in range(min(k, n - j - k)):
                    if (i + j) // (2 * p) == (i + j + k) // (2 * p):
                        out.append((i + j, i + j + k))
            k //= 2
        p *= 2
    return out


def _topk_sorted(problems, kk):
    big = 3.0e38
    state = []
    for s, _, _ in problems:
        n, t = s.shape
        ng = n // SUBLANES
        V = [s[j * SUBLANES:(j + 1) * SUBLANES] for j in range(ng)]
        base = lax.broadcasted_iota(jnp.int32, (SUBLANES, t), 0).astype(F32)
        I = [base + float(j * SUBLANES) for j in range(ng)]
        for a, b in _sort_network(ng):
            swap = V[b] > V[a]
            V[a], V[b] = jnp.where(swap, V[b], V[a]), jnp.where(swap, V[a], V[b])
            I[a], I[b] = jnp.where(swap, I[b], I[a]), jnp.where(swap, I[a], I[b])
        V = V[:kk] + [jnp.full((SUBLANES, t), -jnp.inf, F32)]
        I = I[:kk] + [jnp.full((SUBLANES, t), big, F32)]
        state.append([V, I, None, jnp.zeros((1, t), F32)])
    for k in range(kk + 1):
        for st, (_, val_ref, idx_ref) in zip(state, problems):
            V, I, prev, flag = st
            m = jnp.max(V[0], axis=0, keepdims=True)
            if prev is not None:
                flag = jnp.maximum(flag, jnp.where(m == prev, 1.0, 0.0))
            if k < kk:
                idx = jnp.min(jnp.where(V[0] == m, I[0], big), axis=0, keepdims=True)
                val_ref[k:k + 1, :] = m
                idx_ref[k:k + 1, :] = idx
                won = I[0] == idx
                V = [jnp.where(won, V[p + 1], V[p]) for p in range(len(V) - 1)]
                I = [jnp.where(won, I[p + 1], I[p]) for p in range(len(I) - 1)]
            st[:] = [V, I, m, flag]
    flag = state[0][3]
    for st in state[1:]:
        flag = jnp.maximum(flag, st[3])
    return flag


def _pair_layout(K):
    segs, ids, r0 = [], [], 0
    b = 0
    while b < K and K // (b + 1) > 1:
        na = K // (b + 1)
        segs.append((r0, (0, na), (b, b + 1)))
        rows = -(-na // SUBLANES) * SUBLANES
        ids += [a * K + b for a in range(na)] + [K * K] * (rows - na)
        r0 += rows
        b += 1
    if b < K:
        nb = K - b
        segs.append((r0, (0, 1), (b, K)))
        rows = -(-nb // SUBLANES) * SUBLANES
        ids += list(range(b, K)) + [K * K] * (rows - nb)
        r0 += rows
    return segs, ids, r0


def _peer_topk_kernel(q_ref, keys_ref, pid_ref, e1_ref, e2_ref, g_ref, sv, si, cand, sc, pos,
                      *, segs):
    tT = q_ref.shape[0]
    H, _, NK, dq = keys_ref.shape
    K = PEER_TOPK
    hp = sv.shape[0]
    ids = lax.broadcasted_iota(jnp.int32, (NK, tT), 0).astype(F32)
    pids = pid_ref[...]

    def heads(i, carry):
        halves = []
        for n in range(hp):
            for p in range(2):
                h = i * hp + n
                off = pl.multiple_of((2 * h + p) * dq, dq)
                qp = q_ref[:, pl.ds(off, dq)].astype(BF16)
                s = lax.dot_general(keys_ref[h, p].astype(BF16), qp, (((1,), (1,)), ((), ())),
                                    preferred_element_type=F32)
                halves.append((s, sv.at[n, p], si.at[n, p]))
        tied = _topk_sorted(halves, K)

        @pl.when(jnp.max(tied) > 0.0)
        def _():
            _topk_rows(halves, ids, K, float(NK))

        pairs = []
        for n in range(hp):
            v0, v1 = sv[n, 0], sv[n, 1]
            cand[n] = jnp.full(cand.shape[1:], -jnp.inf, F32)
            for r0, (a0, a1), (b0, b1) in segs:
                rows = max(a1 - a0, b1 - b0)
                cand[n, r0:r0 + rows, :] = v0[a0:a1] + v1[b0:b1]
            pairs.append((cand[n], sc.at[n], pos.at[n]))
        _topk_rows(pairs, pids, K, float(K * K))
        for n in range(hp):
            h = i * hp + n
            pa = jnp.floor(pos[n] * (1.0 / K))
            pb = pos[n] - pa * K
            i0, i1 = si[n, 0], si[n, 1]
            e1 = jnp.zeros((K, tT), F32)
            e2 = jnp.zeros((K, tT), F32)
            for a in range(K):
                e1 = jnp.where(pa == a, i0[a:a + 1], e1)
                e2 = jnp.where(pb == a, i1[a:a + 1], e2)
            scv = sc[n]
            ex = jnp.exp(scv - jnp.max(scv, axis=0, keepdims=True))
            e1_ref[h] = e1
            e2_ref[h] = e2
            g_ref[h] = ex / jnp.sum(ex, axis=0, keepdims=True)
        return carry

    lax.fori_loop(0, H // hp, heads, 0)


def _peer_topk(q, keys):
    T = q.shape[0]
    H, _, NK, dq = keys.shape
    K = PEER_TOPK
    tT = _pick(T, LANES, LANES)
    hp = 8 if H % 8 == 0 else 1
    segs, ids, nr = _pair_layout(K)
    pids = jnp.broadcast_to(jnp.asarray(ids, F32)[:, None], (nr, tT))
    out = jax.ShapeDtypeStruct((H, K, T), F32)
    ospec = pl.BlockSpec((H, K, tT), lambda i: (0, 0, i))
    return pl.pallas_call(
        functools.partial(_peer_topk_kernel, segs=segs),
        name="peer_topk",
        grid=(T // tT,),
        in_specs=[pl.BlockSpec((tT, 2 * H * dq), lambda i: (i, 0)),
                  pl.BlockSpec((H, 2, NK, dq), lambda i: (0, 0, 0, 0)),
                  pl.BlockSpec((nr, tT), lambda i: (0, 0))],
        out_specs=[ospec, ospec, ospec],
        out_shape=[out, out, out],
        scratch_shapes=[pltpu.VMEM((hp, 2, K, tT), F32), pltpu.VMEM((hp, 2, K, tT), F32),
                        pltpu.VMEM((hp, nr, tT), F32), pltpu.VMEM((hp, K, tT), F32),
                        pltpu.VMEM((hp, K, tT), F32)],
        compiler_params=_params("arbitrary"),
    )(q, keys, pids)


def _peer_gate_kernel(e1_ref, e2_ref, g_ref, o_ref, e1t, e2t, gt, gbuf, *, unroll, pitch):
    NS, tT = e1_ref.shape
    NK = gbuf.shape[1]
    e1t[...] = e1_ref[...].T
    e2t[...] = e2_ref[...].T
    gt[...] = g_ref[...].T
    kid = lax.broadcasted_iota(jnp.int32, (NK, NS), 0).astype(F32)

    def body(i, carry):
        for j in range(unroll):
            t = i * unroll + j
            a = jnp.where(kid == e1t[pl.ds(t, 1), :], gt[pl.ds(t, 1), :], 0.0).astype(BF16)
            b = jnp.where(kid == e2t[pl.ds(t, 1), :], 1.0, 0.0).astype(BF16)
            gbuf[pl.ds(pl.multiple_of(t * pitch, SUBLANES), NK), :] = lax.dot_general(
                a, b, (((1,), (1,)), ((), ())), preferred_element_type=F32)
        return carry

    lax.fori_loop(0, tT // unroll, body, 0)
    for i1 in range(NK):
        o_ref[:, i1 * NK:(i1 + 1) * NK] = gbuf[pl.ds(i1, tT, stride=pitch), :].astype(o_ref.dtype)


def _peer_gates(e1, e2, g, NK):
    NS, T = e1.shape
    tT = _pick(T, 128, LANES)
    unroll = 32 if tT % 32 == 0 else 1
    pitch = NK + SUBLANES
    ispec = pl.BlockSpec((NS, tT), lambda i: (0, i))
    return pl.pallas_call(
        functools.partial(_peer_gate_kernel, unroll=unroll, pitch=pitch),
        name="peer_gate",
        grid=(T // tT,),
        in_specs=[ispec, ispec, ispec],
        out_specs=pl.BlockSpec((tT, NK * NK), lambda i: (i, 0)),
        out_shape=jax.ShapeDtypeStruct((T, NK * NK), BF16),
        scratch_shapes=[pltpu.VMEM((tT, NS), F32)] * 3 + [pltpu.VMEM((tT * pitch, NK), F32)],
        compiler_params=_params("arbitrary"),
    )(e1, e2, g)


def _peer_dense_kernel(x_ref, u_ref, v_ref, g_ref, o_ref, *, nsplit):
    e = pl.program_id(1)

    @pl.when(e == 0)
    def _():
        o_ref[...] = jnp.zeros(o_ref.shape, o_ref.dtype)

    rs = x_ref.shape[0] // nsplit
    for r in range(nsplit):
        rows = slice(r * rs, (r + 1) * rs)
        act = lax.dot_general(x_ref[rows, :], u_ref[...], (((1,), (1,)), ((), ())),
                              preferred_element_type=F32)
        p = (g_ref[rows, :].astype(F32) * jax.nn.gelu(act)).astype(BF16)
        o_ref[rows, :] += jnp.dot(p, v_ref[...], preferred_element_type=F32)


def _peer_dense(x, u, v, G):
    T, D = x.shape
    E = u.shape[0]
    tT = _pick(T, 640, 16)
    tE = _pick(E, 512, LANES)
    nsplit = 2 if tT % 32 == 0 else 1
    return pl.pallas_call(
        functools.partial(_peer_dense_kernel, nsplit=nsplit),
        name="peer_dense",
        grid=(T // tT, E // tE),
        in_specs=[pl.BlockSpec((tT, D), lambda i, e: (i, 0)),
                  pl.BlockSpec((tE, D), lambda i, e: (e, 0)),
                  pl.BlockSpec((tE, D), lambda i, e: (e, 0)),
                  pl.BlockSpec((tT, tE), lambda i, e: (i, e))],
        out_specs=pl.BlockSpec((tT, D), lambda i, e: (i, 0)),
        out_shape=jax.ShapeDtypeStruct((T, D), F32),
        compiler_params=_params("arbitrary", "arbitrary"),
    )(x, u, v, G)


def _row_groups(rows):
    for n in (5, 4, 2):
        if rows % (16 * n) == 0:
            return n
    return 1


def _peer_act_kernel(x_ref, u_ref, g_ref, p_ref, *, nsplit):
    ub = u_ref[...].astype(BF16)
    rs = x_ref.shape[0] // nsplit
    for r in range(nsplit):
        rows = slice(r * rs, (r + 1) * rs)
        act = lax.dot_general(x_ref[rows, :], ub, (((1,), (1,)), ((), ())),
                              preferred_element_type=F32)
        p_ref[rows, :] = (g_ref[rows, :].astype(F32) * jax.nn.gelu(act)).astype(p_ref.dtype)


def _peer_act(x, u, G):
    T, D = x.shape
    E = u.shape[0]
    tT = _pick(T, 2080, 16)
    tE = _pick(E, 512, LANES)
    nsplit = _row_groups(tT)
    return pl.pallas_call(
        functools.partial(_peer_act_kernel, nsplit=nsplit),
        name="peer_act",
        grid=(T // tT, E // tE),
        in_specs=[pl.BlockSpec((tT, D), lambda i, e: (i, 0), pipeline_mode=pl.Buffered(1)),
                  pl.BlockSpec((tE, D), lambda i, e: (e, 0)),
                  pl.BlockSpec((tT, tE), lambda i, e: (i, e))],
        out_specs=pl.BlockSpec((tT, tE), lambda i, e: (i, e)),
        out_shape=jax.ShapeDtypeStruct((T, E), BF16),
        compiler_params=_params("arbitrary", "arbitrary"),
    )(x, u, G)


def _peer_out_kernel(p_ref, v_ref, o_ref, *, nsplit):
    e = pl.program_id(1)

    @pl.when(e == 0)
    def _():
        o_ref[...] = jnp.zeros(o_ref.shape, o_ref.dtype)

    vb = v_ref[...].astype(BF16)
    rs = p_ref.shape[0] // nsplit
    for r in range(nsplit):
        rows = slice(r * rs, (r + 1) * rs)
        o_ref[rows, :] += jnp.dot(p_ref[rows, :], vb, preferred_element_type=F32)


def _peer_out(p, v):
    T, E = p.shape
    D = v.shape[1]
    tT = _pick(T, 1664, 16)
    tE = _pick(E, 512, LANES)
    nsplit = 8 if tT % 128 == 0 else _row_groups(tT)
    return pl.pallas_call(
        functools.partial(_peer_out_kernel, nsplit=nsplit),
        name="peer_out",
        grid=(T // tT, E // tE),
        in_specs=[pl.BlockSpec((tT, tE), lambda i, e: (i, e)),
                  pl.BlockSpec((tE, D), lambda i, e: (e, 0))],
        out_specs=pl.BlockSpec((tT, D), lambda i, e: (i, 0), pipeline_mode=pl.Buffered(1)),
        out_shape=jax.ShapeDtypeStruct((T, D), F32),
        compiler_params=pltpu.CompilerParams(dimension_semantics=("arbitrary", "arbitrary"),
                                             vmem_limit_bytes=BIG_VMEM_LIMIT),
    )(p, v)


def _resid_kernel(x_ref, y_ref, gt_ref, o_ref):
    o_ref[0] = x_ref[0] + gt_ref[0] * y_ref[...]


def _resid_norm_kernel(x_ref, y_ref, gt_ref, g_ref, o_ref):
    x = x_ref[0] + gt_ref[0] * y_ref[...]
    o_ref[0] = x * lax.rsqrt(jnp.mean(x * x, axis=-1, keepdims=True) + RMS_EPS) * g_ref[...]


def _resid(x, y, row0, gt, g=None):
    B, L, D = x.shape
    tl = _pick(L, 256)
    assert row0 % tl == 0
    lb = L // tl
    blk = pl.BlockSpec((1, tl, D), lambda b, l: (b, l, 0))
    in_specs = [blk, pl.BlockSpec((tl, D), lambda b, l: (row0 // tl + b * lb + l, 0)),
                _mod_spec(gt, tl, D)]
    args = [x, y, gt]
    if g is not None:
        in_specs.append(pl.BlockSpec((1, D), lambda b, l: (0, 0)))
        args.append(g)
    return pl.pallas_call(
        _resid_kernel if g is None else _resid_norm_kernel,
        name="resid",
        grid=(B, L // tl),
        in_specs=in_specs,
        out_specs=blk,
        out_shape=jax.ShapeDtypeStruct((B, L, D), F32),
        compiler_params=_params("arbitrary", "arbitrary"),
    )(*args)


def kernel(x_prompt, x_sample, state_conv, state_lru_h, state_gla, c_prompt, c_sample, w_ada, b_ada, g_mix, w_in, conv_w, conv_b, lru_w_a, lru_b_a, lru_w_x, lru_b_x, lru_lambda, gla_w_alpha, gla_b_alpha, gla_g_norm, w_out, g_ffn, peer_w_q, peer_sub_keys, peer_u, peer_v, g_final):
    depth = w_ada.shape[0]
    Bp, L, D = x_prompt.shape
    Bs = x_sample.shape[0]
    assert x_sample.shape[1] == 1
    W = conv_w.shape[2]
    nt = conv_w.shape[1] - 1
    _, _, H, DK, DV = state_gla.shape
    rank = gla_w_alpha.shape[1]
    PH, _, NK, _ = peer_sub_keys.shape[1:]
    n_main = 2 * W + 2 * H * DK + 2 * H * DV
    assert w_in.shape[2] == n_main + rank and rank <= LANES
    offs = (2 * W, 2 * W + H * DK, 2 * W + 2 * H * DK, 2 * W + 2 * H * DK + H * DV)
    Tp = Bp * L

    R = Bp + Bs
    Rp = -(-R // SUBLANES) * SUBLANES
    c_all = jnp.pad(jnp.concatenate([c_prompt, c_sample], axis=0), ((0, Rp - R), (0, 0)))

    x_p = x_prompt
    x_s = x_sample.reshape(1, Bs, D)
    outs = [[] for _ in range(6)]
    for l in range(depth):
        ada = _ada(c_all, w_ada[l], b_ada[l][None])
        mods_p = [ada[:Bp, i * D:(i + 1) * D][:, None, :] for i in range(6)]
        mods_s = [ada[Bp:R, i * D:(i + 1) * D][None] for i in range(6)]

        w_in_t = jnp.transpose(w_in[l])
        in_proj = functools.partial(_matmul, w=w_in_t, N=n_main, w_rows_are_outputs=True,
                                    vmem_limit=BIG_VMEM_LIMIT)
        norm_gate = functools.partial(_norm_mod_gate, g=g_mix[l][None], w_t=w_in_t, row0=n_main,
                                      w2=gla_w_alpha[l], b2=gla_b_alpha[l][None])
        wa = lru_w_a[l].astype(BF16)
        wx = lru_w_x[l].astype(BF16)
        lru_vecs = (conv_w[l], conv_b[l][None], wa, lru_b_a[l][None], wx, lru_b_x[l][None],
                    lru_lambda[l][None])
        gn = gla_g_norm[l]

        sh1, sc1, gt1, sh2, sc2, gt2 = mods_p
        hn, la = norm_gate(x_p, sc=sc1, sh=sh1)
        proj = in_proj(hn.reshape(Tp, D)).reshape(Bp, L, n_main)
        y_lru, h_p, conv_p = _lru_prompt(
            proj, jnp.zeros((Bp, nt, W), F32), jnp.zeros((Bp, 1, W), F32), *lru_vecs, W)
        y_gla, s_p = _gla_prompt(proj, la, jnp.zeros((Bp, H, DK, DV), F32), gn[None], offs)
        x1_p = _outproj(y_lru, y_gla, w_out[l], x_p, gt1)
        sc2_p, sh2_p, gt2_p = sc2, sh2, gt2

        sh1, sc1, gt1, sh2, sc2, gt2 = mods_s
        hn, la = norm_gate(x_s, sc=sc1, sh=sh1)
        proj = in_proj(hn.reshape(Bs, D))
        la = la.reshape(Bs, H * DK)
        tail = jnp.transpose(state_conv[l], (1, 0, 2))
        y_lru, h_s = _lru_decode(proj, tail, state_lru_h[l], *lru_vecs, W)
        conv_s = jnp.concatenate([state_conv[l][:, 1:], proj[:, None, :W]], axis=1)
        qo, ko, vo, ro = offs
        heads = lambda a, d: a.reshape(Bs, H, 1, d)
        y_gla, s_s = _gla_decode(
            heads(proj[:, qo:ko], DK), heads(proj[:, ko:vo], DK), heads(proj[:, vo:ro], DV),
            heads(proj[:, ro:n_main], DV), heads(la, DK), state_gla[l], gn.reshape(H, 1, DV))
        x1_s = _outproj(y_lru[None], y_gla.reshape(1, Bs, H * DV), w_out[l], x_s, gt1)
        gt2_s = gt2

        hn2 = _norm_mod_cat(x1_p, x1_s, g_ffn[l][None], sc2_p, sh2_p, sc2, sh2, BF16)
        T = Tp + Bs
        q = _matmul(hn2, peer_w_q[l])
        e1, e2, gate = _peer_topk(q, peer_sub_keys[l])
        G = _peer_gates(e1.reshape(PH * PEER_TOPK, T), e2.reshape(PH * PEER_TOPK, T),
                        gate.reshape(PH * PEER_TOPK, T), NK)
        y = _peer_out(_peer_act(hn2, peer_u[l], G), peer_v[l])
        last = l == depth - 1
        x_p = _resid(x1_p, y, 0, gt2_p, g_final[None] if last else None)
        x_s = _resid(x1_s, y, Tp, gt2_s, g_final[None] if last else None)

        for lst, val in zip(outs, (conv_p, h_p.reshape(Bp, W), s_p, conv_s, h_s, s_s)):
            lst.append(val)

    return (x_p, x_s.reshape(Bs, 1, D)) + tuple(jnp.stack(o) for o in outs)
```

```python
import functools

import jax
import jax.numpy as jnp
from jax import lax
from jax.experimental import pallas as pl
from jax.experimental.pallas import tpu as pltpu

F32 = jnp.float32
BF16 = jnp.bfloat16

RMS_EPS = 1e-6
RG_C = 8.0
GLA_TAU = 16.0
PEER_TOPK = 16
GLA_CHUNK = 128
GLA_SUB = 8
LANES = 128
SUBLANES = 8
VMEM_LIMIT = 56 * 1024 * 1024
BIG_VMEM_LIMIT = 60 * 1024 * 1024


def _params(*sem):
    return pltpu.CompilerParams(dimension_semantics=sem, vmem_limit_bytes=VMEM_LIMIT)


def _pick(n, pref, mult=SUBLANES):
    best = None
    for d in range(mult, min(n, pref) + 1, mult):
        if n % d == 0:
            best = d
    return n if best is None else best


def _softplus(y):
    return jnp.maximum(y, 0.0) + jnp.log1p(jnp.exp(-jnp.abs(y)))


def _log_sigmoid(z):
    return jnp.minimum(z, 0.0) - jnp.log1p(jnp.exp(-jnp.abs(z)))


def _row_to_col(row, n):
    r = lax.broadcasted_iota(jnp.int32, (n, n), 0)
    c = lax.broadcasted_iota(jnp.int32, (n, n), 1)
    return jnp.sum(jnp.where(r == c, jnp.broadcast_to(row, (n, n)), 0.0), axis=1, keepdims=True)


def _ada_kernel(c_ref, w_ref, b_ref, o_ref):
    a = jax.nn.silu(c_ref[...]).astype(BF16)
    o_ref[...] = jnp.dot(a, w_ref[...].astype(BF16), preferred_element_type=F32) + b_ref[...]


def _ada(c, w, b):
    R, D = c.shape
    N = w.shape[1]
    tn = _pick(N, 512, LANES)
    return pl.pallas_call(
        _ada_kernel,
        name="ada",
        grid=(N // tn,),
        in_specs=[pl.BlockSpec((R, D), lambda j: (0, 0)),
                  pl.BlockSpec((D, tn), lambda j: (0, j)),
                  pl.BlockSpec((1, tn), lambda j: (0, j))],
        out_specs=pl.BlockSpec((R, tn), lambda j: (0, j)),
        out_shape=jax.ShapeDtypeStruct((R, N), F32),
        compiler_params=_params("arbitrary"),
    )(c, w, b)


def _norm_mod_gate_kernel(x_ref, g_ref, sc_ref, sh_ref, w1_ref, w2_ref, b2_ref, o_ref, la_ref):
    x = x_ref[0]
    y = x * lax.rsqrt(jnp.mean(x * x, axis=-1, keepdims=True) + RMS_EPS) * g_ref[...]
    hn = (y * (1.0 + sc_ref[0]) + sh_ref[0]).astype(o_ref.dtype)
    o_ref[0] = hn
    a = lax.dot_general(hn, w1_ref[...].astype(BF16), (((1,), (1,)), ((), ())),
                        preferred_element_type=F32)
    z = jnp.dot(a.astype(BF16), w2_ref[...].astype(BF16), preferred_element_type=F32) + b2_ref[...]
    la_ref[0] = _log_sigmoid(z) / GLA_TAU


def _mod_spec(m, tl, D):
    if m.shape[1] == 1:
        return pl.BlockSpec((1, 1, D), lambda b, l: (b, 0, 0))
    return pl.BlockSpec((1, tl, D), lambda b, l: (b, l, 0))


def _norm_mod_gate(x, g, sc, sh, w_t, row0, w2, b2):
    B, L, D = x.shape
    R, N = w2.shape
    assert row0 % R == 0 and R % SUBLANES == 0
    tl = _pick(L, 256)
    const = lambda b, l: (0, 0)
    return pl.pallas_call(
        _norm_mod_gate_kernel,
        name="norm_mod_gate",
        grid=(B, L // tl),
        in_specs=[pl.BlockSpec((1, tl, D), lambda b, l: (b, l, 0)),
                  pl.BlockSpec((1, D), const),
                  _mod_spec(sc, tl, D), _mod_spec(sh, tl, D),
                  pl.BlockSpec((R, D), lambda b, l: (row0 // R, 0)),
                  pl.BlockSpec((R, N), const),
                  pl.BlockSpec((1, N), const)],
        out_specs=[pl.BlockSpec((1, tl, D), lambda b, l: (b, l, 0)),
                   pl.BlockSpec((1, tl, N), lambda b, l: (b, l, 0))],
        out_shape=[jax.ShapeDtypeStruct((B, L, D), BF16), jax.ShapeDtypeStruct((B, L, N), F32)],
        compiler_params=_params("arbitrary", "arbitrary"),
    )(x, g, sc, sh, w_t, w2, b2)


def _norm_mod_cat_kernel(xp_ref, xs_ref, g_ref, scp_ref, shp_ref, scs_ref, shs_ref, o_ref, *, n_p):
    i = pl.program_id(0)

    def f(x, sc, sh):
        y = x * lax.rsqrt(jnp.mean(x * x, axis=-1, keepdims=True) + RMS_EPS) * g_ref[...]
        return (y * (1.0 + sc) + sh).astype(o_ref.dtype)

    @pl.when(i < n_p)
    def _():
        o_ref[...] = f(xp_ref[0], scp_ref[0], shp_ref[0])

    @pl.when(i >= n_p)
    def _():
        o_ref[...] = f(xs_ref[0], scs_ref[0], shs_ref[0])


def _norm_mod_cat(xp, xs, g, scp, shp, scs, shs, out_dtype):
    Bp, L, D = xp.shape
    Bs = xs.shape[1]
    tl = _pick(Bs, 256)
    assert L % tl == 0
    lb = L // tl
    n_p = Bp * lb
    n_s = Bs // tl
    pb = lambda i: jnp.minimum(i, n_p - 1)
    sb = lambda i: jnp.maximum(i - n_p, 0)
    pmod = pl.BlockSpec((1, 1, D), lambda i: (pb(i) // lb, 0, 0))
    smod = pl.BlockSpec((1, tl, D), lambda i: (0, sb(i), 0))
    return pl.pallas_call(
        functools.partial(_norm_mod_cat_kernel, n_p=n_p),
        name="norm_mod_cat",
        grid=(n_p + n_s,),
        in_specs=[pl.BlockSpec((1, tl, D), lambda i: (pb(i) // lb, pb(i) % lb, 0)),
                  pl.BlockSpec((1, tl, D), lambda i: (0, sb(i), 0)),
                  pl.BlockSpec((1, D), lambda i: (0, 0)),
                  pmod, pmod, smod, smod],
        out_specs=pl.BlockSpec((tl, D), lambda i: (i, 0)),
        out_shape=jax.ShapeDtypeStruct((Bp * L + Bs, D), out_dtype),
        compiler_params=_params("arbitrary"),
    )(xp, xs, g, scp, shp, scs, shs)


def _mm_kernel(x_ref, w_ref, o_ref, wb, *, w_rows_are_outputs):
    @pl.when(pl.program_id(1) == 0)
    def _():
        wb[...] = w_ref[...].astype(BF16)

    dims = (((1,), (1,)), ((), ())) if w_rows_are_outputs else (((1,), (0,)), ((), ()))
    o_ref[...] = lax.dot_general(x_ref[...], wb[...], dims, preferred_element_type=F32)


def _matmul(x, w, N=None, *, w_rows_are_outputs=False, tm_pref=512, tn_pref=1024,
            vmem_limit=VMEM_LIMIT):
    M, K = x.shape
    n_all = w.shape[0] if w_rows_are_outputs else w.shape[1]
    N = n_all if N is None else N
    tm = _pick(M, tm_pref, 16)
    tn = _pick(N, tn_pref, LANES)
    wblk = (tn, K) if w_rows_are_outputs else (K, tn)
    wmap = (lambda j, i: (j, 0)) if w_rows_are_outputs else (lambda j, i: (0, j))
    return pl.pallas_call(
        functools.partial(_mm_kernel, w_rows_are_outputs=w_rows_are_outputs),
        name="matmul",
        grid=(N // tn, M // tm),
        in_specs=[pl.BlockSpec((tm, K), lambda j, i: (i, 0)), pl.BlockSpec(wblk, wmap)],
        out_specs=pl.BlockSpec((tm, tn), lambda j, i: (i, j)),
        out_shape=jax.ShapeDtypeStruct((M, N), F32),
        scratch_shapes=[pltpu.VMEM(wblk, BF16)],
        compiler_params=pltpu.CompilerParams(dimension_semantics=("arbitrary", "arbitrary"),
                                             vmem_limit_bytes=vmem_limit),
    )(x, w)


def _lru_gates(xc, wa_ref, ba, wx_ref, bx, lam):
    nb, bd = wa_ref.shape[0], wa_ref.shape[1]
    xcb = xc.astype(BF16)
    ra, ri = [], []
    for n in range(nb):
        xs = xcb[:, n * bd:(n + 1) * bd]
        ra.append(jnp.dot(xs, wa_ref[n], preferred_element_type=F32))
        ri.append(jnp.dot(xs, wx_ref[n], preferred_element_type=F32))
    r = jax.nn.sigmoid(jnp.concatenate(ra, axis=1) + ba)
    i = jax.nn.sigmoid(jnp.concatenate(ri, axis=1) + bx)
    log_a = -RG_C * r * _softplus(-lam)
    a = jnp.exp(log_a)
    u = jnp.sqrt(-jnp.tanh(log_a) * (a * a + 1.0)) * i * xc
    return a, u


def _lru_prompt_kernel(xb_ref, gb_ref, tail_ref, h0_ref, cw_ref, cb_ref, wa_ref, ba_ref, wx_ref,
                       bx_ref, lam_ref, y_ref, hl_ref, conv_ref, buf, hcar):
    l = pl.program_id(2)
    nl = pl.num_programs(2)
    tl, tw = xb_ref.shape[1], xb_ref.shape[2]
    nt = tail_ref.shape[1]
    base = SUBLANES

    @pl.when(l == 0)
    def _():
        buf[base - nt:base, :] = tail_ref[0]
        hcar[...] = h0_ref[0]

    xb = xb_ref[0]
    buf[base:base + tl, :] = xb
    cw = cw_ref[...]
    xc = cb_ref[...] + xb * cw[nt:nt + 1]
    for j in range(nt):
        xc = xc + buf[base - nt + j:base - nt + j + tl, :] * cw[j:j + 1]
    new_tail = buf[base + tl - nt:base + tl, :]

    a, u = _lru_gates(xc, wa_ref, ba_ref[...], wx_ref, bx_ref[...], lam_ref[...])

    ng = tl // SUBLANES
    A = a.reshape(ng, SUBLANES, tw)
    Bv = u.reshape(ng, SUBLANES, tw)
    row = lax.broadcasted_iota(jnp.int32, (1, SUBLANES, tw), 1)
    s = 1
    while s < SUBLANES:
        As = pltpu.roll(A, s, axis=1)
        Bs = pltpu.roll(Bv, s, axis=1)
        m = row >= s
        Bv = jnp.where(m, A * Bs + Bv, Bv)
        A = jnp.where(m, A * As, A)
        s *= 2
    h = hcar[...]
    outs = []
    for g in range(ng):
        hg = A[g] * h + Bv[g]
        outs.append(hg)
        h = hg[SUBLANES - 1:SUBLANES]
    hseq = jnp.concatenate(outs, axis=0)
    hcar[...] = h
    buf[base - nt:base, :] = new_tail
    y_ref[0] = (hseq * jax.nn.gelu(gb_ref[0])).astype(y_ref.dtype)

    @pl.when(l == nl - 1)
    def _():
        hl_ref[0] = h
        conv_ref[0] = new_tail


def _lru_prompt(proj, tail, h0, cw, cb, wa, ba, wx, bx, lam, W):
    B, L, _ = proj.shape
    nt = tail.shape[1]
    bd = wa.shape[1]
    tw = _pick(W, 512, bd)
    tl = _pick(L, 512)
    nwb = W // tw
    col = lambda b, j, l: (0, j)
    return pl.pallas_call(
        _lru_prompt_kernel,
        name="lru_prompt",
        grid=(B, nwb, L // tl),
        in_specs=[pl.BlockSpec((1, tl, tw), lambda b, j, l: (b, l, j)),
                  pl.BlockSpec((1, tl, tw), lambda b, j, l: (b, l, j + nwb)),
                  pl.BlockSpec((1, nt, tw), lambda b, j, l: (b, 0, j)),
                  pl.BlockSpec((1, 1, tw), lambda b, j, l: (b, 0, j)),
                  pl.BlockSpec((nt + 1, tw), col),
                  pl.BlockSpec((1, tw), col),
                  pl.BlockSpec((tw // bd, bd, bd), lambda b, j, l: (j, 0, 0)),
                  pl.BlockSpec((1, tw), col),
                  pl.BlockSpec((tw // bd, bd, bd), lambda b, j, l: (j, 0, 0)),
                  pl.BlockSpec((1, tw), col),
                  pl.BlockSpec((1, tw), col)],
        out_specs=[pl.BlockSpec((1, tl, tw), lambda b, j, l: (b, l, j)),
                   pl.BlockSpec((1, 1, tw), lambda b, j, l: (b, 0, j)),
                   pl.BlockSpec((1, nt, tw), lambda b, j, l: (b, 0, j))],
        out_shape=[jax.ShapeDtypeStruct((B, L, W), BF16),
                   jax.ShapeDtypeStruct((B, 1, W), F32),
                   jax.ShapeDtypeStruct((B, nt, W), F32)],
        scratch_shapes=[pltpu.VMEM((tl + SUBLANES, tw), F32), pltpu.VMEM((1, tw), F32)],
        compiler_params=_params("arbitrary", "arbitrary", "arbitrary"),
    )(proj, proj, tail, h0, cw, cb, wa, ba, wx, bx, lam)


def _lru_decode_kernel(xb_ref, gb_ref, tail_ref, h0_ref, cw_ref, cb_ref, wa_ref, ba_ref, wx_ref,
                       bx_ref, lam_ref, y_ref, h_ref):
    nt = tail_ref.shape[0]
    xb = xb_ref[...]
    cw = cw_ref[...]
    xc = cb_ref[...] + xb * cw[nt:nt + 1]
    for j in range(nt):
        xc = xc + tail_ref[j] * cw[j:j + 1]
    a, u = _lru_gates(xc, wa_ref, ba_ref[...], wx_ref, bx_ref[...], lam_ref[...])
    h = a * h0_ref[...] + u
    h_ref[...] = h
    y_ref[...] = (h * jax.nn.gelu(gb_ref[...])).astype(y_ref.dtype)


def _lru_decode(proj, tail, h0, cw, cb, wa, ba, wx, bx, lam, W):
    B = proj.shape[0]
    nt = tail.shape[0]
    bd = wa.shape[1]
    tw = _pick(W, 512, bd)
    nwb = W // tw
    col = lambda j: (0, j)
    return pl.pallas_call(
        _lru_decode_kernel,
        name="lru_decode",
        grid=(nwb,),
        in_specs=[pl.BlockSpec((B, tw), col),
                  pl.BlockSpec((B, tw), lambda j: (0, j + nwb)),
                  pl.BlockSpec((nt, B, tw), lambda j: (0, 0, j)),
                  pl.BlockSpec((B, tw), col),
                  pl.BlockSpec((nt + 1, tw), col),
                  pl.BlockSpec((1, tw), col),
                  pl.BlockSpec((tw // bd, bd, bd), lambda j: (j, 0, 0)),
                  pl.BlockSpec((1, tw), col),
                  pl.BlockSpec((tw // bd, bd, bd), lambda j: (j, 0, 0)),
                  pl.BlockSpec((1, tw), col),
                  pl.BlockSpec((1, tw), col)],
        out_specs=[pl.BlockSpec((B, tw), col), pl.BlockSpec((B, tw), col)],
        out_shape=[jax.ShapeDtypeStruct((B, W), BF16), jax.ShapeDtypeStruct((B, W), F32)],
        compiler_params=_params("arbitrary"),
    )(proj, proj, tail, h0, cw, cb, wa, ba, wx, bx, lam)


def _head_norm_gate(o, gn, r):
    on = o * lax.rsqrt(jnp.mean(o * o, axis=-1, keepdims=True) + RMS_EPS) * gn
    return on * jax.nn.silu(r)


def _gla_chunk(q, k, v, r, g, gn, S, att):
    C, DK = q.shape
    sub = min(GLA_SUB, C)
    q = q * (DK ** -0.5)
    vb = v.astype(BF16)
    rr = lax.broadcasted_iota(jnp.int32, (C, C), 0)
    cc = lax.broadcasted_iota(jnp.int32, (C, C), 1)
    b = jnp.dot((rr >= cc).astype(F32), g, precision=lax.Precision.HIGHEST,
                preferred_element_type=F32)
    b_last = b[C - 1:C]

    inter = jnp.dot((q * jnp.exp(b)).astype(BF16), S[...].astype(BF16), preferred_element_type=F32)

    lane = lax.broadcasted_iota(jnp.int32, (sub, sub), 1)
    srow = lax.broadcasted_iota(jnp.int32, (sub, sub), 0)
    for I in range(C // sub):
        lo = I * sub
        qI, kI, bI = q[lo:lo + sub], k[lo:lo + sub], b[lo:lo + sub]
        acc = jnp.zeros((sub, sub), F32)
        for s in range(sub):
            w = qI * kI[s:s + 1] * jnp.exp(jnp.minimum(bI - bI[s:s + 1], 0.0))
            acc = jnp.where(lane == s, jnp.sum(w, axis=1, keepdims=True), acc)
        att[lo:lo + sub, lo:lo + sub] = jnp.where(srow >= lane, acc, 0.0)
        if lo > 0:
            ref = b[lo - 1:lo]
            qt = (qI * jnp.exp(bI - ref)).astype(BF16)
            kt = (k[0:lo] * jnp.exp(ref - b[0:lo])).astype(BF16)
            att[lo:lo + sub, 0:lo] = lax.dot_general(qt, kt, (((1,), (1,)), ((), ())),
                                                     preferred_element_type=F32)
        if lo + sub < C:
            att[lo:lo + sub, lo + sub:C] = jnp.zeros((sub, C - lo - sub), F32)
    o = inter + jnp.dot(att[...].astype(BF16), vb, preferred_element_type=F32)

    kh = (k * jnp.exp(b_last - b)).astype(BF16)
    upd = lax.dot_general(kh, vb, (((0,), (0,)), ((), ())), preferred_element_type=F32)
    S[...] = _row_to_col(jnp.exp(b_last), DK) * S[...] + upd
    return _head_norm_gate(o, gn, r)


def _gla_prompt_kernel(q_ref, k_ref, v_ref, r_ref, la_ref, s0_ref, gn_ref, y_ref, so_ref, S, att):
    c = pl.program_id(2)
    nc = pl.num_programs(2)
    hb, DK, DV = S.shape

    @pl.when(c == 0)
    def _():
        S[...] = s0_ref[0]

    for h in range(hb):
        kk = slice(h * DK, (h + 1) * DK)
        vv = slice(h * DV, (h + 1) * DV)
        y = _gla_chunk(q_ref[0, :, kk], k_ref[0, :, kk], v_ref[0, :, vv], r_ref[0, :, vv],
                       la_ref[0, :, kk], gn_ref[:, vv], S.at[h], att.at[h])
        y_ref[0, :, vv] = y.astype(y_ref.dtype)

    @pl.when(c == nc - 1)
    def _():
        so_ref[0] = S[...]


def _gla_prompt(proj, la, s0, gn, offs):
    B, L, _ = proj.shape
    _, H, DK, DV = s0.shape
    C = _pick(L, GLA_CHUNK)
    hb = 4 if H % 4 == 0 else (2 if H % 2 == 0 else 1)
    wk, wv = hb * DK, hb * DV
    qo, ko, vo, ro = offs
    assert all(o % wk == 0 for o in (qo, ko)) and all(o % wv == 0 for o in (vo, ro))
    return pl.pallas_call(
        _gla_prompt_kernel,
        name="gla_prompt",
        grid=(B, H // hb, L // C),
        in_specs=[pl.BlockSpec((1, C, wk), lambda b, h, c: (b, c, qo // wk + h)),
                  pl.BlockSpec((1, C, wk), lambda b, h, c: (b, c, ko // wk + h)),
                  pl.BlockSpec((1, C, wv), lambda b, h, c: (b, c, vo // wv + h)),
                  pl.BlockSpec((1, C, wv), lambda b, h, c: (b, c, ro // wv + h)),
                  pl.BlockSpec((1, C, wk), lambda b, h, c: (b, c, h)),
                  pl.BlockSpec((1, hb, DK, DV), lambda b, h, c: (b, h, 0, 0)),
                  pl.BlockSpec((1, wv), lambda b, h, c: (0, h))],
        out_specs=[pl.BlockSpec((1, C, wv), lambda b, h, c: (b, c, h)),
                   pl.BlockSpec((1, hb, DK, DV), lambda b, h, c: (b, h, 0, 0))],
        out_shape=[jax.ShapeDtypeStruct((B, L, H * DV), BF16),
                   jax.ShapeDtypeStruct((B, H, DK, DV), F32)],
        scratch_shapes=[pltpu.VMEM((hb, DK, DV), F32), pltpu.VMEM((hb, C, C), F32)],
        compiler_params=_params("arbitrary", "arbitrary", "arbitrary"),
    )(proj, proj, proj, proj, la, s0, gn)


def _gla_decode_kernel(q_ref, k_ref, v_ref, r_ref, la_ref, s0_ref, gn_ref, y_ref, so_ref):
    bb, H, _, DK = q_ref.shape
    for b in range(bb):
        for h in range(H):
            q = q_ref[b, h] * (DK ** -0.5)
            S = (_row_to_col(jnp.exp(la_ref[b, h]), DK) * s0_ref[b, h]
                 + _row_to_col(k_ref[b, h], DK) * v_ref[b, h])
            so_ref[b, h] = S
            o = jnp.sum(_row_to_col(q, DK) * S, axis=0, keepdims=True)
            y_ref[b, h] = _head_norm_gate(o, gn_ref[h], r_ref[b, h]).astype(y_ref.dtype)


def _gla_decode(q, k, v, r, la, s0, gn):
    B, H, DK, DV = s0.shape
    bb = 4 if B % 4 == 0 else 1
    vec = lambda d: pl.BlockSpec((bb, H, 1, d), lambda b: (b, 0, 0, 0))
    mat = pl.BlockSpec((bb, H, DK, DV), lambda b: (b, 0, 0, 0))
    return pl.pallas_call(
        _gla_decode_kernel,
        name="gla_decode",
        grid=(B // bb,),
        in_specs=[vec(DK), vec(DK), vec(DV), vec(DV), vec(DK), mat,
                  pl.BlockSpec((H, 1, DV), lambda b: (0, 0, 0))],
        out_specs=[vec(DV), mat],
        out_shape=[jax.ShapeDtypeStruct((B, H, 1, DV), BF16),
                   jax.ShapeDtypeStruct((B, H, DK, DV), F32)],
        compiler_params=_params("arbitrary"),
    )(q, k, v, r, la, s0, gn)


def _outproj_kernel(ya_ref, yb_ref, wa_ref, wb_ref, x_ref, gt_ref, o_ref, wab, wbb):
    @pl.when((pl.program_id(1) == 0) & (pl.program_id(2) == 0))
    def _():
        wab[...] = wa_ref[...].astype(BF16)
        wbb[...] = wb_ref[...].astype(BF16)

    mix = (jnp.dot(ya_ref[0], wab[...], preferred_element_type=F32)
           + jnp.dot(yb_ref[0], wbb[...], preferred_element_type=F32))
    o_ref[0] = x_ref[0] + gt_ref[0] * mix


def _outproj(ya, yb, w, x, gt):
    B, L, D = x.shape
    Ka, Kb = ya.shape[2], yb.shape[2]
    assert Ka == Kb
    tm = _pick(L, 512)
    tn = _pick(D, 1024, LANES)
    gspec = (pl.BlockSpec((1, 1, tn), lambda j, b, l: (b, 0, j)) if gt.shape[1] == 1
             else pl.BlockSpec((1, tm, tn), lambda j, b, l: (b, l, j)))
    once = pl.Buffered(1)
    return pl.pallas_call(
        _outproj_kernel,
        name="outproj",
        grid=(D // tn, B, L // tm),
        in_specs=[pl.BlockSpec((1, tm, Ka), lambda j, b, l: (b, l, 0)),
                  pl.BlockSpec((1, tm, Kb), lambda j, b, l: (b, l, 0)),
                  pl.BlockSpec((Ka, tn), lambda j, b, l: (0, j), pipeline_mode=once),
                  pl.BlockSpec((Kb, tn), lambda j, b, l: (1, j), pipeline_mode=once),
                  pl.BlockSpec((1, tm, tn), lambda j, b, l: (b, l, j)),
                  gspec],
        out_specs=pl.BlockSpec((1, tm, tn), lambda j, b, l: (b, l, j)),
        out_shape=jax.ShapeDtypeStruct((B, L, D), F32),
        scratch_shapes=[pltpu.VMEM((Ka, tn), BF16), pltpu.VMEM((Kb, tn), BF16)],
        compiler_params=_params("arbitrary", "arbitrary", "arbitrary"),
    )(ya, yb, w, w, x, gt)


def _topk_rows(problems, ids, kk, fill):
    ss = [p[0] for p in problems]
    for j in range(kk):
        for n, (_, val_ref, idx_ref) in enumerate(problems):
            s = ss[n]
            m = jnp.max(s, axis=0, keepdims=True)
            idx = jnp.min(jnp.where(s == m, ids, fill), axis=0, keepdims=True)
            val_ref[j:j + 1, :] = m
            idx_ref[j:j + 1, :] = idx
            ss[n] = jnp.where(ids == idx, -jnp.inf, s)


def _sort_network(n):
    out = []
    p = 1
    while p < n:
        k = p
        while k >= 1:
            for j in range(k % p, n - k, 2 * k):
                for i in range(min(k, n - j - k)):
                    if (i + j) // (2 * p) == (i + j + k) // (2 * p):
                        out.append((i + j, i + j + k))
            k //= 2
        p *= 2
    return out


def _topk_sorted(problems, kk):
    big = 3.0e38
    state = []
    for s, _, _ in problems:
        n, t = s.shape
        ng = n // SUBLANES
        V = [s[j * SUBLANES:(j + 1) * SUBLANES] for j in range(ng)]
        base = lax.broadcasted_iota(jnp.int32, (SUBLANES, t), 0).astype(F32)
        I = [base + float(j * SUBLANES) for j in range(ng)]
        for a, b in _sort_network(ng):
            swap = V[b] > V[a]
            V[a], V[b] = jnp.where(swap, V[b], V[a]), jnp.where(swap, V[a], V[b])
            I[a], I[b] = jnp.where(swap, I[b], I[a]), jnp.where(swap, I[a], I[b])
        V = V[:kk] + [jnp.full((SUBLANES, t), -jnp.inf, F32)]
        I = I[:kk] + [jnp.full((SUBLANES, t), big, F32)]
        state.append([V, I, None, jnp.zeros((1, t), F32)])
    for k in range(kk + 1):
        for st, (_, val_ref, idx_ref) in zip(state, problems):
            V, I, prev, flag = st
            m = jnp.max(V[0], axis=0, keepdims=True)
            if prev is not None:
                flag = jnp.maximum(flag, jnp.where(m == prev, 1.0, 0.0))
            if k < kk:
                idx = jnp.min(jnp.where(V[0] == m, I[0], big), axis=0, keepdims=True)
                val_ref[k:k + 1, :] = m
                idx_ref[k:k + 1, :] = idx
                won = I[0] == idx
                V = [jnp.where(won, V[p + 1], V[p]) for p in range(len(V) - 1)]
                I = [jnp.where(won, I[p + 1], I[p]) for p in range(len(I) - 1)]
            st[:] = [V, I, m, flag]
    flag = state[0][3]
    for st in state[1:]:
        flag = jnp.maximum(flag, st[3])
    return flag


def _pair_layout(K):
    segs, ids, r0 = [], [], 0
    b = 0
    while b < K and K // (b + 1) > 1:
        na = K // (b + 1)
        segs.append((r0, (0, na), (b, b + 1)))
        rows = -(-na // SUBLANES) * SUBLANES
        ids += [a * K + b for a in range(na)] + [K * K] * (rows - na)
        r0 += rows
        b += 1
    if b < K:
        nb = K - b
        segs.append((r0, (0, 1), (b, K)))
        rows = -(-nb // SUBLANES) * SUBLANES
        ids += list(range(b, K)) + [K * K] * (rows - nb)
        r0 += rows
    return segs, ids, r0


def _peer_topk_kernel(q_ref, keys_ref, pid_ref, e1_ref, e2_ref, g_ref, sv, si, cand, sc, pos,
                      *, segs):
    tT = q_ref.shape[0]
    H, _, NK, dq = keys_ref.shape
    K = PEER_TOPK
    hp = sv.shape[0]
    ids = lax.broadcasted_iota(jnp.int32, (NK, tT), 0).astype(F32)
    pids = pid_ref[...]

    def heads(i, carry):
        halves = []
        for n in range(hp):
            for p in range(2):
                h = i * hp + n
                off = pl.multiple_of((2 * h + p) * dq, dq)
                qp = q_ref[:, pl.ds(off, dq)].astype(BF16)
                s = lax.dot_general(keys_ref[h, p].astype(BF16), qp, (((1,), (1,)), ((), ())),
                                    preferred_element_type=F32)
                halves.append((s, sv.at[n, p], si.at[n, p]))
        tied = _topk_sorted(halves, K)

        @pl.when(jnp.max(tied) > 0.0)
        def _():
            _topk_rows(halves, ids, K, float(NK))

        pairs = []
        for n in range(hp):
            v0, v1 = sv[n, 0], sv[n, 1]
            cand[n] = jnp.full(cand.shape[1:], -jnp.inf, F32)
            for r0, (a0, a1), (b0, b1) in segs:
                rows = max(a1 - a0, b1 - b0)
                cand[n, r0:r0 + rows, :] = v0[a0:a1] + v1[b0:b1]
            pairs.append((cand[n], sc.at[n], pos.at[n]))
        _topk_rows(pairs, pids, K, float(K * K))
        for n in range(hp):
            h = i * hp + n
            pa = jnp.floor(pos[n] * (1.0 / K))
            pb = pos[n] - pa * K
            i0, i1 = si[n, 0], si[n, 1]
            e1 = jnp.zeros((K, tT), F32)
            e2 = jnp.zeros((K, tT), F32)
            for a in range(K):
                e1 = jnp.where(pa == a, i0[a:a + 1], e1)
                e2 = jnp.where(pb == a, i1[a:a + 1], e2)
            scv = sc[n]
            ex = jnp.exp(scv - jnp.max(scv, axis=0, keepdims=True))
            e1_ref[h] = e1
            e2_ref[h] = e2
            g_ref[h] = ex / jnp.sum(ex, axis=0, keepdims=True)
        return carry

    lax.fori_loop(0, H // hp, heads, 0)


def _peer_topk(q, keys):
    T = q.shape[0]
    H, _, NK, dq = keys.shape
    K = PEER_TOPK
    tT = _pick(T, LANES, LANES)
    hp = 8 if H % 8 == 0 else 1
    segs, ids, nr = _pair_layout(K)
    pids = jnp.broadcast_to(jnp.asarray(ids, F32)[:, None], (nr, tT))
    out = jax.ShapeDtypeStruct((H, K, T), F32)
    ospec = pl.BlockSpec((H, K, tT), lambda i: (0, 0, i))
    return pl.pallas_call(
        functools.partial(_peer_topk_kernel, segs=segs),
        name="peer_topk",
        grid=(T // tT,),
        in_specs=[pl.BlockSpec((tT, 2 * H * dq), lambda i: (i, 0)),
                  pl.BlockSpec((H, 2, NK, dq), lambda i: (0, 0, 0, 0)),
                  pl.BlockSpec((nr, tT), lambda i: (0, 0))],
        out_specs=[ospec, ospec, ospec],
        out_shape=[out, out, out],
        scratch_shapes=[pltpu.VMEM((hp, 2, K, tT), F32), pltpu.VMEM((hp, 2, K, tT), F32),
                        pltpu.VMEM((hp, nr, tT), F32), pltpu.VMEM((hp, K, tT), F32),
                        pltpu.VMEM((hp, K, tT), F32)],
        compiler_params=_params("arbitrary"),
    )(q, keys, pids)


def _peer_gate_kernel(e1_ref, e2_ref, g_ref, o_ref, e1t, e2t, gt, gbuf, *, unroll, pitch):
    NS, tT = e1_ref.shape
    NK = gbuf.shape[1]
    e1t[...] = e1_ref[...].T
    e2t[...] = e2_ref[...].T
    gt[...] = g_ref[...].T
    kid = lax.broadcasted_iota(jnp.int32, (NK, NS), 0).astype(F32)

    def body(i, carry):
        for j in range(unroll):
            t = i * unroll + j
            a = jnp.where(kid == e1t[pl.ds(t, 1), :], gt[pl.ds(t, 1), :], 0.0).astype(BF16)
            b = jnp.where(kid == e2t[pl.ds(t, 1), :], 1.0, 0.0).astype(BF16)
            gbuf[pl.ds(pl.multiple_of(t * pitch, SUBLANES), NK), :] = lax.dot_general(
                a, b, (((1,), (1,)), ((), ())), preferred_element_type=F32)
        return carry

    lax.fori_loop(0, tT // unroll, body, 0)
    for i1 in range(NK):
        o_ref[:, i1 * NK:(i1 + 1) * NK] = gbuf[pl.ds(i1, tT, stride=pitch), :].astype(o_ref.dtype)


def _peer_gates(e1, e2, g, NK):
    NS, T = e1.shape
    tT = _pick(T, 128, LANES)
    unroll = 64 if tT % 64 == 0 else 1
    pitch = NK + SUBLANES
    ispec = pl.BlockSpec((NS, tT), lambda i: (0, i))
    return pl.pallas_call(
        functools.partial(_peer_gate_kernel, unroll=unroll, pitch=pitch),
        name="peer_gate",
        grid=(T // tT,),
        in_specs=[ispec, ispec, ispec],
        out_specs=pl.BlockSpec((tT, NK * NK), lambda i: (i, 0)),
        out_shape=jax.ShapeDtypeStruct((T, NK * NK), BF16),
        scratch_shapes=[pltpu.VMEM((tT, NS), F32)] * 3 + [pltpu.VMEM((tT * pitch, NK), F32)],
        compiler_params=_params("arbitrary"),
    )(e1, e2, g)


def _row_groups(rows):
    for n in (5, 4, 2):
        if rows % (16 * n) == 0:
            return n
    return 1


def _peer_act_kernel(x_ref, u_ref, g_ref, p_ref, *, nsplit):
    ub = u_ref[...].astype(BF16)
    rs = x_ref.shape[0] // nsplit
    for r in range(nsplit):
        rows = slice(r * rs, (r + 1) * rs)
        act = lax.dot_general(x_ref[rows, :], ub, (((1,), (1,)), ((), ())),
                              preferred_element_type=F32)
        p_ref[rows, :] = (g_ref[rows, :].astype(F32) * jax.nn.gelu(act)).astype(p_ref.dtype)


def _peer_act(x, u, G):
    T, D = x.shape
    E = u.shape[0]
    tT = _pick(T, 2080, 16)
    tE = _pick(E, 512, LANES)
    nsplit = _row_groups(tT)
    return pl.pallas_call(
        functools.partial(_peer_act_kernel, nsplit=nsplit),
        name="peer_act",
        grid=(T // tT, E // tE),
        in_specs=[pl.BlockSpec((tT, D), lambda i, e: (i, 0), pipeline_mode=pl.Buffered(1)),
                  pl.BlockSpec((tE, D), lambda i, e: (e, 0)),
                  pl.BlockSpec((tT, tE), lambda i, e: (i, e))],
        out_specs=pl.BlockSpec((tT, tE), lambda i, e: (i, e)),
        out_shape=jax.ShapeDtypeStruct((T, E), BF16),
        compiler_params=_params("arbitrary", "arbitrary"),
    )(x, u, G)


def _peer_out_kernel(p_ref, v_ref, o_ref, *, nsplit):
    e = pl.program_id(1)

    @pl.when(e == 0)
    def _():
        o_ref[...] = jnp.zeros(o_ref.shape, o_ref.dtype)

    vb = v_ref[...].astype(BF16)
    rs = p_ref.shape[0] // nsplit
    for r in range(nsplit):
        rows = slice(r * rs, (r + 1) * rs)
        o_ref[rows, :] += jnp.dot(p_ref[rows, :], vb, preferred_element_type=F32)


def _peer_out(p, v):
    T, E = p.shape
    D = v.shape[1]
    tT = _pick(T, 1664, 16)
    tE = _pick(E, 512, LANES)
    nsplit = 8 if tT % 128 == 0 else _row_groups(tT)
    return pl.pallas_call(
        functools.partial(_peer_out_kernel, nsplit=nsplit),
        name="peer_out",
        grid=(T // tT, E // tE),
        in_specs=[pl.BlockSpec((tT, tE), lambda i, e: (i, e)),
                  pl.BlockSpec((tE, D), lambda i, e: (e, 0))],
        out_specs=pl.BlockSpec((tT, D), lambda i, e: (i, 0), pipeline_mode=pl.Buffered(1)),
        out_shape=jax.ShapeDtypeStruct((T, D), F32),
        compiler_params=pltpu.CompilerParams(dimension_semantics=("arbitrary", "arbitrary"),
                                             vmem_limit_bytes=BIG_VMEM_LIMIT),
    )(p, v)


def _resid_kernel(x_ref, y_ref, gt_ref, o_ref):
    o_ref[0] = x_ref[0] + gt_ref[0] * y_ref[...]


def _resid_norm_kernel(x_ref, y_ref, gt_ref, g_ref, o_ref):
    x = x_ref[0] + gt_ref[0] * y_ref[...]
    o_ref[0] = x * lax.rsqrt(jnp.mean(x * x, axis=-1, keepdims=True) + RMS_EPS) * g_ref[...]


def _resid(x, y, row0, gt, g=None):
    B, L, D = x.shape
    tl = _pick(L, 256)
    assert row0 % tl == 0
    lb = L // tl
    blk = pl.BlockSpec((1, tl, D), lambda b, l: (b, l, 0))
    in_specs = [blk, pl.BlockSpec((tl, D), lambda b, l: (row0 // tl + b * lb + l, 0)),
                _mod_spec(gt, tl, D)]
    args = [x, y, gt]
    if g is not None:
        in_specs.append(pl.BlockSpec((1, D), lambda b, l: (0, 0)))
        args.append(g)
    return pl.pallas_call(
        _resid_kernel if g is None else _resid_norm_kernel,
        name="resid",
        grid=(B, L // tl),
        in_specs=in_specs,
        out_specs=blk,
        out_shape=jax.ShapeDtypeStruct((B, L, D), F32),
        compiler_params=_params("arbitrary", "arbitrary"),
    )(*args)


def kernel(x_prompt, x_sample, state_conv, state_lru_h, state_gla, c_prompt, c_sample, w_ada, b_ada, g_mix, w_in, conv_w, conv_b, lru_w_a, lru_b_a, lru_w_x, lru_b_x, lru_lambda, gla_w_alpha, gla_b_alpha, gla_g_norm, w_out, g_ffn, peer_w_q, peer_sub_keys, peer_u, peer_v, g_final):
    depth = w_ada.shape[0]
    Bp, L, D = x_prompt.shape
    Bs = x_sample.shape[0]
    assert x_sample.shape[1] == 1
    W = conv_w.shape[2]
    nt = conv_w.shape[1] - 1
    _, _, H, DK, DV = state_gla.shape
    rank = gla_w_alpha.shape[1]
    PH, _, NK, _ = peer_sub_keys.shape[1:]
    n_main = 2 * W + 2 * H * DK + 2 * H * DV
    assert w_in.shape[2] == n_main + rank and rank <= LANES
    offs = (2 * W, 2 * W + H * DK, 2 * W + 2 * H * DK, 2 * W + 2 * H * DK + H * DV)
    Tp = Bp * L

    R = Bp + Bs
    Rp = -(-R // SUBLANES) * SUBLANES
    c_all = jnp.pad(jnp.concatenate([c_prompt, c_sample], axis=0), ((0, Rp - R), (0, 0)))

    x_p = x_prompt
    x_s = x_sample.reshape(1, Bs, D)
    outs = [[] for _ in range(6)]
    for l in range(depth):
        ada = _ada(c_all, w_ada[l], b_ada[l][None])
        mods_p = [ada[:Bp, i * D:(i + 1) * D][:, None, :] for i in range(6)]
        mods_s = [ada[Bp:R, i * D:(i + 1) * D][None] for i in range(6)]

        w_in_t = jnp.transpose(w_in[l])
        in_proj = functools.partial(_matmul, w=w_in_t, N=n_main, w_rows_are_outputs=True,
                                    vmem_limit=BIG_VMEM_LIMIT)
        norm_gate = functools.partial(_norm_mod_gate, g=g_mix[l][None], w_t=w_in_t, row0=n_main,
                                      w2=gla_w_alpha[l], b2=gla_b_alpha[l][None])
        wa = lru_w_a[l].astype(BF16)
        wx = lru_w_x[l].astype(BF16)
        lru_vecs = (conv_w[l], conv_b[l][None], wa, lru_b_a[l][None], wx, lru_b_x[l][None],
                    lru_lambda[l][None])
        gn = gla_g_norm[l]

        sh1, sc1, gt1, sh2, sc2, gt2 = mods_p
        hn, la = norm_gate(x_p, sc=sc1, sh=sh1)
        proj = in_proj(hn.reshape(Tp, D)).reshape(Bp, L, n_main)
        y_lru, h_p, conv_p = _lru_prompt(
            proj, jnp.zeros((Bp, nt, W), F32), jnp.zeros((Bp, 1, W), F32), *lru_vecs, W)
        y_gla, s_p = _gla_prompt(proj, la, jnp.zeros((Bp, H, DK, DV), F32), gn[None], offs)
        x1_p = _outproj(y_lru, y_gla, w_out[l], x_p, gt1)
        sc2_p, sh2_p, gt2_p = sc2, sh2, gt2

        sh1, sc1, gt1, sh2, sc2, gt2 = mods_s
        hn, la = norm_gate(x_s, sc=sc1, sh=sh1)
        proj = in_proj(hn.reshape(Bs, D))
        la = la.reshape(Bs, H * DK)
        tail = jnp.transpose(state_conv[l], (1, 0, 2))
        y_lru, h_s = _lru_decode(proj, tail, state_lru_h[l], *lru_vecs, W)
        conv_s = jnp.concatenate([state_conv[l][:, 1:], proj[:, None, :W]], axis=1)
        qo, ko, vo, ro = offs
        heads = lambda a, d: a.reshape(Bs, H, 1, d)
        y_gla, s_s = _gla_decode(
            heads(proj[:, qo:ko], DK), heads(proj[:, ko:vo], DK), heads(proj[:, vo:ro], DV),
            heads(proj[:, ro:n_main], DV), heads(la, DK), state_gla[l], gn.reshape(H, 1, DV))
        x1_s = _outproj(y_lru[None], y_gla.reshape(1, Bs, H * DV), w_out[l], x_s, gt1)
        gt2_s = gt2

        hn2 = _norm_mod_cat(x1_p, x1_s, g_ffn[l][None], sc2_p, sh2_p, sc2, sh2, BF16)
        T = Tp + Bs
        q = _matmul(hn2, peer_w_q[l])
        e1, e2, gate = _peer_topk(q, peer_sub_keys[l])
        G = _peer_gates(e1.reshape(PH * PEER_TOPK, T), e2.reshape(PH * PEER_TOPK, T),
                        gate.reshape(PH * PEER_TOPK, T), NK)
        y = _peer_out(_peer_act(hn2, peer_u[l], G), peer_v[l])
        last = l == depth - 1
        x_p = _resid(x1_p, y, 0, gt2_p, g_final[None] if last else None)
        x_s = _resid(x1_s, y, Tp, gt2_s, g_final[None] if last else None)

        for lst, val in zip(outs, (conv_p, h_p.reshape(Bp, W), s_p, conv_s, h_s, s_s)):
            lst.append(val)

    return (x_p, x_s.reshape(Bs, 1, D)) + tuple(jnp.stack(o) for o in outs)
```

```python
import functools

import jax
import jax.numpy as jnp
from jax import lax
from jax.experimental import pallas as pl
from jax.experimental.pallas import tpu as pltpu

F32 = jnp.float32
BF16 = jnp.bfloat16

RMS_EPS = 1e-6
RG_C = 8.0
GLA_TAU = 16.0
PEER_TOPK = 16
GLA_CHUNK = 128
GLA_SUB = 8
LANES = 128
SUBLANES = 8
VMEM_LIMIT = 56 * 1024 * 1024
BIG_VMEM_LIMIT = 60 * 1024 * 1024


def _params(*sem):
    return pltpu.CompilerParams(dimension_semantics=sem, vmem_limit_bytes=VMEM_LIMIT)


def _pick(n, pref, mult=SUBLANES):
    best = None
    for d in range(mult, min(n, pref) + 1, mult):
        if n % d == 0:
            best = d
    return n if best is None else best


def _softplus(y):
    return jnp.maximum(y, 0.0) + jnp.log1p(jnp.exp(-jnp.abs(y)))


def _log_sigmoid(z):
    return jnp.minimum(z, 0.0) - jnp.log1p(jnp.exp(-jnp.abs(z)))


def _row_to_col(row, n):
    r = lax.broadcasted_iota(jnp.int32, (n, n), 0)
    c = lax.broadcasted_iota(jnp.int32, (n, n), 1)
    return jnp.sum(jnp.where(r == c, jnp.broadcast_to(row, (n, n)), 0.0), axis=1, keepdims=True)


def _ada_kernel(c_ref, w_ref, b_ref, o_ref):
    a = jax.nn.silu(c_ref[...]).astype(BF16)
    o_ref[...] = jnp.dot(a, w_ref[...].astype(BF16), preferred_element_type=F32) + b_ref[...]


def _ada(c, w, b):
    R, D = c.shape
    N = w.shape[1]
    tn = _pick(N, 1024, LANES)
    return pl.pallas_call(
        _ada_kernel,
        name="ada",
        grid=(N // tn,),
        in_specs=[pl.BlockSpec((R, D), lambda j: (0, 0)),
                  pl.BlockSpec((D, tn), lambda j: (0, j)),
                  pl.BlockSpec((1, tn), lambda j: (0, j))],
        out_specs=pl.BlockSpec((R, tn), lambda j: (0, j)),
        out_shape=jax.ShapeDtypeStruct((R, N), F32),
        compiler_params=_params("arbitrary"),
    )(c, w, b)


def _norm_mod_gate_kernel(x_ref, g_ref, sc_ref, sh_ref, w1_ref, w2_ref, b2_ref, o_ref, la_ref):
    x = x_ref[0]
    y = x * lax.rsqrt(jnp.mean(x * x, axis=-1, keepdims=True) + RMS_EPS) * g_ref[...]
    hn = (y * (1.0 + sc_ref[0]) + sh_ref[0]).astype(o_ref.dtype)
    o_ref[0] = hn
    a = lax.dot_general(hn, w1_ref[...].astype(BF16), (((1,), (1,)), ((), ())),
                        preferred_element_type=F32)
    z = jnp.dot(a.astype(BF16), w2_ref[...].astype(BF16), preferred_element_type=F32) + b2_ref[...]
    la_ref[0] = _log_sigmoid(z) / GLA_TAU


def _mod_spec(m, tl, D):
    if m.shape[1] == 1:
        return pl.BlockSpec((1, 1, D), lambda b, l: (b, 0, 0))
    return pl.BlockSpec((1, tl, D), lambda b, l: (b, l, 0))


def _norm_mod_gate(x, g, sc, sh, w_t, row0, w2, b2):
    B, L, D = x.shape
    R, N = w2.shape
    assert row0 % R == 0 and R % SUBLANES == 0
    tl = _pick(L, 512)
    const = lambda b, l: (0, 0)
    return pl.pallas_call(
        _norm_mod_gate_kernel,
        name="norm_mod_gate",
        grid=(B, L // tl),
        in_specs=[pl.BlockSpec((1, tl, D), lambda b, l: (b, l, 0)),
                  pl.BlockSpec((1, D), const),
                  _mod_spec(sc, tl, D), _mod_spec(sh, tl, D),
                  pl.BlockSpec((R, D), lambda b, l: (row0 // R, 0)),
                  pl.BlockSpec((R, N), const),
                  pl.BlockSpec((1, N), const)],
        out_specs=[pl.BlockSpec((1, tl, D), lambda b, l: (b, l, 0)),
                   pl.BlockSpec((1, tl, N), lambda b, l: (b, l, 0))],
        out_shape=[jax.ShapeDtypeStruct((B, L, D), BF16), jax.ShapeDtypeStruct((B, L, N), F32)],
        compiler_params=_params("arbitrary", "arbitrary"),
    )(x, g, sc, sh, w_t, w2, b2)


def _norm_mod_cat_kernel(xp_ref, xs_ref, g_ref, scp_ref, shp_ref, scs_ref, shs_ref, o_ref, *, n_p):
    i = pl.program_id(0)

    def f(x, sc, sh):
        y = x * lax.rsqrt(jnp.mean(x * x, axis=-1, keepdims=True) + RMS_EPS) * g_ref[...]
        return (y * (1.0 + sc) + sh).astype(o_ref.dtype)

    @pl.when(i < n_p)
    def _():
        o_ref[...] = f(xp_ref[0], scp_ref[0], shp_ref[0])

    @pl.when(i >= n_p)
    def _():
        o_ref[...] = f(xs_ref[0], scs_ref[0], shs_ref[0])


def _norm_mod_cat(xp, xs, g, scp, shp, scs, shs, out_dtype):
    Bp, L, D = xp.shape
    Bs = xs.shape[1]
    tl = _pick(Bs, 256)
    assert L % tl == 0
    lb = L // tl
    n_p = Bp * lb
    n_s = Bs // tl
    pb = lambda i: jnp.minimum(i, n_p - 1)
    sb = lambda i: jnp.maximum(i - n_p, 0)
    pmod = pl.BlockSpec((1, 1, D), lambda i: (pb(i) // lb, 0, 0))
    smod = pl.BlockSpec((1, tl, D), lambda i: (0, sb(i), 0))
    return pl.pallas_call(
        functools.partial(_norm_mod_cat_kernel, n_p=n_p),
        name="norm_mod_cat",
        grid=(n_p + n_s,),
        in_specs=[pl.BlockSpec((1, tl, D), lambda i: (pb(i) // lb, pb(i) % lb, 0)),
                  pl.BlockSpec((1, tl, D), lambda i: (0, sb(i), 0)),
                  pl.BlockSpec((1, D), lambda i: (0, 0)),
                  pmod, pmod, smod, smod],
        out_specs=pl.BlockSpec((tl, D), lambda i: (i, 0)),
        out_shape=jax.ShapeDtypeStruct((Bp * L + Bs, D), out_dtype),
        compiler_params=_params("arbitrary"),
    )(xp, xs, g, scp, shp, scs, shs)


def _mm_kernel(x_ref, w_ref, o_ref, wb, *, w_rows_are_outputs):
    @pl.when(pl.program_id(1) == 0)
    def _():
        wb[...] = w_ref[...].astype(BF16)

    dims = (((1,), (1,)), ((), ())) if w_rows_are_outputs else (((1,), (0,)), ((), ()))
    o_ref[...] = lax.dot_general(x_ref[...], wb[...], dims, preferred_element_type=F32)


def _matmul(x, w, N=None, *, w_rows_are_outputs=False, tm_pref=512, tn_pref=1024,
            vmem_limit=VMEM_LIMIT):
    M, K = x.shape
    n_all = w.shape[0] if w_rows_are_outputs else w.shape[1]
    N = n_all if N is None else N
    tm = _pick(M, tm_pref, 16)
    tn = _pick(N, tn_pref, LANES)
    wblk = (tn, K) if w_rows_are_outputs else (K, tn)
    wmap = (lambda j, i: (j, 0)) if w_rows_are_outputs else (lambda j, i: (0, j))
    return pl.pallas_call(
        functools.partial(_mm_kernel, w_rows_are_outputs=w_rows_are_outputs),
        name="matmul",
        grid=(N // tn, M // tm),
        in_specs=[pl.BlockSpec((tm, K), lambda j, i: (i, 0)), pl.BlockSpec(wblk, wmap)],
        out_specs=pl.BlockSpec((tm, tn), lambda j, i: (i, j)),
        out_shape=jax.ShapeDtypeStruct((M, N), F32),
        scratch_shapes=[pltpu.VMEM(wblk, BF16)],
        compiler_params=pltpu.CompilerParams(dimension_semantics=("arbitrary", "arbitrary"),
                                             vmem_limit_bytes=vmem_limit),
    )(x, w)


def _lru_gates(xc, wa_ref, ba, wx_ref, bx, lam):
    nb, bd = wa_ref.shape[0], wa_ref.shape[1]
    xcb = xc.astype(BF16)
    ra, ri = [], []
    for n in range(nb):
        xs = xcb[:, n * bd:(n + 1) * bd]
        ra.append(jnp.dot(xs, wa_ref[n], preferred_element_type=F32))
        ri.append(jnp.dot(xs, wx_ref[n], preferred_element_type=F32))
    r = jax.nn.sigmoid(jnp.concatenate(ra, axis=1) + ba)
    i = jax.nn.sigmoid(jnp.concatenate(ri, axis=1) + bx)
    log_a = -RG_C * r * _softplus(-lam)
    a = jnp.exp(log_a)
    u = jnp.sqrt(-jnp.tanh(log_a) * (a * a + 1.0)) * i * xc
    return a, u


def _lru_prompt_kernel(xb_ref, gb_ref, tail_ref, h0_ref, cw_ref, cb_ref, wa_ref, ba_ref, wx_ref,
                       bx_ref, lam_ref, y_ref, hl_ref, conv_ref, buf, hcar):
    l = pl.program_id(2)
    nl = pl.num_programs(2)
    tl, tw = xb_ref.shape[1], xb_ref.shape[2]
    nt = tail_ref.shape[1]
    base = SUBLANES

    @pl.when(l == 0)
    def _():
        buf[base - nt:base, :] = tail_ref[0]
        hcar[...] = h0_ref[0]

    xb = xb_ref[0]
    buf[base:base + tl, :] = xb
    cw = cw_ref[...]
    xc = cb_ref[...] + xb * cw[nt:nt + 1]
    for j in range(nt):
        xc = xc + buf[base - nt + j:base - nt + j + tl, :] * cw[j:j + 1]
    new_tail = buf[base + tl - nt:base + tl, :]

    a, u = _lru_gates(xc, wa_ref, ba_ref[...], wx_ref, bx_ref[...], lam_ref[...])

    ng = tl // SUBLANES
    A = a.reshape(ng, SUBLANES, tw)
    Bv = u.reshape(ng, SUBLANES, tw)
    row = lax.broadcasted_iota(jnp.int32, (1, SUBLANES, tw), 1)
    s = 1
    while s < SUBLANES:
        As = pltpu.roll(A, s, axis=1)
        Bs = pltpu.roll(Bv, s, axis=1)
        m = row >= s
        Bv = jnp.where(m, A * Bs + Bv, Bv)
        A = jnp.where(m, A * As, A)
        s *= 2
    h = hcar[...]
    outs = []
    for g in range(ng):
        hg = A[g] * h + Bv[g]
        outs.append(hg)
        h = hg[SUBLANES - 1:SUBLANES]
    hseq = jnp.concatenate(outs, axis=0)
    hcar[...] = h
    buf[base - nt:base, :] = new_tail
    y_ref[0] = (hseq * jax.nn.gelu(gb_ref[0])).astype(y_ref.dtype)

    @pl.when(l == nl - 1)
    def _():
        hl_ref[0] = h
        conv_ref[0] = new_tail


def _lru_prompt(proj, tail, h0, cw, cb, wa, ba, wx, bx, lam, W):
    B, L, _ = proj.shape
    nt = tail.shape[1]
    bd = wa.shape[1]
    tw = _pick(W, 512, bd)
    tl = _pick(L, 512)
    nwb = W // tw
    col = lambda b, j, l: (0, j)
    return pl.pallas_call(
        _lru_prompt_kernel,
        name="lru_prompt",
        grid=(B, nwb, L // tl),
        in_specs=[pl.BlockSpec((1, tl, tw), lambda b, j, l: (b, l, j)),
                  pl.BlockSpec((1, tl, tw), lambda b, j, l: (b, l, j + nwb)),
                  pl.BlockSpec((1, nt, tw), lambda b, j, l: (b, 0, j)),
                  pl.BlockSpec((1, 1, tw), lambda b, j, l: (b, 0, j)),
                  pl.BlockSpec((nt + 1, tw), col),
                  pl.BlockSpec((1, tw), col),
                  pl.BlockSpec((tw // bd, bd, bd), lambda b, j, l: (j, 0, 0)),
                  pl.BlockSpec((1, tw), col),
                  pl.BlockSpec((tw // bd, bd, bd), lambda b, j, l: (j, 0, 0)),
                  pl.BlockSpec((1, tw), col),
                  pl.BlockSpec((1, tw), col)],
        out_specs=[pl.BlockSpec((1, tl, tw), lambda b, j, l: (b, l, j)),
                   pl.BlockSpec((1, 1, tw), lambda b, j, l: (b, 0, j)),
                   pl.BlockSpec((1, nt, tw), lambda b, j, l: (b, 0, j))],
        out_shape=[jax.ShapeDtypeStruct((B, L, W), BF16),
                   jax.ShapeDtypeStruct((B, 1, W), F32),
                   jax.ShapeDtypeStruct((B, nt, W), F32)],
        scratch_shapes=[pltpu.VMEM((tl + SUBLANES, tw), F32), pltpu.VMEM((1, tw), F32)],
        compiler_params=_params("arbitrary", "arbitrary", "arbitrary"),
    )(proj, proj, tail, h0, cw, cb, wa, ba, wx, bx, lam)


def _lru_decode_kernel(xb_ref, gb_ref, tail_ref, h0_ref, cw_ref, cb_ref, wa_ref, ba_ref, wx_ref,
                       bx_ref, lam_ref, y_ref, h_ref):
    nt = tail_ref.shape[0]
    xb = xb_ref[...]
    cw = cw_ref[...]
    xc = cb_ref[...] + xb * cw[nt:nt + 1]
    for j in range(nt):
        xc = xc + tail_ref[j] * cw[j:j + 1]
    a, u = _lru_gates(xc, wa_ref, ba_ref[...], wx_ref, bx_ref[...], lam_ref[...])
    h = a * h0_ref[...] + u
    h_ref[...] = h
    y_ref[...] = (h * jax.nn.gelu(gb_ref[...])).astype(y_ref.dtype)


def _lru_decode(proj, tail, h0, cw, cb, wa, ba, wx, bx, lam, W):
    B = proj.shape[0]
    nt = tail.shape[0]
    bd = wa.shape[1]
    tw = _pick(W, 512, bd)
    nwb = W // tw
    col = lambda j: (0, j)
    return pl.pallas_call(
        _lru_decode_kernel,
        name="lru_decode",
        grid=(nwb,),
        in_specs=[pl.BlockSpec((B, tw), col),
                  pl.BlockSpec((B, tw), lambda j: (0, j + nwb)),
                  pl.BlockSpec((nt, B, tw), lambda j: (0, 0, j)),
                  pl.BlockSpec((B, tw), col),
                  pl.BlockSpec((nt + 1, tw), col),
                  pl.BlockSpec((1, tw), col),
                  pl.BlockSpec((tw // bd, bd, bd), lambda j: (j, 0, 0)),
                  pl.BlockSpec((1, tw), col),
                  pl.BlockSpec((tw // bd, bd, bd), lambda j: (j, 0, 0)),
                  pl.BlockSpec((1, tw), col),
                  pl.BlockSpec((1, tw), col)],
        out_specs=[pl.BlockSpec((B, tw), col), pl.BlockSpec((B, tw), col)],
        out_shape=[jax.ShapeDtypeStruct((B, W), BF16), jax.ShapeDtypeStruct((B, W), F32)],
        compiler_params=_params("arbitrary"),
    )(proj, proj, tail, h0, cw, cb, wa, ba, wx, bx, lam)


def _head_norm_gate(o, gn, r):
    on = o * lax.rsqrt(jnp.mean(o * o, axis=-1, keepdims=True) + RMS_EPS) * gn
    return on * jax.nn.silu(r)


def _gla_chunk(q, k, v, r, g, gn, S, att):
    C, DK = q.shape
    sub = min(GLA_SUB, C)
    q = q * (DK ** -0.5)
    vb = v.astype(BF16)
    rr = lax.broadcasted_iota(jnp.int32, (C, C), 0)
    cc = lax.broadcasted_iota(jnp.int32, (C, C), 1)
    b = jnp.dot((rr >= cc).astype(F32), g, precision=lax.Precision.HIGHEST,
                preferred_element_type=F32)
    b_last = b[C - 1:C]

    inter = jnp.dot((q * jnp.exp(b)).astype(BF16), S[...].astype(BF16), preferred_element_type=F32)

    lane = lax.broadcasted_iota(jnp.int32, (sub, sub), 1)
    srow = lax.broadcasted_iota(jnp.int32, (sub, sub), 0)
    for I in range(C // sub):
        lo = I * sub
        qI, kI, bI = q[lo:lo + sub], k[lo:lo + sub], b[lo:lo + sub]
        acc = jnp.zeros((sub, sub), F32)
        for s in range(sub):
            w = qI * kI[s:s + 1] * jnp.exp(jnp.minimum(bI - bI[s:s + 1], 0.0))
            acc = jnp.where(lane == s, jnp.sum(w, axis=1, keepdims=True), acc)
        att[lo:lo + sub, lo:lo + sub] = jnp.where(srow >= lane, acc, 0.0)
        if lo > 0:
            ref = b[lo - 1:lo]
            qt = (qI * jnp.exp(bI - ref)).astype(BF16)
            kt = (k[0:lo] * jnp.exp(ref - b[0:lo])).astype(BF16)
            att[lo:lo + sub, 0:lo] = lax.dot_general(qt, kt, (((1,), (1,)), ((), ())),
                                                     preferred_element_type=F32)
        if lo + sub < C:
            att[lo:lo + sub, lo + sub:C] = jnp.zeros((sub, C - lo - sub), F32)
    o = inter + jnp.dot(att[...].astype(BF16), vb, preferred_element_type=F32)

    kh = (k * jnp.exp(b_last - b)).astype(BF16)
    upd = lax.dot_general(kh, vb, (((0,), (0,)), ((), ())), preferred_element_type=F32)
    S[...] = _row_to_col(jnp.exp(b_last), DK) * S[...] + upd
    return _head_norm_gate(o, gn, r)


def _gla_prompt_kernel(q_ref, k_ref, v_ref, r_ref, la_ref, s0_ref, gn_ref, y_ref, so_ref, S, att):
    c = pl.program_id(2)
    nc = pl.num_programs(2)
    hb, DK, DV = S.shape

    @pl.when(c == 0)
    def _():
        S[...] = s0_ref[0]

    for h in range(hb):
        kk = slice(h * DK, (h + 1) * DK)
        vv = slice(h * DV, (h + 1) * DV)
        y = _gla_chunk(q_ref[0, :, kk], k_ref[0, :, kk], v_ref[0, :, vv], r_ref[0, :, vv],
                       la_ref[0, :, kk], gn_ref[:, vv], S.at[h], att.at[h])
        y_ref[0, :, vv] = y.astype(y_ref.dtype)

    @pl.when(c == nc - 1)
    def _():
        so_ref[0] = S[...]


def _gla_prompt(proj, la, s0, gn, offs):
    B, L, _ = proj.shape
    _, H, DK, DV = s0.shape
    C = _pick(L, GLA_CHUNK)
    hb = 4 if H % 4 == 0 else (2 if H % 2 == 0 else 1)
    wk, wv = hb * DK, hb * DV
    qo, ko, vo, ro = offs
    assert all(o % wk == 0 for o in (qo, ko)) and all(o % wv == 0 for o in (vo, ro))
    return pl.pallas_call(
        _gla_prompt_kernel,
        name="gla_prompt",
        grid=(B, H // hb, L // C),
        in_specs=[pl.BlockSpec((1, C, wk), lambda b, h, c: (b, c, qo // wk + h)),
                  pl.BlockSpec((1, C, wk), lambda b, h, c: (b, c, ko // wk + h)),
                  pl.BlockSpec((1, C, wv), lambda b, h, c: (b, c, vo // wv + h)),
                  pl.BlockSpec((1, C, wv), lambda b, h, c: (b, c, ro // wv + h)),
                  pl.BlockSpec((1, C, wk), lambda b, h, c: (b, c, h)),
                  pl.BlockSpec((1, hb, DK, DV), lambda b, h, c: (b, h, 0, 0)),
                  pl.BlockSpec((1, wv), lambda b, h, c: (0, h))],
        out_specs=[pl.BlockSpec((1, C, wv), lambda b, h, c: (b, c, h)),
                   pl.BlockSpec((1, hb, DK, DV), lambda b, h, c: (b, h, 0, 0))],
        out_shape=[jax.ShapeDtypeStruct((B, L, H * DV), BF16),
                   jax.ShapeDtypeStruct((B, H, DK, DV), F32)],
        scratch_shapes=[pltpu.VMEM((hb, DK, DV), F32), pltpu.VMEM((hb, C, C), F32)],
        compiler_params=_params("arbitrary", "arbitrary", "arbitrary"),
    )(proj, proj, proj, proj, la, s0, gn)


def _gla_decode_kernel(q_ref, k_ref, v_ref, r_ref, la_ref, s0_ref, gn_ref, y_ref, so_ref):
    bb, H, _, DK = q_ref.shape
    for b in range(bb):
        for h in range(H):
            q = q_ref[b, h] * (DK ** -0.5)
            S = (_row_to_col(jnp.exp(la_ref[b, h]), DK) * s0_ref[b, h]
                 + _row_to_col(k_ref[b, h], DK) * v_ref[b, h])
            so_ref[b, h] = S
            o = jnp.sum(_row_to_col(q, DK) * S, axis=0, keepdims=True)
            y_ref[b, h] = _head_norm_gate(o, gn_ref[h], r_ref[b, h]).astype(y_ref.dtype)


def _gla_decode(q, k, v, r, la, s0, gn):
    B, H, DK, DV = s0.shape
    bb = 4 if B % 4 == 0 else 1
    vec = lambda d: pl.BlockSpec((bb, H, 1, d), lambda b: (b, 0, 0, 0))
    mat = pl.BlockSpec((bb, H, DK, DV), lambda b: (b, 0, 0, 0))
    return pl.pallas_call(
        _gla_decode_kernel,
        name="gla_decode",
        grid=(B // bb,),
        in_specs=[vec(DK), vec(DK), vec(DV), vec(DV), vec(DK), mat,
                  pl.BlockSpec((H, 1, DV), lambda b: (0, 0, 0))],
        out_specs=[vec(DV), mat],
        out_shape=[jax.ShapeDtypeStruct((B, H, 1, DV), BF16),
                   jax.ShapeDtypeStruct((B, H, DK, DV), F32)],
        compiler_params=_params("arbitrary"),
    )(q, k, v, r, la, s0, gn)


def _outproj_kernel(ya_ref, yb_ref, wa_ref, wb_ref, x_ref, gt_ref, o_ref, wab, wbb):
    @pl.when((pl.program_id(1) == 0) & (pl.program_id(2) == 0))
    def _():
        wab[...] = wa_ref[...].astype(BF16)
        wbb[...] = wb_ref[...].astype(BF16)

    mix = (jnp.dot(ya_ref[0], wab[...], preferred_element_type=F32)
           + jnp.dot(yb_ref[0], wbb[...], preferred_element_type=F32))
    o_ref[0] = x_ref[0] + gt_ref[0] * mix


def _outproj(ya, yb, w, x, gt):
    B, L, D = x.shape
    Ka, Kb = ya.shape[2], yb.shape[2]
    assert Ka == Kb
    tm = _pick(L, 512)
    tn = _pick(D, 1024, LANES)
    gspec = (pl.BlockSpec((1, 1, tn), lambda j, b, l: (b, 0, j)) if gt.shape[1] == 1
             else pl.BlockSpec((1, tm, tn), lambda j, b, l: (b, l, j)))
    once = pl.Buffered(1)
    return pl.pallas_call(
        _outproj_kernel,
        name="outproj",
        grid=(D // tn, B, L // tm),
        in_specs=[pl.BlockSpec((1, tm, Ka), lambda j, b, l: (b, l, 0)),
                  pl.BlockSpec((1, tm, Kb), lambda j, b, l: (b, l, 0)),
                  pl.BlockSpec((Ka, tn), lambda j, b, l: (0, j), pipeline_mode=once),
                  pl.BlockSpec((Kb, tn), lambda j, b, l: (1, j), pipeline_mode=once),
                  pl.BlockSpec((1, tm, tn), lambda j, b, l: (b, l, j)),
                  gspec],
        out_specs=pl.BlockSpec((1, tm, tn), lambda j, b, l: (b, l, j)),
        out_shape=jax.ShapeDtypeStruct((B, L, D), F32),
        scratch_shapes=[pltpu.VMEM((Ka, tn), BF16), pltpu.VMEM((Kb, tn), BF16)],
        compiler_params=_params("arbitrary", "arbitrary", "arbitrary"),
    )(ya, yb, w, w, x, gt)


def _topk_rows(problems, ids, kk, fill):
    ss = [p[0] for p in problems]
    for j in range(kk):
        for n, (_, val_ref, idx_ref) in enumerate(problems):
            s = ss[n]
            m = jnp.max(s, axis=0, keepdims=True)
            idx = jnp.min(jnp.where(s == m, ids, fill), axis=0, keepdims=True)
            val_ref[j:j + 1, :] = m
            idx_ref[j:j + 1, :] = idx
            ss[n] = jnp.where(ids == idx, -jnp.inf, s)


def _sort_network(n):
    out = []
    p = 1
    while p < n:
        k = p
        while k >= 1:
            for j in range(k % p, n - k, 2 * k):
                for i in range(min(k, n - j - k)):
                    if (i + j) // (2 * p) == (i + j + k) // (2 * p):
                        out.append((i + j, i + j + k))
            k //= 2
        p *= 2
    return out


def _topk_sorted(problems, kk):
    big = 3.0e38
    state = []
    for s, _, _ in problems:
        n, t = s.shape
        ng = n // SUBLANES
        V = [s[j * SUBLANES:(j + 1) * SUBLANES] for j in range(ng)]
        base = lax.broadcasted_iota(jnp.int32, (SUBLANES, t), 0).astype(F32)
        I = [base + float(j * SUBLANES) for j in range(ng)]
        for a, b in _sort_network(ng):
            swap = V[b] > V[a]
            V[a], V[b] = jnp.where(swap, V[b], V[a]), jnp.where(swap, V[a], V[b])
            I[a], I[b] = jnp.where(swap, I[b], I[a]), jnp.where(swap, I[a], I[b])
        V = V[:kk] + [jnp.full((SUBLANES, t), -jnp.inf, F32)]
        I = I[:kk] + [jnp.full((SUBLANES, t), big, F32)]
        state.append([V, I, None, jnp.zeros((1, t), F32)])
    for k in range(kk + 1):
        for st, (_, val_ref, idx_ref) in zip(state, problems):
            V, I, prev, flag = st
            m = jnp.max(V[0], axis=0, keepdims=True)
            if prev is not None:
                flag = jnp.maximum(flag, jnp.where(m == prev, 1.0, 0.0))
            if k < kk:
                idx = jnp.min(jnp.where(V[0] == m, I[0], big), axis=0, keepdims=True)
                val_ref[k:k + 1, :] = m
                idx_ref[k:k + 1, :] = idx
                won = I[0] == idx
                V = [jnp.where(won, V[p + 1], V[p]) for p in range(len(V) - 1)]
                I = [jnp.where(won, I[p + 1], I[p]) for p in range(len(I) - 1)]
            st[:] = [V, I, m, flag]
    flag = state[0][3]
    for st in state[1:]:
        flag = jnp.maximum(flag, st[3])
    return flag


def _pair_layout(K):
    segs, ids, r0 = [], [], 0
    b = 0
    while b < K and K // (b + 1) > 1:
        na = K // (b + 1)
        segs.append((r0, (0, na), (b, b + 1)))
        rows = -(-na // SUBLANES) * SUBLANES
        ids += [a * K + b for a in range(na)] + [K * K] * (rows - na)
        r0 += rows
        b += 1
    if b < K:
        nb = K - b
        segs.append((r0, (0, 1), (b, K)))
        rows = -(-nb // SUBLANES) * SUBLANES
        ids += list(range(b, K)) + [K * K] * (rows - nb)
        r0 += rows
    return segs, ids, r0


def _peer_topk_kernel(q_ref, keys_ref, pid_ref, e1_ref, e2_ref, g_ref, sv, si, cand, sc, pos,
                      *, segs):
    tT = q_ref.shape[0]
    H, _, NK, dq = keys_ref.shape
    K = PEER_TOPK
    hp = sv.shape[0]
    ids = lax.broadcasted_iota(jnp.int32, (NK, tT), 0).astype(F32)
    pids = pid_ref[...]

    def heads(i, carry):
        halves = []
        for n in range(hp):
            for p in range(2):
                h = i * hp + n
                off = pl.multiple_of((2 * h + p) * dq, dq)
                qp = q_ref[:, pl.ds(off, dq)].astype(BF16)
                s = lax.dot_general(keys_ref[h, p].astype(BF16), qp, (((1,), (1,)), ((), ())),
                                    preferred_element_type=F32)
                halves.append((s, sv.at[n, p], si.at[n, p]))
        tied = _topk_sorted(halves, K)

        @pl.when(jnp.max(tied) > 0.0)
        def _():
            _topk_rows(halves, ids, K, float(NK))

        pairs = []
        for n in range(hp):
            v0, v1 = sv[n, 0], sv[n, 1]
            cand[n] = jnp.full(cand.shape[1:], -jnp.inf, F32)
            for r0, (a0, a1), (b0, b1) in segs:
                rows = max(a1 - a0, b1 - b0)
                cand[n, r0:r0 + rows, :] = v0[a0:a1] + v1[b0:b1]
            pairs.append((cand[n], sc.at[n], pos.at[n]))
        _topk_rows(pairs, pids, K, float(K * K))
        for n in range(hp):
            h = i * hp + n
            pa = jnp.floor(pos[n] * (1.0 / K))
            pb = pos[n] - pa * K
            i0, i1 = si[n, 0], si[n, 1]
            e1 = jnp.zeros((K, tT), F32)
            e2 = jnp.zeros((K, tT), F32)
            for a in range(K):
                e1 = jnp.where(pa == a, i0[a:a + 1], e1)
                e2 = jnp.where(pb == a, i1[a:a + 1], e2)
            scv = sc[n]
            ex = jnp.exp(scv - jnp.max(scv, axis=0, keepdims=True))
            e1_ref[h] = e1
            e2_ref[h] = e2
            g_ref[h] = ex / jnp.sum(ex, axis=0, keepdims=True)
        return carry

    lax.fori_loop(0, H // hp, heads, 0)


def _peer_topk(q, keys):
    T = q.shape[0]
    H, _, NK, dq = keys.shape
    K = PEER_TOPK
    tT = _pick(T, LANES, LANES)
    hp = 8 if H % 8 == 0 else 1
    segs, ids, nr = _pair_layout(K)
    pids = jnp.broadcast_to(jnp.asarray(ids, F32)[:, None], (nr, tT))
    out = jax.ShapeDtypeStruct((H, K, T), F32)
    ospec = pl.BlockSpec((H, K, tT), lambda i: (0, 0, i))
    return pl.pallas_call(
        functools.partial(_peer_topk_kernel, segs=segs),
        name="peer_topk",
        grid=(T // tT,),
        in_specs=[pl.BlockSpec((tT, 2 * H * dq), lambda i: (i, 0)),
                  pl.BlockSpec((H, 2, NK, dq), lambda i: (0, 0, 0, 0)),
                  pl.BlockSpec((nr, tT), lambda i: (0, 0))],
        out_specs=[ospec, ospec, ospec],
        out_shape=[out, out, out],
        scratch_shapes=[pltpu.VMEM((hp, 2, K, tT), F32), pltpu.VMEM((hp, 2, K, tT), F32),
                        pltpu.VMEM((hp, nr, tT), F32), pltpu.VMEM((hp, K, tT), F32),
                        pltpu.VMEM((hp, K, tT), F32)],
        compiler_params=_params("arbitrary"),
    )(q, keys, pids)


def _peer_gate_kernel(e1_ref, e2_ref, g_ref, o_ref, e1t, e2t, gt, gbuf, *, unroll, pitch):
    NS, tT = e1_ref.shape
    NK = gbuf.shape[1]
    e1t[...] = e1_ref[...].T
    e2t[...] = e2_ref[...].T
    gt[...] = g_ref[...].T
    kid = lax.broadcasted_iota(jnp.int32, (NK, NS), 0).astype(F32)

    def body(i, carry):
        for j in range(unroll):
            t = i * unroll + j
            a = jnp.where(kid == e1t[pl.ds(t, 1), :], gt[pl.ds(t, 1), :], 0.0).astype(BF16)
            b = jnp.where(kid == e2t[pl.ds(t, 1), :], 1.0, 0.0).astype(BF16)
            gbuf[pl.ds(pl.multiple_of(t * pitch, SUBLANES), NK), :] = lax.dot_general(
                a, b, (((1,), (1,)), ((), ())), preferred_element_type=F32)
        return carry

    lax.fori_loop(0, tT // unroll, body, 0)
    for i1 in range(NK):
        o_ref[:, i1 * NK:(i1 + 1) * NK] = gbuf[pl.ds(i1, tT, stride=pitch), :].astype(o_ref.dtype)


def _peer_gates(e1, e2, g, NK):
    NS, T = e1.shape
    tT = _pick(T, 128, LANES)
    unroll = 64 if tT % 64 == 0 else 1
    pitch = NK + SUBLANES
    ispec = pl.BlockSpec((NS, tT), lambda i: (0, i))
    return pl.pallas_call(
        functools.partial(_peer_gate_kernel, unroll=unroll, pitch=pitch),
        name="peer_gate",
        grid=(T // tT,),
        in_specs=[ispec, ispec, ispec],
        out_specs=pl.BlockSpec((tT, NK * NK), lambda i: (i, 0)),
        out_shape=jax.ShapeDtypeStruct((T, NK * NK), BF16),
        scratch_shapes=[pltpu.VMEM((tT, NS), F32)] * 3 + [pltpu.VMEM((tT * pitch, NK), F32)],
        compiler_params=_params("arbitrary"),
    )(e1, e2, g)


def _row_groups(rows):
    for n in (5, 4, 2):
        if rows % (16 * n) == 0:
            return n
    return 1


def _peer_act_kernel(x_ref, u_ref, g_ref, p_ref, *, nsplit):
    ub = u_ref[...].astype(BF16)
    rs = x_ref.shape[0] // nsplit
    for r in range(nsplit):
        rows = slice(r * rs, (r + 1) * rs)
        act = lax.dot_general(x_ref[rows, :], ub, (((1,), (1,)), ((), ())),
                              preferred_element_type=F32)
        p_ref[rows, :] = (g_ref[rows, :].astype(F32) * jax.nn.gelu(act)).astype(p_ref.dtype)


def _peer_act(x, u, G):
    T, D = x.shape
    E = u.shape[0]
    tT = _pick(T, 2080, 16)
    tE = _pick(E, 512, LANES)
    nsplit = _row_groups(tT)
    return pl.pallas_call(
        functools.partial(_peer_act_kernel, nsplit=nsplit),
        name="peer_act",
        grid=(T // tT, E // tE),
        in_specs=[pl.BlockSpec((tT, D), lambda i, e: (i, 0), pipeline_mode=pl.Buffered(1)),
                  pl.BlockSpec((tE, D), lambda i, e: (e, 0)),
                  pl.BlockSpec((tT, tE), lambda i, e: (i, e))],
        out_specs=pl.BlockSpec((tT, tE), lambda i, e: (i, e)),
        out_shape=jax.ShapeDtypeStruct((T, E), BF16),
        compiler_params=_params("arbitrary", "arbitrary"),
    )(x, u, G)


def _peer_out_kernel(p_ref, v_ref, o_ref, *, nsplit):
    e = pl.program_id(1)

    @pl.when(e == 0)
    def _():
        o_ref[...] = jnp.zeros(o_ref.shape, o_ref.dtype)

    vb = v_ref[...].astype(BF16)
    rs = p_ref.shape[0] // nsplit
    for r in range(nsplit):
        rows = slice(r * rs, (r + 1) * rs)
        o_ref[rows, :] += jnp.dot(p_ref[rows, :], vb, preferred_element_type=F32)


def _peer_out(p, v):
    T, E = p.shape
    D = v.shape[1]
    tT = _pick(T, 1664, 16)
    tE = _pick(E, 512, LANES)
    nsplit = 8 if tT % 128 == 0 else _row_groups(tT)
    return pl.pallas_call(
        functools.partial(_peer_out_kernel, nsplit=nsplit),
        name="peer_out",
        grid=(T // tT, E // tE),
        in_specs=[pl.BlockSpec((tT, tE), lambda i, e: (i, e)),
                  pl.BlockSpec((tE, D), lambda i, e: (e, 0))],
        out_specs=pl.BlockSpec((tT, D), lambda i, e: (i, 0), pipeline_mode=pl.Buffered(1)),
        out_shape=jax.ShapeDtypeStruct((T, D), F32),
        compiler_params=pltpu.CompilerParams(dimension_semantics=("arbitrary", "arbitrary"),
                                             vmem_limit_bytes=BIG_VMEM_LIMIT),
    )(p, v)


def _resid_kernel(x_ref, y_ref, gt_ref, o_ref):
    o_ref[0] = x_ref[0] + gt_ref[0] * y_ref[...]


def _resid_norm_kernel(x_ref, y_ref, gt_ref, g_ref, o_ref):
    x = x_ref[0] + gt_ref[0] * y_ref[...]
    o_ref[0] = x * lax.rsqrt(jnp.mean(x * x, axis=-1, keepdims=True) + RMS_EPS) * g_ref[...]


def _resid(x, y, row0, gt, g=None):
    B, L, D = x.shape
    tl = _pick(L, 256)
    assert row0 % tl == 0
    lb = L // tl
    blk = pl.BlockSpec((1, tl, D), lambda b, l: (b, l, 0))
    in_specs = [blk, pl.BlockSpec((tl, D), lambda b, l: (row0 // tl + b * lb + l, 0)),
                _mod_spec(gt, tl, D)]
    args = [x, y, gt]
    if g is not None:
        in_specs.append(pl.BlockSpec((1, D), lambda b, l: (0, 0)))
        args.append(g)
    return pl.pallas_call(
        _resid_kernel if g is None else _resid_norm_kernel,
        name="resid",
        grid=(B, L // tl),
        in_specs=in_specs,
        out_specs=blk,
        out_shape=jax.ShapeDtypeStruct((B, L, D), F32),
        compiler_params=_params("arbitrary", "arbitrary"),
    )(*args)


def kernel(x_prompt, x_sample, state_conv, state_lru_h, state_gla, c_prompt, c_sample, w_ada, b_ada, g_mix, w_in, conv_w, conv_b, lru_w_a, lru_b_a, lru_w_x, lru_b_x, lru_lambda, gla_w_alpha, gla_b_alpha, gla_g_norm, w_out, g_ffn, peer_w_q, peer_sub_keys, peer_u, peer_v, g_final):
    depth = w_ada.shape[0]
    Bp, L, D = x_prompt.shape
    Bs = x_sample.shape[0]
    assert x_sample.shape[1] == 1
    W = conv_w.shape[2]
    nt = conv_w.shape[1] - 1
    _, _, H, DK, DV = state_gla.shape
    rank = gla_w_alpha.shape[1]
    PH, _, NK, _ = peer_sub_keys.shape[1:]
    n_main = 2 * W + 2 * H * DK + 2 * H * DV
    assert w_in.shape[2] == n_main + rank and rank <= LANES
    offs = (2 * W, 2 * W + H * DK, 2 * W + 2 * H * DK, 2 * W + 2 * H * DK + H * DV)
    Tp = Bp * L

    R = Bp + Bs
    Rp = -(-R // SUBLANES) * SUBLANES
    c_all = jnp.pad(jnp.concatenate([c_prompt, c_sample], axis=0), ((0, Rp - R), (0, 0)))

    x_p = x_prompt
    x_s = x_sample.reshape(1, Bs, D)
    outs = [[] for _ in range(6)]
    for l in range(depth):
        ada = _ada(c_all, w_ada[l], b_ada[l][None])
        mods_p = [ada[:Bp, i * D:(i + 1) * D][:, None, :] for i in range(6)]
        mods_s = [ada[Bp:R, i * D:(i + 1) * D][None] for i in range(6)]

        w_in_t = jnp.transpose(w_in[l])
        in_proj = functools.partial(_matmul, w=w_in_t, N=n_main, w_rows_are_outputs=True,
                                    vmem_limit=BIG_VMEM_LIMIT)
        norm_gate = functools.partial(_norm_mod_gate, g=g_mix[l][None], w_t=w_in_t, row0=n_main,
                                      w2=gla_w_alpha[l], b2=gla_b_alpha[l][None])
        wa = lru_w_a[l].astype(BF16)
        wx = lru_w_x[l].astype(BF16)
        lru_vecs = (conv_w[l], conv_b[l][None], wa, lru_b_a[l][None], wx, lru_b_x[l][None],
                    lru_lambda[l][None])
        gn = gla_g_norm[l]

        sh1, sc1, gt1, sh2, sc2, gt2 = mods_p
        hn, la = norm_gate(x_p, sc=sc1, sh=sh1)
        proj = in_proj(hn.reshape(Tp, D)).reshape(Bp, L, n_main)
        y_lru, h_p, conv_p = _lru_prompt(
            proj, jnp.zeros((Bp, nt, W), F32), jnp.zeros((Bp, 1, W), F32), *lru_vecs, W)
        y_gla, s_p = _gla_prompt(proj, la, jnp.zeros((Bp, H, DK, DV), F32), gn[None], offs)
        x1_p = _outproj(y_lru, y_gla, w_out[l], x_p, gt1)
        sc2_p, sh2_p, gt2_p = sc2, sh2, gt2

        sh1, sc1, gt1, sh2, sc2, gt2 = mods_s
        hn, la = norm_gate(x_s, sc=sc1, sh=sh1)
        proj = in_proj(hn.reshape(Bs, D))
        la = la.reshape(Bs, H * DK)
        tail = jnp.transpose(state_conv[l], (1, 0, 2))
        y_lru, h_s = _lru_decode(proj, tail, state_lru_h[l], *lru_vecs, W)
        conv_s = jnp.concatenate([state_conv[l][:, 1:], proj[:, None, :W]], axis=1)
        qo, ko, vo, ro = offs
        heads = lambda a, d: a.reshape(Bs, H, 1, d)
        y_gla, s_s = _gla_decode(
            heads(proj[:, qo:ko], DK), heads(proj[:, ko:vo], DK), heads(proj[:, vo:ro], DV),
            heads(proj[:, ro:n_main], DV), heads(la, DK), state_gla[l], gn.reshape(H, 1, DV))
        x1_s = _outproj(y_lru[None], y_gla.reshape(1, Bs, H * DV), w_out[l], x_s, gt1)
        gt2_s = gt2

        hn2 = _norm_mod_cat(x1_p, x1_s, g_ffn[l][None], sc2_p, sh2_p, sc2, sh2, BF16)
        T = Tp + Bs
        q = _matmul(hn2, peer_w_q[l])
        e1, e2, gate = _peer_topk(q, peer_sub_keys[l])
        G = _peer_gates(e1.reshape(PH * PEER_TOPK, T), e2.reshape(PH * PEER_TOPK, T),
                        gate.reshape(PH * PEER_TOPK, T), NK)
        y = _peer_out(_peer_act(hn2, peer_u[l], G), peer_v[l])
        last = l == depth - 1
        x_p = _resid(x1_p, y, 0, gt2_p, g_final[None] if last else None)
        x_s = _resid(x1_s, y, Tp, gt2_s, g_final[None] if last else None)

        for lst, val in zip(outs, (conv_p, h_p.reshape(Bp, W), s_p, conv_s, h_s, s_s)):
            lst.append(val)

    return (x_p, x_s.reshape(Bs, 1, D)) + tuple(jnp.stack(o) for o in outs)
```
